```python
import jax, jax.numpy as jnp
from jax import lax
import numpy as np

D_MODEL = 1024
BATCH = 2
SEQ = 8192
DEPTH = 4

GRID_W = 64
CTX_LEN = 256
N_EVEN = (DEPTH + 1) // 2
N_ODD = DEPTH // 2
EPS = 1e-6

ML_HEADS = 4
ML_DK = 64
ML_DV = 128
ML_CHUNK = 64
SC_WIDTH = D_MODEL // 2
CONV_K = 3
HYB_SIZES = (ML_HEADS * ML_DK, ML_HEADS * ML_DK, ML_HEADS * ML_DV, ML_HEADS * ML_DV, 4 * ML_HEADS, SC_WIDTH, SC_WIDTH, SC_WIDTH)
HYB_IN = sum(HYB_SIZES)
HYB_OUT = ML_HEADS * ML_DV + SC_WIDTH
ATT_HEADS = 16
KV_HEADS = 4
HEAD_DIM = 64
ATT_IN = (ATT_HEADS + 2 * KV_HEADS) * HEAD_DIM
Q_BLOCK = 128
ROPE_THETA = 10000.0
MLP_HIDDEN = 4 * D_MODEL

kernel_name = "hybrid_mlstm_conv_gqa_dit_trunk"


def rms_norm(x, g):
    xf = x.astype(jnp.float32)
    y = xf * lax.rsqrt(jnp.mean(xf * xf, axis=-1, keepdims=True) + EPS)
    return (y * g.astype(jnp.float32)).astype(x.dtype)


def sqrelu_mlp(h, w1, w2):
    return jnp.square(jax.nn.relu(h @ w1)) @ w2


def axial_rope_tables(n_rows):
    half = HEAD_DIM // 2
    inv = ROPE_THETA ** (-jnp.arange(0, half, 2, dtype=jnp.float32) / half)
    t = jnp.arange(n_rows * GRID_W)
    row = (t // GRID_W).astype(jnp.float32)
    col = (t % GRID_W).astype(jnp.float32)
    ang = jnp.concatenate([row[:, None] * inv, col[:, None] * inv], axis=-1)
    return jnp.cos(ang), jnp.sin(ang)


def _rotate_half(part, cs, sn):
    n = part.shape[-1] // 2
    p1, p2 = part[..., :n], part[..., n:]
    return jnp.concatenate([p1 * cs - p2 * sn, p2 * cs + p1 * sn], axis=-1)


def apply_axial_rope(x, cos, sin):
    half, quarter = HEAD_DIM // 2, HEAD_DIM // 4
    xf = x.astype(jnp.float32)
    cs, sn = cos[None, :, None, :], sin[None, :, None, :]
    row = _rotate_half(xf[..., :half], cs[..., :quarter], sn[..., :quarter])
    col = _rotate_half(xf[..., half:], cs[..., quarter:], sn[..., quarter:])
    return jnp.concatenate([row, col], axis=-1).astype(x.dtype)


def mlstm_chunk_states(k, v, ig, lf, state0):
    b = jnp.cumsum(lf, axis=-1)
    b_end = b[..., -1]
    logw = b_end[..., None] - b + ig
    m_loc = jnp.max(logw, axis=-1)
    w = jnp.exp(logw - m_loc[..., None])
    c_loc = jnp.einsum('bhcsv,bhcsk->bhcvk', v * w[..., None], k)
    n_loc = jnp.einsum('bhcs,bhcsk->bhck', w, k)

    def step(carry, inp):
        c_st, n_st, m_st = carry
        bl, ml, cl, nl = inp
        m_new = jnp.maximum(bl + m_st, ml)
        a = jnp.exp(bl + m_st - m_new)
        g = jnp.exp(ml - m_new)
        c_new = a[..., None, None] * c_st + g[..., None, None] * cl
        n_new = a[..., None] * n_st + g[..., None] * nl
        return (c_new, n_new, m_new), (c_st, n_st, m_st)

    xs = tuple(jnp.moveaxis(t, 2, 0) for t in (b_end, m_loc, c_loc, n_loc))
    final, starts = lax.scan(step, state0, xs)
    starts = tuple(jnp.moveaxis(t, 0, 2) for t in starts)
    return starts, final


def mlstm_chunk_outputs(q, k, v, ig, lf, starts):
    c0, n0, m0 = starts
    b = jnp.cumsum(lf, axis=-1)
    length = q.shape[-2]
    tri = jnp.tril(jnp.ones((length, length), dtype=bool))
    log_d = jnp.where(tri, b[..., :, None] - b[..., None, :] + ig[..., None, :], -jnp.inf)
    m_inter = b + m0[..., None]
    m = jnp.maximum(m_inter, jnp.max(log_d, axis=-1))
    d = jnp.exp(log_d - m[..., None])
    a = jnp.exp(m_inter - m)
    s = jnp.einsum('bhcjd,bhcsd->bhcjs', q, k) * d
    num = jnp.einsum('bhcjs,bhcsv->bhcjv', s, v) + a[..., None] * jnp.einsum('bhcvd,bhcjd->bhcjv', c0, q)
    den = jnp.sum(s, axis=-1) + a * jnp.einsum('bhcd,bhcjd->bhcj', n0, q)
    return num / jnp.maximum(jnp.abs(den), jnp.exp(-m))[..., None]


def mlstm_direction(q, k, v, ig, lf, state0, reverse, need_h):
    if reverse:
        q, k, v, ig, lf = (jnp.flip(t, axis=2) for t in (q, k, v, ig, lf))
    bsz, nh, n, _ = q.shape
    nc = n // ML_CHUNK
    chunk = lambda t: t.reshape(bsz, nh, nc, ML_CHUNK, *t.shape[3:])
    q, k, v, ig, lf = (chunk(t) for t in (q, k, v, ig, lf))
    starts, final = mlstm_chunk_states(k, v, ig, lf, state0)
    if not need_h:
        return None, final
    h = mlstm_chunk_outputs(q, k, v, ig, lf, starts).reshape(bsz, nh, n, ML_DV)
    if reverse:
        h = jnp.flip(h, axis=2)
    return h, final


def short_conv(u, w):
    return lax.conv_general_dilated(u, w[:, None, :].astype(u.dtype), window_strides=(1,),
                                    padding=[(CONV_K // 2, CONV_K // 2)],
                                    dimension_numbers=('NWC', 'WIO', 'NWC'),
                                    feature_group_count=u.shape[-1])


def hybrid_mixer(hc, hx, w_in, gate_b, mnorm_g, conv_w, w_out, with_ctx):
    splits = np.cumsum(HYB_SIZES)[:-1].tolist()

    def project(h):
        bsz, n, _ = h.shape
        q, k, v, o, g, gb, gc, u = jnp.split(h @ w_in, splits, axis=-1)
        heads = lambda t, dd: t.reshape(bsz, n, ML_HEADS, dd).transpose(0, 2, 1, 3).astype(jnp.float32)
        g = (g.astype(jnp.float32) + gate_b.astype(jnp.float32)).reshape(bsz, n, 4, ML_HEADS).transpose(2, 0, 3, 1)
        fwd = (g[0], jax.nn.log_sigmoid(g[1]))
        bwd = (g[2], jax.nn.log_sigmoid(g[3]))
        qkv = (heads(q, ML_DK) * ML_DK ** -0.5, heads(k, ML_DK), heads(v, ML_DV))
        return qkv, fwd, bwd, (o, gb, gc, u)

    def combine(h, side):
        o, gb, gc, u = side
        bsz, n, _ = o.shape
        hn = rms_norm(h.transpose(0, 2, 1, 3), mnorm_g.reshape(ML_HEADS, ML_DV))
        og = jax.nn.sigmoid(o.astype(jnp.float32)).reshape(bsz, n, ML_HEADS, ML_DV)
        m_out = (hn * og).reshape(bsz, n, ML_HEADS * ML_DV).astype(o.dtype)
        c_out = gb * short_conv(gc * u, conv_w)
        return jnp.concatenate([m_out, c_out], axis=-1) @ w_out

    qkv_c, fwd_c, bwd_c, side_c = project(hc)
    qkv_x, fwd_x, bwd_x, side_x = project(hx)
    bsz = hx.shape[0]
    zero = (jnp.zeros((bsz, ML_HEADS, ML_DV, ML_DK), jnp.float32),
            jnp.zeros((bsz, ML_HEADS, ML_DK), jnp.float32),
            jnp.zeros((bsz, ML_HEADS), jnp.float32))
    h_cf, s_f = mlstm_direction(*qkv_c, *fwd_c, zero, False, with_ctx)
    h_cb, s_b = mlstm_direction(*qkv_c, *bwd_c, zero, True, with_ctx)
    h_xf, _ = mlstm_direction(*qkv_x, *fwd_x, s_f, False, True)
    h_xb, _ = mlstm_direction(*qkv_x, *bwd_x, s_b, True, True)
    y_x = combine(h_xf + h_xb, side_x)
    y_c = combine(h_cf + h_cb, side_c) if with_ctx else None
    return y_c, y_x


def gqa_attend(q, k, v):
    bsz, nq, _, _ = q.shape
    qg = q.reshape(bsz, nq, KV_HEADS, ATT_HEADS // KV_HEADS, HEAD_DIM)
    s = jnp.einsum('bqhgd,bkhd->bhgqk', qg, k).astype(jnp.float32) * HEAD_DIM ** -0.5
    p = jax.nn.softmax(s, axis=-1).astype(v.dtype)
    o = jnp.einsum('bhgqk,bkhd->bqhgd', p, v)
    return o.reshape(bsz, nq, ATT_HEADS * HEAD_DIM)


def attention_mixer(hc, hx, w_in, q_g, k_g, w_out, cos, sin, with_ctx):
    bsz, n, _ = hx.shape
    nq = ATT_HEADS * HEAD_DIM
    heads = lambda p, nh: p.reshape(bsz, p.shape[1], nh, HEAD_DIM)

    def kv(p):
        k, v = jnp.split(p, 2, axis=-1)
        return rms_norm(heads(k, KV_HEADS), k_g), heads(v, KV_HEADS)

    px = hx @ w_in
    qx = apply_axial_rope(rms_norm(heads(px[..., :nq], ATT_HEADS), q_g), cos, sin)
    kx, vx = kv(px[..., nq:])
    kx = apply_axial_rope(kx, cos, sin)
    if with_ctx:
        pc = hc @ w_in
        qc = rms_norm(heads(pc[..., :nq], ATT_HEADS), q_g)
        kc, vc = kv(pc[..., nq:])
    else:
        kc, vc = kv(hc @ w_in[:, nq:])
    k_all = jnp.concatenate([kc, kx], axis=1)
    v_all = jnp.concatenate([vc, vx], axis=1)
    nb = n // Q_BLOCK
    q_blocks = qx.reshape(bsz, nb, Q_BLOCK, ATT_HEADS, HEAD_DIM).swapaxes(0, 1)
    o_x = lax.map(lambda qb: gqa_attend(qb, k_all, v_all), q_blocks)
    y_x = o_x.swapaxes(0, 1).reshape(bsz, n, nq) @ w_out
    y_c = gqa_attend(qc, kc, vc) @ w_out if with_ctx else None
    return y_c, y_x


def setup_inputs(seed: int = 0) -> dict:
    key = jax.random.key(seed)
    ks = jax.random.split(key, 20)
    nrm = lambda k, shape, s: jax.random.normal(k, shape, jnp.float32) * s
    d = D_MODEL
    ig_b = nrm(ks[11], (N_EVEN, 2, ML_HEADS), 0.1)
    fg_b = jnp.linspace(3.0, 6.0, ML_HEADS, dtype=jnp.float32)[None, None, :] + nrm(ks[12], (N_EVEN, 2, ML_HEADS), 0.1)
    return {
        "x": nrm(ks[0], (BATCH, SEQ, d), 1.0),
        "c": nrm(ks[1], (BATCH, d), 1.0),
        "ctx": nrm(ks[2], (BATCH, CTX_LEN, d), 1.0),
        "c_ctx": nrm(ks[3], (d,), 1.0),
        "ada_w": nrm(ks[4], (DEPTH, d, 6 * d), d ** -0.5),
        "ada_b": nrm(ks[5], (DEPTH, 6 * d), 0.02),
        "norm1_g": 1.0 + nrm(ks[6], (DEPTH, d), 0.05),
        "norm2_g": 1.0 + nrm(ks[7], (DEPTH, d), 0.05),
        "mlp_w1": nrm(ks[8], (DEPTH, d, MLP_HIDDEN), d ** -0.5),
        "mlp_w2": nrm(ks[9], (DEPTH, MLP_HIDDEN, d), MLP_HIDDEN ** -0.5),
        "hyb_w_in": nrm(ks[10], (N_EVEN, d, HYB_IN), d ** -0.5),
        "hyb_gate_b": jnp.stack([ig_b, fg_b], axis=2).reshape(N_EVEN, 4 * ML_HEADS),
        "mlstm_norm_g": 1.0 + nrm(ks[13], (N_EVEN, ML_HEADS * ML_DV), 0.05),
        "conv_w": nrm(ks[14], (N_EVEN, CONV_K, SC_WIDTH), CONV_K ** -0.5),
        "hyb_w_out": nrm(ks[15], (N_EVEN, HYB_OUT, d), HYB_OUT ** -0.5),
        "att_w_in": nrm(ks[16], (N_ODD, d, ATT_IN), d ** -0.5),
        "q_norm_g": 1.0 + nrm(ks[17], (N_ODD, HEAD_DIM), 0.05),
        "k_norm_g": 1.0 + nrm(ks[18], (N_ODD, HEAD_DIM), 0.05),
        "att_w_out": nrm(ks[19], (N_ODD, ATT_HEADS * HEAD_DIM, d), (ATT_HEADS * HEAD_DIM) ** -0.5),
    }


def reference(x, c, ctx, c_ctx, ada_w, ada_b, norm1_g, norm2_g, mlp_w1, mlp_w2, hyb_w_in, hyb_gate_b,
              mlstm_norm_g, conv_w, hyb_w_out, att_w_in, q_norm_g, k_norm_g, att_w_out):
    rows = x.shape[1] // GRID_W
    cos, sin = axial_rope_tables(rows)
    silu_c = jax.nn.silu(c)
    silu_cc = jax.nn.silu(c_ctx)
    s_c = ctx
    for layer in range(DEPTH):
        last = layer == DEPTH - 1
        mod_x = jnp.split((silu_c @ ada_w[layer] + ada_b[layer])[:, None, :], 6, axis=-1)
        mod_c = jnp.split((silu_cc @ ada_w[layer] + ada_b[layer])[None, None, :], 6, axis=-1)
        hx = rms_norm(x, norm1_g[layer]) * (1 + mod_x[1]) + mod_x[0]
        hc = rms_norm(s_c, norm1_g[layer]) * (1 + mod_c[1]) + mod_c[0]
        if layer % 2 == 0:
            e = layer // 2
            y_c, y_x = hybrid_mixer(hc, hx, hyb_w_in[e], hyb_gate_b[e], mlstm_norm_g[e], conv_w[e], hyb_w_out[e], not last)
        else:
            o = layer // 2
            y_c, y_x = attention_mixer(hc, hx, att_w_in[o], q_norm_g[o], k_norm_g[o], att_w_out[o], cos, sin, not last)
        x = x + mod_x[2] * y_x
        x = x + mod_x[5] * sqrelu_mlp(rms_norm(x, norm2_g[layer]) * (1 + mod_x[4]) + mod_x[3], mlp_w1[layer], mlp_w2[layer])
        if not last:
            s_c = s_c + mod_c[2] * y_c
            s_c = s_c + mod_c[5] * sqrelu_mlp(rms_norm(s_c, norm2_g[layer]) * (1 + mod_c[4]) + mod_c[3], mlp_w1[layer], mlp_w2[layer])
    return x
```

```python
import functools

import jax
import jax.numpy as jnp
from jax import lax
from jax.experimental import pallas as pl
from jax.experimental.pallas import tpu as pltpu

F32 = jnp.float32
BF16 = jnp.bfloat16

D_MODEL = 1024
DEPTH = 4
GRID_W = 64
EPS = 1e-6
ML_HEADS = 4
ML_DK = 64
ML_DV = 128
ML_CHUNK = 64
SC_WIDTH = D_MODEL // 2
CONV_K = 3
ATT_HEADS = 16
KV_HEADS = 4
HEAD_DIM = 64
GROUP = ATT_HEADS // KV_HEADS
ROPE_THETA = 10000.0
MLP_HIDDEN = 4 * D_MODEL

LANE = 128
SUBLANE = 8
TOKEN_TILE = 512
SEQ_BLOCK = 256
KV_TILE = TOKEN_TILE
VMEM_LIMIT_BYTES = 56 * 1024 * 1024

HYB_COLS = (4 * LANE, 4 * LANE, 4 * ML_DV, 4 * ML_DV, LANE, SC_WIDTH, SC_WIDTH, SC_WIDTH)
ATT_Q_COLS = ATT_HEADS * LANE
ATT_K_COLS = KV_HEADS * LANE


def _cparams(*sem):
    return pltpu.CompilerParams(dimension_semantics=sem, vmem_limit_bytes=VMEM_LIMIT_BYTES)


def _normmod(xf, g, shift, scale):
    var = jnp.mean(xf * xf, axis=-1, keepdims=True)
    return xf * lax.rsqrt(var + EPS) * g * (1.0 + scale) + shift


def _dot(a, b):
    return jnp.dot(a, b, preferred_element_type=F32)


def _dot_nt(a, b):
    return lax.dot_general(a, b, (((1,), (1,)), ((), ())), preferred_element_type=F32)


def _adaln_body(c_ref, w_ref, b_ref, o_ref):
    cv = c_ref[...]
    s = cv * jax.nn.sigmoid(cv)
    o_ref[0] = _dot(s.astype(BF16), w_ref[0].astype(BF16)) + b_ref[0]


def _adaln(cvec, ada_w, ada_b):
    depth, d, n = ada_w.shape
    bn = n // 4
    return pl.pallas_call(
        _adaln_body,
        grid=(depth, n // bn),
        in_specs=[pl.BlockSpec((SUBLANE, d), lambda l, j: (0, 0)),
                  pl.BlockSpec((1, d, bn), lambda l, j: (l, 0, j)),
                  pl.BlockSpec((1, 1, bn), lambda l, j: (l, 0, j))],
        out_specs=pl.BlockSpec((1, SUBLANE, bn), lambda l, j: (l, 0, j)),
        out_shape=jax.ShapeDtypeStruct((depth, SUBLANE, n), F32),
        compiler_params=_cparams("arbitrary", "arbitrary"),
        name="adaln",
    )(cvec, ada_w, ada_b.reshape(depth, 1, n))


def _hyb_in_body(x_ref, mod_ref, g_ref, w_ref, gb_ref, q_ref, k_ref, v_ref, o_ref, gate_ref, cb_ref, z_ref):
    d = D_MODEL
    h = _normmod(x_ref[...], g_ref[...], mod_ref[0, :, 0:d], mod_ref[0, :, d:2 * d]).astype(BF16)
    c0 = 0
    q_ref[...] = (_dot(h, w_ref[:, c0:c0 + 512]) * (ML_DK ** -0.5)).astype(BF16)
    c0 += 512
    k_ref[...] = _dot(h, w_ref[:, c0:c0 + 512]).astype(BF16)
    c0 += 512
    v_ref[...] = _dot(h, w_ref[:, c0:c0 + 512])
    c0 += 512
    o_ref[...] = _dot(h, w_ref[:, c0:c0 + 512])
    c0 += 512
    g = _dot(h, w_ref[:, c0:c0 + LANE]) + gb_ref[...]
    c0 += LANE
    lane = lax.broadcasted_iota(jnp.int32, g.shape, 1)
    is_forget = ((lane // ML_HEADS) % 2) == 1
    logsig = jnp.minimum(g, 0.0) - jnp.log1p(jnp.exp(-jnp.abs(g)))
    gate_ref[...] = jnp.where(is_forget, logsig, g)
    cb_ref[...] = _dot(h, w_ref[:, c0:c0 + 512])
    c0 += 512
    gc = _dot(h, w_ref[:, c0:c0 + 512])
    c0 += 512
    z_ref[...] = gc * _dot(h, w_ref[:, c0:c0 + 512])


def _hyb_in(stream, mod, g1, w, gate_b, n_xt):
    rows, d = stream.shape
    tm = TOKEN_TILE
    ncol = w.shape[1]
    row = lambda i: (i, 0)
    const = lambda i: (0, 0)
    outs = [(512, BF16), (512, BF16), (512, F32), (512, F32), (LANE, F32), (512, F32), (512, F32)]
    return pl.pallas_call(
        _hyb_in_body,
        grid=(rows // tm,),
        in_specs=[pl.BlockSpec((tm, d), row),
                  pl.BlockSpec((1, 1, 6 * d), lambda i: (i // n_xt, 0, 0)),
                  pl.BlockSpec((1, d), const),
                  pl.BlockSpec((d, ncol), const),
                  pl.BlockSpec((1, LANE), const)],
        out_specs=[pl.BlockSpec((tm, c), row) for c, _ in outs],
        out_shape=[jax.ShapeDtypeStruct((rows, c), t) for c, t in outs],
        compiler_params=_cparams("arbitrary"),
        name="hyb_in",
    )(stream, mod, g1, w, gate_b)


def _split3(a):
    hi = a.astype(BF16)
    r = a - hi.astype(F32)
    mid = r.astype(BF16)
    lo = (r - mid.astype(F32)).astype(BF16)
    return hi, mid, lo


def _mlstm_body(q_ref, k_ref, v_ref, gate_ref, h_ref, c_ref, m_ref, *, rev):
    L = ML_CHUNK

    @pl.when(pl.program_id(1) == 0)
    def _():
        c_ref[...] = jnp.zeros_like(c_ref)
        m_ref[...] = jnp.zeros_like(m_ref)

    jj = lax.broadcasted_iota(jnp.int32, (L, L), 0)
    ss = lax.broadcasted_iota(jnp.int32, (L, L), 1)
    lower = ss <= jj
    upper = ss >= jj
    mask = upper if rev else lower
    tc = mask.astype(BF16)
    tr = (lower if rev else upper).astype(BF16)
    ones_v = jnp.ones((L, ML_DV), F32)
    n_chunks = q_ref.shape[0] // L
    order = range(n_chunks - 1, -1, -1) if rev else range(n_chunks)
    lane0 = 2 * ML_HEADS if rev else 0

    for c in order:
        r0 = c * L
        G = gate_ref[r0:r0 + L, :]
        GT = G.T
        bc = sum(_dot(tc, p) for p in _split3(G))
        br = sum(_dot(p, tr) for p in _split3(GT))
        ig_sh = pltpu.roll(G, ML_HEADS, axis=1)
        b_end = bc[0:1, :] if rev else bc[L - 1:L, :]
        m0 = m_ref[0:1, :]
        logw = b_end - bc + ig_sh
        m_loc = jnp.max(logw, axis=0, keepdims=True)
        w_all = jnp.exp(logw - m_loc)
        m_new = jnp.maximum(b_end + m0, m_loc)
        a2_all = jnp.exp(b_end + m0 - m_new)
        g2_all = jnp.exp(m_loc - m_new)
        m_inter_all = bc + m0
        m_ref[0:1, :] = m_new

        for h in range(ML_HEADS):
            il = lane0 + h
            fl = il + ML_HEADS
            cs = slice(h * LANE, (h + 1) * LANE)
            qh = q_ref[r0:r0 + L, cs]
            kh = k_ref[r0:r0 + L, cs]
            vh = v_ref[r0:r0 + L, cs]
            bcol = bc[:, fl:fl + 1]
            brow = br[fl:fl + 1, :]
            igrow = GT[il:il + 1, :]
            logd = jnp.where(mask, bcol - brow + igrow, -jnp.inf)
            m_inter = m_inter_all[:, fl:fl + 1]
            m = jnp.maximum(m_inter, jnp.max(logd, axis=1, keepdims=True))
            dmat = jnp.exp(logd - m)
            a = jnp.exp(m_inter - m)
            s = _dot_nt(qh, kh) * dmat
            vaug = jnp.concatenate([vh, ones_v], axis=1)
            c_old = c_ref[h]
            num_aug = _dot(s.astype(BF16), vaug.astype(BF16)) + a * _dot(qh, c_old.astype(BF16))
            num = num_aug[:, :ML_DV]
            den = num_aug[:, ML_DV:]
            h_ref[r0:r0 + L, cs] = num / jnp.maximum(jnp.abs(den), jnp.exp(-m))
            w = w_all[:, fl:fl + 1]
            wv = (w * vaug).astype(BF16)
            khT = kh.astype(F32).T.astype(BF16)
            c_loc = _dot(khT, wv)
            c_ref[h] = a2_all[:, fl:fl + 1] * c_old + g2_all[:, fl:fl + 1] * c_loc


def _mlstm(q, k, v, gate, nb, t, rev):
    rows = q.shape[0]
    blk = SEQ_BLOCK
    nxb = t // blk
    ctx_blk0 = nb * nxb

    def imap(b, i):
        xi = (nxb - i) if rev else (i - 1)
        return (jnp.where(i == 0, ctx_blk0 + b, b * nxb + xi), 0)

    spec = lambda c: pl.BlockSpec((blk, c), imap)
    return pl.pallas_call(
        functools.partial(_mlstm_body, rev=rev),
        grid=(nb, nxb + 1),
        in_specs=[spec(512), spec(512), spec(512), spec(LANE)],
        out_specs=spec(512),
        out_shape=jax.ShapeDtypeStruct((rows, 512), F32),
        scratch_shapes=[pltpu.VMEM((ML_HEADS, LANE, 2 * ML_DV), F32), pltpu.VMEM((SUBLANE, LANE), F32)],
        compiler_params=_cparams("arbitrary", "arbitrary"),
        name="mlstm_bwd" if rev else "mlstm_fwd",
    )(q, k, v, gate)


def _hyb_comb_body(hf_ref, hb_ref, o_ref, cb_ref, z_ref, zp_ref, zn_ref, mg_ref, cw_ref, out_ref, *, starts, ends):
    tm = hf_ref.shape[0]
    i = pl.program_id(0)
    for h in range(ML_HEADS):
        cs = slice(h * ML_DV, (h + 1) * ML_DV)
        blk = hf_ref[:, cs] + hb_ref[:, cs]
        var = jnp.mean(blk * blk, axis=-1, keepdims=True)
        hn = blk * lax.rsqrt(var + EPS) * mg_ref[:, cs]
        out_ref[:, cs] = (hn * jax.nn.sigmoid(o_ref[:, cs])).astype(BF16)
    z = z_ref[...]
    loc = lax.broadcasted_iota(jnp.int32, (tm, 1), 0)
    row = loc + i * tm
    is_start = functools.reduce(jnp.logical_or, [row == r for r in starts])
    is_end = functools.reduce(jnp.logical_or, [row == r for r in ends])
    zprev = jnp.where(loc == 0, zp_ref[SUBLANE - 1:SUBLANE, :], pltpu.roll(z, 1, axis=0))
    zprev = jnp.where(is_start, 0.0, zprev)
    znext = jnp.where(loc == tm - 1, zn_ref[0:1, :], pltpu.roll(z, tm - 1, axis=0))
    znext = jnp.where(is_end, 0.0, znext)
    conv = cw_ref[0:1, :] * zprev + cw_ref[1:2, :] * z + cw_ref[2:3, :] * znext
    out_ref[:, 4 * ML_DV:] = (cb_ref[...] * conv).astype(BF16)


def _hyb_combine(hf, hb, o, cb, z, mg, cw, starts, ends):
    rows = hf.shape[0]
    tm = TOKEN_TILE
    per = tm // SUBLANE
    nblk = rows // SUBLANE
    row = lambda i: (i, 0)
    const = lambda i: (0, 0)
    return pl.pallas_call(
        functools.partial(_hyb_comb_body, starts=starts, ends=ends),
        grid=(rows // tm,),
        in_specs=[pl.BlockSpec((tm, 512), row)] * 5 + [
            pl.BlockSpec((SUBLANE, 512), lambda i: (jnp.maximum(i * per - 1, 0), 0)),
            pl.BlockSpec((SUBLANE, 512), lambda i: (jnp.minimum((i + 1) * per, nblk - 1), 0)),
            pl.BlockSpec((1, 512), const),
            pl.BlockSpec((SUBLANE, 512), const)],
        out_specs=pl.BlockSpec((tm, D_MODEL), row),
        out_shape=jax.ShapeDtypeStruct((rows, D_MODEL), BF16),
        compiler_params=_cparams("arbitrary"),
        name="hyb_combine",
    )(hf, hb, o, cb, z, z, z, mg, cw)


def _att_in_body(x_ref, mod_ref, g_ref, w_ref, qg_ref, kg_ref, cos_ref, sin_ref, bd_ref, q_ref, k_ref, vt_ref):
    d = D_MODEL
    h = _normmod(x_ref[...], g_ref[...], mod_ref[0, :, 0:d], mod_ref[0, :, d:2 * d]).astype(BF16)
    cosv = cos_ref[...]
    sinv = sin_ref[...]
    bd = bd_ref[...]
    lane = lax.broadcasted_iota(jnp.int32, cosv.shape, 1)
    upper_half = ((lane // (HEAD_DIM // 4)) % 2) == 1

    def norm_rope(p, gain):
        ssq = _dot((p * p).astype(BF16), bd)
        pn = p * lax.rsqrt(ssq * (1.0 / HEAD_DIM) + EPS) * gain
        outs = []
        for half in range(2):
            xh = pn[:, half * LANE:(half + 1) * LANE]
            partner = jnp.where(upper_half, pltpu.roll(xh, HEAD_DIM // 4, axis=1),
                                pltpu.roll(xh, LANE - HEAD_DIM // 4, axis=1))
            outs.append(xh * cosv + partner * sinv)
        return jnp.concatenate(outs, axis=1)

    for j in range(ATT_Q_COLS // 256):
        p = _dot(h, w_ref[:, j * 256:(j + 1) * 256])
        q_ref[:, j * 256:(j + 1) * 256] = (norm_rope(p, qg_ref[...]) * (HEAD_DIM ** -0.5)).astype(BF16)
    for j in range(ATT_K_COLS // 256):
        c0 = ATT_Q_COLS + j * 256
        p = _dot(h, w_ref[:, c0:c0 + 256])
        k_ref[:, j * 256:(j + 1) * 256] = norm_rope(p, kg_ref[...]).astype(BF16)
    c0 = ATT_Q_COLS + ATT_K_COLS
    v = _dot(h, w_ref[:, c0:c0 + ATT_K_COLS])
    vt_ref[0] = v.T.astype(BF16)


def _att_in(stream, mod, g1, w, qg, kg, cos_t, sin_t, bd, n_xt):
    rows, d = stream.shape
    tm = TOKEN_TILE
    nt = rows // tm
    ncol = w.shape[1]
    row = lambda i: (i, 0)
    const = lambda i: (0, 0)
    tab = lambda i: (jnp.where(i == nt - 1, n_xt, i % n_xt), 0)
    return pl.pallas_call(
        _att_in_body,
        grid=(nt,),
        in_specs=[pl.BlockSpec((tm, d), row),
                  pl.BlockSpec((1, 1, 6 * d), lambda i: (i // n_xt, 0, 0)),
                  pl.BlockSpec((1, d), const),
                  pl.BlockSpec((d, ncol), const),
                  pl.BlockSpec((1, 256), const),
                  pl.BlockSpec((1, 256), const),
                  pl.BlockSpec((tm, LANE), tab),
                  pl.BlockSpec((tm, LANE), tab),
                  pl.BlockSpec((256, 256), const)],
        out_specs=[pl.BlockSpec((tm, ATT_Q_COLS), row),
                   pl.BlockSpec((tm, ATT_K_COLS), row),
                   pl.BlockSpec((1, ATT_K_COLS, tm), lambda i: (i, 0, 0))],
        out_shape=[jax.ShapeDtypeStruct((rows, ATT_Q_COLS), BF16),
                   jax.ShapeDtypeStruct((rows, ATT_K_COLS), BF16),
                   jax.ShapeDtypeStruct((nt, ATT_K_COLS, tm), BF16)],
        compiler_params=_cparams("arbitrary"),
        name="att_in",
    )(stream, mod, g1, w, qg, kg, cos_t, sin_t, bd)


def _flash_body(*refs, use_x):
    q_ref, kc_ref, vtc_ref = refs[:3]
    if use_x:
        kx_ref, vtx_ref = refs[3:5]
    o_ref = refs[-1]
    tq = q_ref.shape[0]
    qs = [q_ref[:, g * LANE:(g + 1) * LANE] for g in range(GROUP)]

    def step(kt, vt, carry):
        new = []
        for g in range(GROUP):
            m, l, acc = carry[g]
            st = _dot_nt(kt, qs[g])
            m_new = jnp.maximum(m, jnp.max(st, axis=0, keepdims=True))
            alpha = jnp.exp(m - m_new)
            p = jnp.exp(st - m_new)
            l_new = alpha * l + jnp.sum(p, axis=0, keepdims=True)
            acc_new = alpha * acc + _dot(vt, p.astype(BF16))
            new.append((m_new, l_new, acc_new))
        return tuple(new)

    init = tuple((jnp.full((1, tq), -jnp.inf, F32), jnp.zeros((1, tq), F32), jnp.zeros((LANE, tq), F32))
                 for _ in range(GROUP))
    carry = step(kc_ref[...], vtc_ref[0], init)
    if use_x:
        carry = lax.fori_loop(
            0, vtx_ref.shape[0],
            lambda j, cr: step(kx_ref[pl.ds(pl.multiple_of(j * KV_TILE, KV_TILE), KV_TILE), :], vtx_ref[j], cr),
            carry)
    outs = [(acc / l).T for _, l, acc in carry]
    for pair in range(GROUP // 2):
        packed = outs[2 * pair] + pltpu.roll(outs[2 * pair + 1], HEAD_DIM, axis=1)
        o_ref[:, pair * LANE:(pair + 1) * LANE] = packed.astype(BF16)


def _flash(q, k, vt, o_prev, nb, t, lc, use_x):
    rows = q.shape[0]
    tq = SEQ_BLOCK
    n_xt = t // KV_TILE
    ctx_q0 = nb * t // tq
    ctx_tile = nb * t // KV_TILE
    nq = t // tq if use_x else 1
    qmap = (lambda b, h, i: (b * nq + i, h)) if use_x else (lambda b, h, i: (ctx_q0 + b, h))
    in_specs = [pl.BlockSpec((tq, GROUP * LANE), qmap),
                pl.BlockSpec((lc, LANE), lambda b, h, i: (nb * t // lc + b, h)),
                pl.BlockSpec((1, LANE, lc), lambda b, h, i: (ctx_tile, h, b))]
    args = [q, k, vt]
    if use_x:
        in_specs += [pl.BlockSpec((t, LANE), lambda b, h, i: (b, h)),
                     pl.BlockSpec((n_xt, LANE, KV_TILE), lambda b, h, i: (b, h, 0))]
        args += [k, vt]
    aliases = {}
    if o_prev is not None:
        in_specs.append(pl.BlockSpec(memory_space=pl.ANY))
        args.append(o_prev)
        aliases = {len(args) - 1: 0}
    return pl.pallas_call(
        functools.partial(_flash_body, use_x=use_x),
        grid=(nb, KV_HEADS, nq),
        in_specs=in_specs,
        out_specs=pl.BlockSpec((tq, GROUP * HEAD_DIM), qmap),
        out_shape=jax.ShapeDtypeStruct((rows, ATT_HEADS * HEAD_DIM), BF16),
        input_output_aliases=aliases,
        compiler_params=_cparams("arbitrary", "arbitrary", "arbitrary"),
        name="flash_x" if use_x else "flash_ctx",
    )(*args)


def _outproj_mlp_body(y_ref, x_ref, mod_ref, g_ref, wo_ref, w1_ref, w2_ref, out_ref):
    d = D_MODEL
    mod = lambda k: mod_ref[0, :, k * d:(k + 1) * d]
    x1 = x_ref[...] + mod(2) * _dot(y_ref[...], wo_ref[...])
    h2 = _normmod(x1, g_ref[...], mod(3), mod(4)).astype(BF16)
    acc = jnp.zeros_like(x1)
    for j in range(MLP_HIDDEN // d):
        u = jnp.maximum(_dot(h2, w1_ref[:, j * d:(j + 1) * d]), 0.0)
        acc = acc + _dot((u * u).astype(BF16), w2_ref[j * d:(j + 1) * d, :])
    out_ref[...] = x1 + mod(5) * acc


def _outproj_mlp(y, stream, mod, g2, wo, w1, w2, n_tiles, n_xt):
    d = D_MODEL
    tm = TOKEN_TILE
    row = lambda i: (i, 0)
    const = lambda i: (0, 0)
    return pl.pallas_call(
        _outproj_mlp_body,
        grid=(n_tiles,),
        in_specs=[pl.BlockSpec((tm, d), row),
                  pl.BlockSpec((tm, d), row),
                  pl.BlockSpec((1, 1, 6 * d), lambda i: (i // n_xt, 0, 0)),
                  pl.BlockSpec((1, d), const),
                  pl.BlockSpec((d, d), const),
                  pl.BlockSpec((d, MLP_HIDDEN), const),
                  pl.BlockSpec((MLP_HIDDEN, d), const)],
        out_specs=pl.BlockSpec((tm, d), row),
        out_shape=jax.ShapeDtypeStruct((n_tiles * tm, d), F32),
        compiler_params=_cparams("arbitrary"),
        name="outproj_mlp",
    )(y, stream, mod, g2, wo, w1, w2)


def _pad_heads(w, n_heads, width):
    d = w.shape[0]
    w = w.reshape(d, n_heads, width)
    return jnp.pad(w, ((0, 0), (0, 0), (0, LANE - width))).reshape(d, n_heads * LANE)


def _hyb_weights(w_in, gate_b):
    sizes = (ML_HEADS * ML_DK, ML_HEADS * ML_DK, ML_HEADS * ML_DV, ML_HEADS * ML_DV, 4 * ML_HEADS,
             SC_WIDTH, SC_WIDTH, SC_WIDTH)
    parts, c0 = [], 0
    for s in sizes:
        parts.append(w_in[:, c0:c0 + s])
        c0 += s
    wq, wk, wv, wo, wg, wb, wc, wu = parts
    wg = jnp.pad(wg, ((0, 0), (0, LANE - wg.shape[1])))
    w = jnp.concatenate([_pad_heads(wq, ML_HEADS, ML_DK), _pad_heads(wk, ML_HEADS, ML_DK), wv, wo, wg, wb, wc, wu],
                        axis=1).astype(BF16)
    gb = jnp.pad(gate_b.astype(F32), (0, LANE - gate_b.shape[0])).reshape(1, LANE)
    return w, gb


def _att_weights(w_in):
    nq = ATT_HEADS * HEAD_DIM
    nk = KV_HEADS * HEAD_DIM
    return jnp.concatenate([_pad_heads(w_in[:, :nq], ATT_HEADS, HEAD_DIM),
                            _pad_heads(w_in[:, nq:nq + nk], KV_HEADS, HEAD_DIM),
                            _pad_heads(w_in[:, nq + nk:], KV_HEADS, HEAD_DIM)], axis=1).astype(BF16)


def _rope_tables(t, n_ident):
    half, quarter = HEAD_DIM // 2, HEAD_DIM // 4
    inv = ROPE_THETA ** (-jnp.arange(0, half, 2, dtype=F32) / half)
    pos = jnp.arange(t)
    r = (pos // GRID_W).astype(F32)[:, None] * inv
    c = (pos % GRID_W).astype(F32)[:, None] * inv
    one = jnp.ones((t, LANE - HEAD_DIM), F32)
    cos_t = jnp.concatenate([jnp.cos(r), jnp.cos(r), jnp.cos(c), jnp.cos(c), one], axis=1)
    sin_t = jnp.concatenate([-jnp.sin(r), jnp.sin(r), -jnp.sin(c), jnp.sin(c), 0.0 * one], axis=1)
    assert quarter * 4 == HEAD_DIM
    cos_t = jnp.concatenate([cos_t, jnp.ones((n_ident, LANE), F32)], axis=0)
    sin_t = jnp.concatenate([sin_t, jnp.zeros((n_ident, LANE), F32)], axis=0)
    return cos_t, sin_t


def kernel(x, c, ctx, c_ctx, ada_w, ada_b, norm1_g, norm2_g, mlp_w1, mlp_w2, hyb_w_in, hyb_gate_b, mlstm_norm_g,
           conv_w, hyb_w_out, att_w_in, q_norm_g, k_norm_g, att_w_out):
    nb, t, d = x.shape
    lc = ctx.shape[1]
    assert d == D_MODEL and nb * lc == TOKEN_TILE and lc == SEQ_BLOCK and t % TOKEN_TILE == 0
    assert nb + 1 <= SUBLANE
    n_xt = t // TOKEN_TILE
    n_tiles = nb * n_xt + 1

    stream = jnp.concatenate([x.reshape(nb * t, d), ctx.reshape(nb * lc, d)], axis=0).astype(F32)
    cvec = jnp.concatenate([c, c_ctx[None, :], jnp.zeros((SUBLANE - nb - 1, d), c.dtype)], axis=0).astype(F32)
    mods = _adaln(cvec, ada_w.astype(F32), ada_b.astype(F32))

    cos_t, sin_t = _rope_tables(t, TOKEN_TILE)
    blk = lax.broadcasted_iota(jnp.int32, (256, 256), 0) // LANE == lax.broadcasted_iota(jnp.int32, (256, 256), 1) // LANE
    bd = blk.astype(BF16)
    pad_gain = lambda g: jnp.tile(jnp.pad(g.astype(F32), (0, LANE - HEAD_DIM)), 2).reshape(1, 2 * LANE)
    seq_starts = tuple(b * t for b in range(nb)) + tuple(nb * t + b * lc for b in range(nb))
    seq_ends = tuple((b + 1) * t - 1 for b in range(nb)) + tuple(nb * t + (b + 1) * lc - 1 for b in range(nb))

    for layer in range(DEPTH):
        last = layer == DEPTH - 1
        mod = mods[layer, :nb + 1].reshape(nb + 1, 1, 6 * d)
        g1 = norm1_g[layer].astype(F32).reshape(1, d)
        g2 = norm2_g[layer].astype(F32).reshape(1, d)
        if layer % 2 == 0:
            e = layer // 2
            w, gb = _hyb_weights(hyb_w_in[e], hyb_gate_b[e])
            q, k, v, o, gate, cb, z = _hyb_in(stream, mod, g1, w, gb, n_xt)
            hf = _mlstm(q, k, v, gate, nb, t, rev=False)
            hb = _mlstm(q, k, v, gate, nb, t, rev=True)
            cw = jnp.pad(conv_w[e].astype(F32), ((0, SUBLANE - CONV_K), (0, 0)))
            y = _hyb_combine(hf, hb, o, cb, z, mlstm_norm_g[e].astype(F32).reshape(1, -1), cw, seq_starts, seq_ends)
            wo = hyb_w_out[e].astype(BF16)
        else:
            a = layer // 2
            q, k, vt = _att_in(stream, mod, g1, _att_weights(att_w_in[a]), pad_gain(q_norm_g[a]),
                               pad_gain(k_norm_g[a]), cos_t, sin_t, bd, n_xt)
            y = _flash(q, k, vt, None, nb, t, lc, use_x=True)
            if not last:
                y = _flash(q, k, vt, y, nb, t, lc, use_x=False)
            wo = att_w_out[a].astype(BF16)
        stream = _outproj_mlp(y, stream, mod, g2, wo, mlp_w1[layer].astype(BF16), mlp_w2[layer].astype(BF16),
                              nb * n_xt if last else n_tiles, n_xt)
    return stream[:nb * t].reshape(nb, t, d).astype(x.dtype)
```

```python
import functools

import jax
import jax.numpy as jnp
from jax import lax
from jax.experimental import pallas as pl
from jax.experimental.pallas import tpu as pltpu

F32 = jnp.float32
BF16 = jnp.bfloat16

D_MODEL = 1024
DEPTH = 4
GRID_W = 64
EPS = 1e-6
ML_HEADS = 4
ML_DK = 64
ML_DV = 128
ML_CHUNK = 64
SC_WIDTH = D_MODEL // 2
CONV_K = 3
ATT_HEADS = 16
KV_HEADS = 4
HEAD_DIM = 64
GROUP = ATT_HEADS // KV_HEADS
ROPE_THETA = 10000.0
MLP_HIDDEN = 4 * D_MODEL

LANE = 128
SUBLANE = 8
TOKEN_TILE = 512
SEQ_BLOCK = 256
KV_TILE = TOKEN_TILE
VMEM_LIMIT_BYTES = 56 * 1024 * 1024

HYB_COLS = (4 * LANE, 4 * LANE, 4 * ML_DV, 4 * ML_DV, LANE, SC_WIDTH, SC_WIDTH, SC_WIDTH)
ATT_Q_COLS = ATT_HEADS * LANE
ATT_K_COLS = KV_HEADS * LANE


def _cparams(*sem):
    return pltpu.CompilerParams(dimension_semantics=sem, vmem_limit_bytes=VMEM_LIMIT_BYTES)


def _normmod(xf, g, shift, scale):
    var = jnp.mean(xf * xf, axis=-1, keepdims=True)
    return xf * lax.rsqrt(var + EPS) * g * (1.0 + scale) + shift


def _dot(a, b):
    return jnp.dot(a, b, preferred_element_type=F32)


def _dot_nt(a, b):
    return lax.dot_general(a, b, (((1,), (1,)), ((), ())), preferred_element_type=F32)


def _adaln_body(c_ref, w_ref, b_ref, o_ref):
    cv = c_ref[...]
    s = cv * jax.nn.sigmoid(cv)
    o_ref[0] = _dot(s.astype(BF16), w_ref[0].astype(BF16)) + b_ref[0]


def _adaln(cvec, ada_w, ada_b):
    depth, d, n = ada_w.shape
    bn = n // 4
    return pl.pallas_call(
        _adaln_body,
        grid=(depth, n // bn),
        in_specs=[pl.BlockSpec((SUBLANE, d), lambda l, j: (0, 0)),
                  pl.BlockSpec((1, d, bn), lambda l, j: (l, 0, j)),
                  pl.BlockSpec((1, 1, bn), lambda l, j: (l, 0, j))],
        out_specs=pl.BlockSpec((1, SUBLANE, bn), lambda l, j: (l, 0, j)),
        out_shape=jax.ShapeDtypeStruct((depth, SUBLANE, n), F32),
        compiler_params=_cparams("arbitrary", "arbitrary"),
        name="adaln",
    )(cvec, ada_w, ada_b.reshape(depth, 1, n))


def _hyb_in_body(x_ref, mod_ref, g_ref, w_ref, gb_ref, q_ref, k_ref, v_ref, o_ref, gate_ref, cb_ref, z_ref):
    d = D_MODEL
    h = _normmod(x_ref[...], g_ref[...], mod_ref[0, :, 0:d], mod_ref[0, :, d:2 * d]).astype(BF16)
    c0 = 0
    q_ref[...] = (_dot(h, w_ref[:, c0:c0 + 512]) * (ML_DK ** -0.5)).astype(BF16)
    c0 += 512
    k_ref[...] = _dot(h, w_ref[:, c0:c0 + 512]).astype(BF16)
    c0 += 512
    v_ref[...] = _dot(h, w_ref[:, c0:c0 + 512])
    c0 += 512
    o_ref[...] = _dot(h, w_ref[:, c0:c0 + 512])
    c0 += 512
    g = _dot(h, w_ref[:, c0:c0 + LANE]) + gb_ref[...]
    c0 += LANE
    lane = lax.broadcasted_iota(jnp.int32, g.shape, 1)
    is_forget = ((lane // ML_HEADS) % 2) == 1
    logsig = jnp.minimum(g, 0.0) - jnp.log1p(jnp.exp(-jnp.abs(g)))
    gate_ref[...] = jnp.where(is_forget, logsig, g)
    cb_ref[...] = _dot(h, w_ref[:, c0:c0 + 512])
    c0 += 512
    gc = _dot(h, w_ref[:, c0:c0 + 512])
    c0 += 512
    z_ref[...] = gc * _dot(h, w_ref[:, c0:c0 + 512])


def _hyb_in(stream, mod, g1, w, gate_b, n_xt):
    rows, d = stream.shape
    tm = TOKEN_TILE
    ncol = w.shape[1]
    row = lambda i: (i, 0)
    const = lambda i: (0, 0)
    outs = [(512, BF16), (512, BF16), (512, F32), (512, F32), (LANE, F32), (512, F32), (512, F32)]
    return pl.pallas_call(
        _hyb_in_body,
        grid=(rows // tm,),
        in_specs=[pl.BlockSpec((tm, d), row),
                  pl.BlockSpec((1, 1, 6 * d), lambda i: (i // n_xt, 0, 0)),
                  pl.BlockSpec((1, d), const),
                  pl.BlockSpec((d, ncol), const),
                  pl.BlockSpec((1, LANE), const)],
        out_specs=[pl.BlockSpec((tm, c), row) for c, _ in outs],
        out_shape=[jax.ShapeDtypeStruct((rows, c), t) for c, t in outs],
        compiler_params=_cparams("arbitrary"),
        name="hyb_in",
    )(stream, mod, g1, w, gate_b)


def _split3(a):
    hi = a.astype(BF16)
    r = a - hi.astype(F32)
    mid = r.astype(BF16)
    lo = (r - mid.astype(F32)).astype(BF16)
    return hi, mid, lo


def _mlstm_body(q_ref, k_ref, v_ref, gate_ref, h_ref, c_ref, m_ref, *, rev):
    L = ML_CHUNK

    @pl.when(pl.program_id(1) == 0)
    def _():
        c_ref[...] = jnp.zeros_like(c_ref)
        m_ref[...] = jnp.zeros_like(m_ref)

    jj = lax.broadcasted_iota(jnp.int32, (L, L), 0)
    ss = lax.broadcasted_iota(jnp.int32, (L, L), 1)
    lower = ss <= jj
    upper = ss >= jj
    mask = upper if rev else lower
    tc = mask.astype(BF16)
    tr = (lower if rev else upper).astype(BF16)
    ones_v = jnp.ones((L, ML_DV), F32)
    n_chunks = q_ref.shape[0] // L
    order = range(n_chunks - 1, -1, -1) if rev else range(n_chunks)
    lane0 = 2 * ML_HEADS if rev else 0

    for c in order:
        r0 = c * L
        G = gate_ref[r0:r0 + L, :]
        GT = G.T
        bc = sum(_dot(tc, p) for p in _split3(G))
        br = sum(_dot(p, tr) for p in _split3(GT))
        ig_sh = pltpu.roll(G, ML_HEADS, axis=1)
        b_end = bc[0:1, :] if rev else bc[L - 1:L, :]
        m0 = m_ref[0:1, :]
        logw = b_end - bc + ig_sh
        m_loc = jnp.max(logw, axis=0, keepdims=True)
        w_all = jnp.exp(logw - m_loc)
        m_new = jnp.maximum(b_end + m0, m_loc)
        a2_all = jnp.exp(b_end + m0 - m_new)
        g2_all = jnp.exp(m_loc - m_new)
        m_inter_all = bc + m0
        m_ref[0:1, :] = m_new

        for h in range(ML_HEADS):
            il = lane0 + h
            fl = il + ML_HEADS
            cs = slice(h * LANE, (h + 1) * LANE)
            qh = q_ref[r0:r0 + L, cs]
            kh = k_ref[r0:r0 + L, cs]
            vh = v_ref[r0:r0 + L, cs]
            bcol = bc[:, fl:fl + 1]
            brow = br[fl:fl + 1, :]
            igrow = GT[il:il + 1, :]
            logd = jnp.where(mask, bcol - brow + igrow, -jnp.inf)
            m_inter = m_inter_all[:, fl:fl + 1]
            m = jnp.maximum(m_inter, jnp.max(logd, axis=1, keepdims=True))
            dmat = jnp.exp(logd - m)
            a = jnp.exp(m_inter - m)
            s = _dot_nt(qh, kh) * dmat
            vaug = jnp.concatenate([vh, ones_v], axis=1)
            c_old = c_ref[h]
            num_aug = _dot(s.astype(BF16), vaug.astype(BF16)) + a * _dot(qh, c_old.astype(BF16))
            num = num_aug[:, :ML_DV]
            den = num_aug[:, ML_DV:]
            h_ref[r0:r0 + L, cs] = num / jnp.maximum(jnp.abs(den), jnp.exp(-m))
            w = w_all[:, fl:fl + 1]
            wv = (w * vaug).astype(BF16)
            khT = kh.astype(F32).T.astype(BF16)
            c_loc = _dot(khT, wv)
            c_ref[h] = a2_all[:, fl:fl + 1] * c_old + g2_all[:, fl:fl + 1] * c_loc


def _mlstm(q, k, v, gate, nb, t, rev):
    rows = q.shape[0]
    blk = SEQ_BLOCK
    nxb = t // blk
    ctx_blk0 = nb * nxb

    def imap(b, i):
        xi = (nxb - i) if rev else (i - 1)
        return (jnp.where(i == 0, ctx_blk0 + b, b * nxb + xi), 0)

    spec = lambda c: pl.BlockSpec((blk, c), imap)
    return pl.pallas_call(
        functools.partial(_mlstm_body, rev=rev),
        grid=(nb, nxb + 1),
        in_specs=[spec(512), spec(512), spec(512), spec(LANE)],
        out_specs=spec(512),
        out_shape=jax.ShapeDtypeStruct((rows, 512), F32),
        scratch_shapes=[pltpu.VMEM((ML_HEADS, LANE, 2 * ML_DV), F32), pltpu.VMEM((SUBLANE, LANE), F32)],
        compiler_params=_cparams("arbitrary", "arbitrary"),
        name="mlstm_bwd" if rev else "mlstm_fwd",
    )(q, k, v, gate)


def _hyb_comb_body(hf_ref, hb_ref, o_ref, cb_ref, z_ref, zp_ref, zn_ref, mg_ref, cw_ref, out_ref, *, starts, ends):
    tm = hf_ref.shape[0]
    i = pl.program_id(0)
    for h in range(ML_HEADS):
        cs = slice(h * ML_DV, (h + 1) * ML_DV)
        blk = hf_ref[:, cs] + hb_ref[:, cs]
        var = jnp.mean(blk * blk, axis=-1, keepdims=True)
        hn = blk * lax.rsqrt(var + EPS) * mg_ref[:, cs]
        out_ref[:, cs] = (hn * jax.nn.sigmoid(o_ref[:, cs])).astype(BF16)
    z = z_ref[...]
    loc = lax.broadcasted_iota(jnp.int32, (tm, 1), 0)
    row = loc + i * tm
    is_start = functools.reduce(jnp.logical_or, [row == r for r in starts])
    is_end = functools.reduce(jnp.logical_or, [row == r for r in ends])
    zprev = jnp.where(loc == 0, zp_ref[SUBLANE - 1:SUBLANE, :], pltpu.roll(z, 1, axis=0))
    zprev = jnp.where(is_start, 0.0, zprev)
    znext = jnp.where(loc == tm - 1, zn_ref[0:1, :], pltpu.roll(z, tm - 1, axis=0))
    znext = jnp.where(is_end, 0.0, znext)
    conv = cw_ref[0:1, :] * zprev + cw_ref[1:2, :] * z + cw_ref[2:3, :] * znext
    out_ref[:, 4 * ML_DV:] = (cb_ref[...] * conv).astype(BF16)


def _hyb_combine(hf, hb, o, cb, z, mg, cw, starts, ends):
    rows = hf.shape[0]
    tm = TOKEN_TILE
    per = tm // SUBLANE
    nblk = rows // SUBLANE
    row = lambda i: (i, 0)
    const = lambda i: (0, 0)
    return pl.pallas_call(
        functools.partial(_hyb_comb_body, starts=starts, ends=ends),
        grid=(rows // tm,),
        in_specs=[pl.BlockSpec((tm, 512), row)] * 5 + [
            pl.BlockSpec((SUBLANE, 512), lambda i: (jnp.maximum(i * per - 1, 0), 0)),
            pl.BlockSpec((SUBLANE, 512), lambda i: (jnp.minimum((i + 1) * per, nblk - 1), 0)),
            pl.BlockSpec((1, 512), const),
            pl.BlockSpec((SUBLANE, 512), const)],
        out_specs=pl.BlockSpec((tm, D_MODEL), row),
        out_shape=jax.ShapeDtypeStruct((rows, D_MODEL), BF16),
        compiler_params=_cparams("arbitrary"),
        name="hyb_combine",
    )(hf, hb, o, cb, z, z, z, mg, cw)


CONST_LANE = HEAD_DIM
LOG2E = 1.4426950408889634
BOUND_MARGIN = 1.02
MAX_BOUND_LOG2 = 48.0


def _att_in_body(x_ref, mod_ref, g_ref, w_ref, qg_ref, kg_ref, cos_ref, sin_ref, bd_ref, pads_ref,
                 q_ref, k_ref, vt_ref):
    d = D_MODEL
    h = _normmod(x_ref[...], g_ref[...], mod_ref[0, :, 0:d], mod_ref[0, :, d:2 * d]).astype(BF16)
    cosv = cos_ref[...]
    sinv = sin_ref[...]
    bd = bd_ref[...]
    lane = lax.broadcasted_iota(jnp.int32, cosv.shape, 1)
    upper_half = ((lane // (HEAD_DIM // 4)) % 2) == 1

    def norm_rope(p, gain, scale, pad):
        ssq = _dot((p * p).astype(BF16), bd)
        pn = p * lax.rsqrt(ssq * (1.0 / HEAD_DIM) + EPS) * gain
        outs = []
        for half in range(2):
            xh = pn[:, half * LANE:(half + 1) * LANE]
            partner = jnp.where(upper_half, pltpu.roll(xh, HEAD_DIM // 4, axis=1),
                                pltpu.roll(xh, LANE - HEAD_DIM // 4, axis=1))
            outs.append((xh * cosv + partner * sinv) * scale + pad)
        return jnp.concatenate(outs, axis=1)

    q_scale = HEAD_DIM ** -0.5 * LOG2E
    for j in range(ATT_Q_COLS // 256):
        p = _dot(h, w_ref[:, j * 256:(j + 1) * 256])
        q_ref[:, j * 256:(j + 1) * 256] = norm_rope(p, qg_ref[...], q_scale, pads_ref[0:1, :]).astype(BF16)
    for j in range(ATT_K_COLS // 256):
        c0 = ATT_Q_COLS + j * 256
        p = _dot(h, w_ref[:, c0:c0 + 256])
        k_ref[:, j * 256:(j + 1) * 256] = norm_rope(p, kg_ref[...], 1.0, pads_ref[1:2, :]).astype(BF16)
    c0 = ATT_Q_COLS + ATT_K_COLS
    vpad = pads_ref[2:3, :]
    v = _dot(h, w_ref[:, c0:c0 + ATT_K_COLS]) + jnp.concatenate([vpad] * KV_HEADS, axis=1)
    vt_ref[0] = v.T.astype(BF16)


def _att_in(stream, mod, g1, w, qg, kg, cos_t, sin_t, bd, pads, n_xt):
    rows, d = stream.shape
    tm = TOKEN_TILE
    nt = rows // tm
    ncol = w.shape[1]
    row = lambda i: (i, 0)
    const = lambda i: (0, 0)
    tab = lambda i: (jnp.where(i == nt - 1, n_xt, i % n_xt), 0)
    return pl.pallas_call(
        _att_in_body,
        grid=(nt,),
        in_specs=[pl.BlockSpec((tm, d), row),
                  pl.BlockSpec((1, 1, 6 * d), lambda i: (i // n_xt, 0, 0)),
                  pl.BlockSpec((1, d), const),
                  pl.BlockSpec((d, ncol), const),
                  pl.BlockSpec((1, 256), const),
                  pl.BlockSpec((1, 256), const),
                  pl.BlockSpec((tm, LANE), tab),
                  pl.BlockSpec((tm, LANE), tab),
                  pl.BlockSpec((256, 256), const),
                  pl.BlockSpec((SUBLANE, LANE), const)],
        out_specs=[pl.BlockSpec((tm, ATT_Q_COLS), row),
                   pl.BlockSpec((tm, ATT_K_COLS), row),
                   pl.BlockSpec((1, ATT_K_COLS, tm), lambda i: (i, 0, 0))],
        out_shape=[jax.ShapeDtypeStruct((rows, ATT_Q_COLS), BF16),
                   jax.ShapeDtypeStruct((rows, ATT_K_COLS), BF16),
                   jax.ShapeDtypeStruct((nt, ATT_K_COLS, tm), BF16)],
        compiler_params=_cparams("arbitrary"),
        name="att_in",
    )(stream, mod, g1, w, qg, kg, cos_t, sin_t, bd, pads)


def _store_heads(o_ref, outs_t):
    keep = lax.broadcasted_iota(jnp.int32, (1, LANE), 1) < HEAD_DIM
    outs = [jnp.where(keep, o.T, 0.0) for o in outs_t]
    for pair in range(GROUP // 2):
        packed = outs[2 * pair] + pltpu.roll(outs[2 * pair + 1], HEAD_DIM, axis=1)
        o_ref[:, pair * LANE:(pair + 1) * LANE] = packed.astype(BF16)


def _flash_safe_body(*refs, use_x):
    q_ref, kc_ref, vtc_ref = refs[:3]
    if use_x:
        kx_ref, vtx_ref = refs[3:5]
    o_ref = refs[-1]
    tq = q_ref.shape[0]
    qs = [q_ref[:, g * LANE:(g + 1) * LANE] for g in range(GROUP)]

    def step(kt, vt, carry):
        new = []
        for g in range(GROUP):
            m, acc = carry[g]
            st = _dot_nt(kt, qs[g])
            m_new = jnp.maximum(m, jnp.max(st, axis=0, keepdims=True))
            p = jnp.exp2(st - m_new)
            acc_new = jnp.exp2(m - m_new) * acc + _dot(vt, p.astype(BF16))
            new.append((m_new, acc_new))
        return tuple(new)

    init = tuple((jnp.full((1, tq), -jnp.inf, F32), jnp.zeros((LANE, tq), F32)) for _ in range(GROUP))
    carry = step(kc_ref[...], vtc_ref[0], init)
    if use_x:
        carry = lax.fori_loop(
            0, vtx_ref.shape[0],
            lambda j, cr: step(kx_ref[pl.ds(pl.multiple_of(j * KV_TILE, KV_TILE), KV_TILE), :], vtx_ref[j], cr),
            carry)
    _store_heads(o_ref, [acc / acc[CONST_LANE:CONST_LANE + 1, :] for _, acc in carry])


def _flash_bounded_body(q_ref, kc_ref, vtc_ref, kx_ref, vtx_ref, o_ref, qa_ref, acc_ref):
    tq = q_ref.shape[0]
    for g in range(GROUP):
        qa_ref[g * tq:(g + 1) * tq, :] = q_ref[:, g * LANE:(g + 1) * LANE]

    def chunk(kt, vt):
        st = _dot_nt(kt, qa_ref[...])
        return _dot(vt, jnp.exp2(st).astype(BF16))

    acc_ref[...] = chunk(kc_ref[...], vtc_ref[0])

    def body(j, carry):
        acc_ref[...] += chunk(kx_ref[pl.ds(pl.multiple_of(j * KV_TILE, KV_TILE), KV_TILE), :], vtx_ref[j])
        return carry

    lax.fori_loop(0, vtx_ref.shape[0], body, 0, unroll=2)
    outs = []
    for g in range(GROUP):
        a = acc_ref[:, g * tq:(g + 1) * tq]
        outs.append(a * (1.0 / a[CONST_LANE:CONST_LANE + 1, :]))
    _store_heads(o_ref, outs)


def _flash(q, k, vt, o_prev, nb, t, lc, use_x, bounded=False):
    rows = q.shape[0]
    tq = SEQ_BLOCK
    n_xt = t // KV_TILE
    ctx_q0 = nb * t // tq
    ctx_tile = nb * t // KV_TILE
    nq = t // tq if use_x else 1
    qmap = (lambda b, h, i: (b * nq + i, h)) if use_x else (lambda b, h, i: (ctx_q0 + b, h))
    in_specs = [pl.BlockSpec((tq, GROUP * LANE), qmap),
                pl.BlockSpec((lc, LANE), lambda b, h, i: (nb * t // lc + b, h)),
                pl.BlockSpec((1, LANE, lc), lambda b, h, i: (ctx_tile, h, b))]
    args = [q, k, vt]
    if use_x:
        in_specs += [pl.BlockSpec((t, LANE), lambda b, h, i: (b, h)),
                     pl.BlockSpec((n_xt, LANE, KV_TILE), lambda b, h, i: (b, h, 0))]
        args += [k, vt]
    aliases = {}
    if o_prev is not None:
        in_specs.append(pl.BlockSpec(memory_space=pl.ANY))
        args.append(o_prev)
        aliases = {len(args) - 1: 0}
    if bounded:
        assert use_x and o_prev is None
        body = _flash_bounded_body
        scratch = [pltpu.VMEM((GROUP * tq, LANE), BF16), pltpu.VMEM((LANE, GROUP * tq), F32)]
        name = "flash_bounded"
    else:
        body = functools.partial(_flash_safe_body, use_x=use_x)
        scratch = []
        name = "flash_x" if use_x else "flash_ctx"
    return pl.pallas_call(
        body,
        grid=(nb, KV_HEADS, nq),
        in_specs=in_specs,
        out_specs=pl.BlockSpec((tq, GROUP * HEAD_DIM), qmap),
        out_shape=jax.ShapeDtypeStruct((rows, ATT_HEADS * HEAD_DIM), BF16),
        scratch_shapes=scratch,
        input_output_aliases=aliases,
        compiler_params=_cparams("arbitrary", "arbitrary", "arbitrary"),
        name=name,
    )(*args)


def _outproj_mlp_body(y_ref, x_ref, mod_ref, g_ref, wo_ref, w1_ref, w2_ref, out_ref):
    d = D_MODEL
    mod = lambda k: mod_ref[0, :, k * d:(k + 1) * d]
    x1 = x_ref[...] + mod(2) * _dot(y_ref[...], wo_ref[...])
    h2 = _normmod(x1, g_ref[...], mod(3), mod(4)).astype(BF16)
    acc = jnp.zeros_like(x1)
    for j in range(MLP_HIDDEN // d):
        u = jnp.maximum(_dot(h2, w1_ref[:, j * d:(j + 1) * d]), 0.0)
        acc = acc + _dot((u * u).astype(BF16), w2_ref[j * d:(j + 1) * d, :])
    out_ref[...] = x1 + mod(5) * acc


def _outproj_mlp(y, stream, mod, g2, wo, w1, w2, n_tiles, n_xt):
    d = D_MODEL
    tm = TOKEN_TILE
    row = lambda i: (i, 0)
    const = lambda i: (0, 0)
    return pl.pallas_call(
        _outproj_mlp_body,
        grid=(n_tiles,),
        in_specs=[pl.BlockSpec((tm, d), row),
                  pl.BlockSpec((tm, d), row),
                  pl.BlockSpec((1, 1, 6 * d), lambda i: (i // n_xt, 0, 0)),
                  pl.BlockSpec((1, d), const),
                  pl.BlockSpec((d, d), const),
                  pl.BlockSpec((d, MLP_HIDDEN), const),
                  pl.BlockSpec((MLP_HIDDEN, d), const)],
        out_specs=pl.BlockSpec((tm, d), row),
        out_shape=jax.ShapeDtypeStruct((n_tiles * tm, d), F32),
        compiler_params=_cparams("arbitrary"),
        name="outproj_mlp",
    )(y, stream, mod, g2, wo, w1, w2)


def _pad_heads(w, n_heads, width):
    d = w.shape[0]
    w = w.reshape(d, n_heads, width)
    return jnp.pad(w, ((0, 0), (0, 0), (0, LANE - width))).reshape(d, n_heads * LANE)


def _hyb_weights(w_in, gate_b):
    sizes = (ML_HEADS * ML_DK, ML_HEADS * ML_DK, ML_HEADS * ML_DV, ML_HEADS * ML_DV, 4 * ML_HEADS,
             SC_WIDTH, SC_WIDTH, SC_WIDTH)
    parts, c0 = [], 0
    for s in sizes:
        parts.append(w_in[:, c0:c0 + s])
        c0 += s
    wq, wk, wv, wo, wg, wb, wc, wu = parts
    wg = jnp.pad(wg, ((0, 0), (0, LANE - wg.shape[1])))
    w = jnp.concatenate([_pad_heads(wq, ML_HEADS, ML_DK), _pad_heads(wk, ML_HEADS, ML_DK), wv, wo, wg, wb, wc, wu],
                        axis=1).astype(BF16)
    gb = jnp.pad(gate_b.astype(F32), (0, LANE - gate_b.shape[0])).reshape(1, LANE)
    return w, gb


def _att_weights(w_in):
    nq = ATT_HEADS * HEAD_DIM
    nk = KV_HEADS * HEAD_DIM
    return jnp.concatenate([_pad_heads(w_in[:, :nq], ATT_HEADS, HEAD_DIM),
                            _pad_heads(w_in[:, nq:nq + nk], KV_HEADS, HEAD_DIM),
                            _pad_heads(w_in[:, nq + nk:], KV_HEADS, HEAD_DIM)], axis=1).astype(BF16)


def _rope_tables(t, n_ident):
    half, quarter = HEAD_DIM // 2, HEAD_DIM // 4
    inv = ROPE_THETA ** (-jnp.arange(0, half, 2, dtype=F32) / half)
    pos = jnp.arange(t)
    r = (pos // GRID_W).astype(F32)[:, None] * inv
    c = (pos % GRID_W).astype(F32)[:, None] * inv
    one = jnp.ones((t, LANE - HEAD_DIM), F32)
    cos_t = jnp.concatenate([jnp.cos(r), jnp.cos(r), jnp.cos(c), jnp.cos(c), one], axis=1)
    sin_t = jnp.concatenate([-jnp.sin(r), jnp.sin(r), -jnp.sin(c), jnp.sin(c), 0.0 * one], axis=1)
    assert quarter * 4 == HEAD_DIM
    cos_t = jnp.concatenate([cos_t, jnp.ones((n_ident, LANE), F32)], axis=0)
    sin_t = jnp.concatenate([sin_t, jnp.zeros((n_ident, LANE), F32)], axis=0)
    return cos_t, sin_t


def kernel(x, c, ctx, c_ctx, ada_w, ada_b, norm1_g, norm2_g, mlp_w1, mlp_w2, hyb_w_in, hyb_gate_b, mlstm_norm_g,
           conv_w, hyb_w_out, att_w_in, q_norm_g, k_norm_g, att_w_out):
    nb, t, d = x.shape
    lc = ctx.shape[1]
    assert d == D_MODEL and nb * lc == TOKEN_TILE and lc == SEQ_BLOCK and t % TOKEN_TILE == 0
    assert nb + 1 <= SUBLANE
    n_xt = t // TOKEN_TILE
    n_tiles = nb * n_xt + 1

    stream = jnp.concatenate([x.reshape(nb * t, d), ctx.reshape(nb * lc, d)], axis=0).astype(F32)
    cvec = jnp.concatenate([c, c_ctx[None, :], jnp.zeros((SUBLANE - nb - 1, d), c.dtype)], axis=0).astype(F32)
    mods = _adaln(cvec, ada_w.astype(F32), ada_b.astype(F32))

    cos_t, sin_t = _rope_tables(t, TOKEN_TILE)
    blk = lax.broadcasted_iota(jnp.int32, (256, 256), 0) // LANE == lax.broadcasted_iota(jnp.int32, (256, 256), 1) // LANE
    bd = blk.astype(BF16)
    pad_gain = lambda g: jnp.tile(jnp.pad(g.astype(F32), (0, LANE - HEAD_DIM)), 2).reshape(1, 2 * LANE)
    seq_starts = tuple(b * t for b in range(nb)) + tuple(nb * t + b * lc for b in range(nb))
    seq_ends = tuple((b + 1) * t - 1 for b in range(nb)) + tuple(nb * t + (b + 1) * lc - 1 for b in range(nb))

    for layer in range(DEPTH):
        last = layer == DEPTH - 1
        mod = mods[layer, :nb + 1].reshape(nb + 1, 1, 6 * d)
        g1 = norm1_g[layer].astype(F32).reshape(1, d)
        g2 = norm2_g[layer].astype(F32).reshape(1, d)
        if layer % 2 == 0:
            e = layer // 2
            w, gb = _hyb_weights(hyb_w_in[e], hyb_gate_b[e])
            q, k, v, o, gate, cb, z = _hyb_in(stream, mod, g1, w, gb, n_xt)
            hf = _mlstm(q, k, v, gate, nb, t, rev=False)
            hb = _mlstm(q, k, v, gate, nb, t, rev=True)
            cw = jnp.pad(conv_w[e].astype(F32), ((0, SUBLANE - CONV_K), (0, 0)))
            y = _hyb_combine(hf, hb, o, cb, z, mlstm_norm_g[e].astype(F32).reshape(1, -1), cw, seq_starts, seq_ends)
            wo = hyb_w_out[e].astype(BF16)
        else:
            a = layer // 2
            bound = (HEAD_DIM ** 0.5 * LOG2E * BOUND_MARGIN) * jnp.max(jnp.abs(q_norm_g[a])) * jnp.max(jnp.abs(k_norm_g[a]))
            bound = bound.astype(F32)
            lane_const = (jnp.arange(LANE) == CONST_LANE).astype(F32)
            pads = jnp.stack([-bound * lane_const, lane_const, lane_const] + [0.0 * lane_const] * (SUBLANE - 3))
            q, k, vt = _att_in(stream, mod, g1, _att_weights(att_w_in[a]), pad_gain(q_norm_g[a]),
                               pad_gain(k_norm_g[a]), cos_t, sin_t, bd, pads, n_xt)
            y = lax.cond(bound <= MAX_BOUND_LOG2,
                         lambda: _flash(q, k, vt, None, nb, t, lc, use_x=True, bounded=True),
                         lambda: _flash(q, k, vt, None, nb, t, lc, use_x=True))
            if not last:
                y = _flash(q, k, vt, y, nb, t, lc, use_x=False)
            wo = att_w_out[a].astype(BF16)
        stream = _outproj_mlp(y, stream, mod, g2, wo, mlp_w1[layer].astype(BF16), mlp_w2[layer].astype(BF16),
                              nb * n_xt if last else n_tiles, n_xt)
    return stream[:nb * t].reshape(nb, t, d).astype(x.dtype)
```

```python
import functools

import jax
import jax.numpy as jnp
from jax import lax
from jax.experimental import pallas as pl
from jax.experimental.pallas import tpu as pltpu

F32 = jnp.float32
BF16 = jnp.bfloat16

D_MODEL = 1024
DEPTH = 4
GRID_W = 64
EPS = 1e-6
ML_HEADS = 4
ML_DK = 64
ML_DV = 128
ML_CHUNK = 64
SC_WIDTH = D_MODEL // 2
CONV_K = 3
ATT_HEADS = 16
KV_HEADS = 4
HEAD_DIM = 64
GROUP = ATT_HEADS // KV_HEADS
ROPE_THETA = 10000.0
MLP_HIDDEN = 4 * D_MODEL

LANE = 128
SUBLANE = 8
TOKEN_TILE = 512
SEQ_BLOCK = 256
KV_TILE = TOKEN_TILE
VMEM_LIMIT_BYTES = 56 * 1024 * 1024

HYB_COLS = (4 * LANE, 4 * LANE, 4 * ML_DV, 4 * ML_DV, LANE, SC_WIDTH, SC_WIDTH, SC_WIDTH)
ATT_Q_COLS = ATT_HEADS * LANE
ATT_K_COLS = KV_HEADS * LANE


def _cparams(*sem):
    return pltpu.CompilerParams(dimension_semantics=sem, vmem_limit_bytes=VMEM_LIMIT_BYTES)


def _normmod(xf, g, shift, scale):
    var = jnp.mean(xf * xf, axis=-1, keepdims=True)
    return xf * lax.rsqrt(var + EPS) * g * (1.0 + scale) + shift


def _dot(a, b):
    return jnp.dot(a, b, preferred_element_type=F32)


def _dot_nt(a, b):
    return lax.dot_general(a, b, (((1,), (1,)), ((), ())), preferred_element_type=F32)


def _adaln_body(c_ref, w_ref, b_ref, o_ref):
    cv = c_ref[...]
    s = cv * jax.nn.sigmoid(cv)
    o_ref[0] = _dot(s.astype(BF16), w_ref[0].astype(BF16)) + b_ref[0]


def _adaln(cvec, ada_w, ada_b):
    depth, d, n = ada_w.shape
    bn = n // 4
    return pl.pallas_call(
        _adaln_body,
        grid=(depth, n // bn),
        in_specs=[pl.BlockSpec((SUBLANE, d), lambda l, j: (0, 0)),
                  pl.BlockSpec((1, d, bn), lambda l, j: (l, 0, j)),
                  pl.BlockSpec((1, 1, bn), lambda l, j: (l, 0, j))],
        out_specs=pl.BlockSpec((1, SUBLANE, bn), lambda l, j: (l, 0, j)),
        out_shape=jax.ShapeDtypeStruct((depth, SUBLANE, n), F32),
        compiler_params=_cparams("arbitrary", "arbitrary"),
        name="adaln",
    )(cvec, ada_w, ada_b.reshape(depth, 1, n))


def _hyb_in_body(x_ref, mod_ref, g_ref, w_ref, gb_ref, q_ref, k_ref, v_ref, o_ref, gate_ref, cb_ref, z_ref):
    d = D_MODEL
    h = _normmod(x_ref[...], g_ref[...], mod_ref[0, :, 0:d], mod_ref[0, :, d:2 * d]).astype(BF16)
    c0 = 0
    q_ref[...] = (_dot(h, w_ref[:, c0:c0 + 512]) * (ML_DK ** -0.5)).astype(BF16)
    c0 += 512
    k_ref[...] = _dot(h, w_ref[:, c0:c0 + 512]).astype(BF16)
    c0 += 512
    v_ref[...] = _dot(h, w_ref[:, c0:c0 + 512])
    c0 += 512
    o_ref[...] = _dot(h, w_ref[:, c0:c0 + 512])
    c0 += 512
    g = _dot(h, w_ref[:, c0:c0 + LANE]) + gb_ref[...]
    c0 += LANE
    lane = lax.broadcasted_iota(jnp.int32, g.shape, 1)
    is_forget = ((lane // ML_HEADS) % 2) == 1
    logsig = jnp.minimum(g, 0.0) - jnp.log1p(jnp.exp(-jnp.abs(g)))
    gate_ref[...] = jnp.where(is_forget, logsig, g)
    cb_ref[...] = _dot(h, w_ref[:, c0:c0 + 512])
    c0 += 512
    gc = _dot(h, w_ref[:, c0:c0 + 512])
    c0 += 512
    z_ref[...] = gc * _dot(h, w_ref[:, c0:c0 + 512])


def _hyb_in(stream, mod, g1, w, gate_b, n_xt):
    rows, d = stream.shape
    tm = TOKEN_TILE
    ncol = w.shape[1]
    row = lambda i: (i, 0)
    const = lambda i: (0, 0)
    outs = [(512, BF16), (512, BF16), (512, F32), (512, F32), (LANE, F32), (512, F32), (512, F32)]
    return pl.pallas_call(
        _hyb_in_body,
        grid=(rows // tm,),
        in_specs=[pl.BlockSpec((tm, d), row),
                  pl.BlockSpec((1, 1, 6 * d), lambda i: (i // n_xt, 0, 0)),
                  pl.BlockSpec((1, d), const),
                  pl.BlockSpec((d, ncol), const),
                  pl.BlockSpec((1, LANE), const)],
        out_specs=[pl.BlockSpec((tm, c), row) for c, _ in outs],
        out_shape=[jax.ShapeDtypeStruct((rows, c), t) for c, t in outs],
        compiler_params=_cparams("arbitrary"),
        name="hyb_in",
    )(stream, mod, g1, w, gate_b)


def _split3(a):
    hi = a.astype(BF16)
    r = a - hi.astype(F32)
    mid = r.astype(BF16)
    lo = (r - mid.astype(F32)).astype(BF16)
    return hi, mid, lo


def _mlstm_body(q_ref, k_ref, v_ref, gate_ref, h_ref, c_ref, m_ref, *, rev):
    L = ML_CHUNK

    @pl.when(pl.program_id(1) == 0)
    def _():
        c_ref[...] = jnp.zeros_like(c_ref)
        m_ref[...] = jnp.zeros_like(m_ref)

    jj = lax.broadcasted_iota(jnp.int32, (L, L), 0)
    ss = lax.broadcasted_iota(jnp.int32, (L, L), 1)
    lower = ss <= jj
    upper = ss >= jj
    mask = upper if rev else lower
    tc = mask.astype(BF16)
    tr = (lower if rev else upper).astype(BF16)
    ones_v = jnp.ones((L, ML_DV), F32)
    n_chunks = q_ref.shape[0] // L
    order = range(n_chunks - 1, -1, -1) if rev else range(n_chunks)
    lane0 = 2 * ML_HEADS if rev else 0

    for c in order:
        r0 = c * L
        G = gate_ref[r0:r0 + L, :]
        GT = G.T
        bc = sum(_dot(tc, p) for p in _split3(G))
        br = sum(_dot(p, tr) for p in _split3(GT))
        ig_sh = pltpu.roll(G, ML_HEADS, axis=1)
        b_end = bc[0:1, :] if rev else bc[L - 1:L, :]
        m0 = m_ref[0:1, :]
        logw = b_end - bc + ig_sh
        m_loc = jnp.max(logw, axis=0, keepdims=True)
        w_all = jnp.exp(logw - m_loc)
        m_new = jnp.maximum(b_end + m0, m_loc)
        a2_all = jnp.exp(b_end + m0 - m_new)
        g2_all = jnp.exp(m_loc - m_new)
        m_inter_all = bc + m0
        m_ref[0:1, :] = m_new

        for h in range(ML_HEADS):
            il = lane0 + h
            fl = il + ML_HEADS
            cs = slice(h * LANE, (h + 1) * LANE)
            qh = q_ref[r0:r0 + L, cs]
            kh = k_ref[r0:r0 + L, cs]
            vh = v_ref[r0:r0 + L, cs]
            bcol = bc[:, fl:fl + 1]
            brow = br[fl:fl + 1, :]
            igrow = GT[il:il + 1, :]
            logd = jnp.where(mask, bcol - brow + igrow, -jnp.inf)
            m_inter = m_inter_all[:, fl:fl + 1]
            m = jnp.maximum(m_inter, jnp.max(logd, axis=1, keepdims=True))
            dmat = jnp.exp(logd - m)
            a = jnp.exp(m_inter - m)
            s = _dot_nt(qh, kh) * dmat
            vaug = jnp.concatenate([vh, ones_v], axis=1)
            c_old = c_ref[h]
            num_aug = _dot(s.astype(BF16), vaug.astype(BF16)) + a * _dot(qh, c_old.astype(BF16))
            num = num_aug[:, :ML_DV]
            den = num_aug[:, ML_DV:]
            h_ref[r0:r0 + L, cs] = num / jnp.maximum(jnp.abs(den), jnp.exp(-m))
            w = w_all[:, fl:fl + 1]
            wv = (w * vaug).astype(BF16)
            khT = kh.astype(F32).T.astype(BF16)
            c_loc = _dot(khT, wv)
            c_ref[h] = a2_all[:, fl:fl + 1] * c_old + g2_all[:, fl:fl + 1] * c_loc


def _mlstm(q, k, v, gate, nb, t, rev):
    rows = q.shape[0]
    blk = SEQ_BLOCK
    nxb = t // blk
    ctx_blk0 = nb * nxb

    def imap(b, i):
        xi = (nxb - i) if rev else (i - 1)
        return (jnp.where(i == 0, ctx_blk0 + b, b * nxb + xi), 0)

    spec = lambda c: pl.BlockSpec((blk, c), imap)
    return pl.pallas_call(
        functools.partial(_mlstm_body, rev=rev),
        grid=(nb, nxb + 1),
        in_specs=[spec(512), spec(512), spec(512), spec(LANE)],
        out_specs=spec(512),
        out_shape=jax.ShapeDtypeStruct((rows, 512), F32),
        scratch_shapes=[pltpu.VMEM((ML_HEADS, LANE, 2 * ML_DV), F32), pltpu.VMEM((SUBLANE, LANE), F32)],
        compiler_params=_cparams("arbitrary", "arbitrary"),
        name="mlstm_bwd" if rev else "mlstm_fwd",
    )(q, k, v, gate)


def _hyb_comb_body(hf_ref, hb_ref, o_ref, cb_ref, z_ref, zp_ref, zn_ref, mg_ref, cw_ref, out_ref, *, starts, ends):
    tm = hf_ref.shape[0]
    i = pl.program_id(0)
    for h in range(ML_HEADS):
        cs = slice(h * ML_DV, (h + 1) * ML_DV)
        blk = hf_ref[:, cs] + hb_ref[:, cs]
        var = jnp.mean(blk * blk, axis=-1, keepdims=True)
        hn = blk * lax.rsqrt(var + EPS) * mg_ref[:, cs]
        out_ref[:, cs] = (hn * jax.nn.sigmoid(o_ref[:, cs])).astype(BF16)
    z = z_ref[...]
    loc = lax.broadcasted_iota(jnp.int32, (tm, 1), 0)
    row = loc + i * tm
    is_start = functools.reduce(jnp.logical_or, [row == r for r in starts])
    is_end = functools.reduce(jnp.logical_or, [row == r for r in ends])
    zprev = jnp.where(loc == 0, zp_ref[SUBLANE - 1:SUBLANE, :], pltpu.roll(z, 1, axis=0))
    zprev = jnp.where(is_start, 0.0, zprev)
    znext = jnp.where(loc == tm - 1, zn_ref[0:1, :], pltpu.roll(z, tm - 1, axis=0))
    znext = jnp.where(is_end, 0.0, znext)
    conv = cw_ref[0:1, :] * zprev + cw_ref[1:2, :] * z + cw_ref[2:3, :] * znext
    out_ref[:, 4 * ML_DV:] = (cb_ref[...] * conv).astype(BF16)


def _hyb_combine(hf, hb, o, cb, z, mg, cw, starts, ends):
    rows = hf.shape[0]
    tm = TOKEN_TILE
    per = tm // SUBLANE
    nblk = rows // SUBLANE
    row = lambda i: (i, 0)
    const = lambda i: (0, 0)
    return pl.pallas_call(
        functools.partial(_hyb_comb_body, starts=starts, ends=ends),
        grid=(rows // tm,),
        in_specs=[pl.BlockSpec((tm, 512), row)] * 5 + [
            pl.BlockSpec((SUBLANE, 512), lambda i: (jnp.maximum(i * per - 1, 0), 0)),
            pl.BlockSpec((SUBLANE, 512), lambda i: (jnp.minimum((i + 1) * per, nblk - 1), 0)),
            pl.BlockSpec((1, 512), const),
            pl.BlockSpec((SUBLANE, 512), const)],
        out_specs=pl.BlockSpec((tm, D_MODEL), row),
        out_shape=jax.ShapeDtypeStruct((rows, D_MODEL), BF16),
        compiler_params=_cparams("arbitrary"),
        name="hyb_combine",
    )(hf, hb, o, cb, z, z, z, mg, cw)


CONST_LANE = HEAD_DIM
LOG2E = 1.4426950408889634
BOUND_MARGIN = 1.02
MAX_BOUND_LOG2 = 48.0


def _att_in_body(x_ref, mod_ref, g_ref, w_ref, qg_ref, kg_ref, cos_ref, sin_ref, bd_ref, pads_ref,
                 q_ref, k_ref, vt_ref):
    d = D_MODEL
    h = _normmod(x_ref[...], g_ref[...], mod_ref[0, :, 0:d], mod_ref[0, :, d:2 * d]).astype(BF16)
    cosv = cos_ref[...]
    sinv = sin_ref[...]
    bd = bd_ref[...]
    lane = lax.broadcasted_iota(jnp.int32, cosv.shape, 1)
    upper_half = ((lane // (HEAD_DIM // 4)) % 2) == 1

    def norm_rope(p, gain, scale, pad):
        ssq = _dot((p * p).astype(BF16), bd)
        pn = p * lax.rsqrt(ssq * (1.0 / HEAD_DIM) + EPS) * gain
        outs = []
        for half in range(2):
            xh = pn[:, half * LANE:(half + 1) * LANE]
            partner = jnp.where(upper_half, pltpu.roll(xh, HEAD_DIM // 4, axis=1),
                                pltpu.roll(xh, LANE - HEAD_DIM // 4, axis=1))
            outs.append((xh * cosv + partner * sinv) * scale + pad)
        return jnp.concatenate(outs, axis=1)

    q_scale = HEAD_DIM ** -0.5 * LOG2E
    for j in range(ATT_Q_COLS // 256):
        p = _dot(h, w_ref[:, j * 256:(j + 1) * 256])
        q_ref[:, j * 256:(j + 1) * 256] = norm_rope(p, qg_ref[...], q_scale, pads_ref[0:1, :]).astype(BF16)
    for j in range(ATT_K_COLS // 256):
        c0 = ATT_Q_COLS + j * 256
        p = _dot(h, w_ref[:, c0:c0 + 256])
        k_ref[:, j * 256:(j + 1) * 256] = norm_rope(p, kg_ref[...], 1.0, pads_ref[1:2, :]).astype(BF16)
    c0 = ATT_Q_COLS + ATT_K_COLS
    vpad = pads_ref[2:3, :]
    v = _dot(h, w_ref[:, c0:c0 + ATT_K_COLS]) + jnp.concatenate([vpad] * KV_HEADS, axis=1)
    vt_ref[0] = v.T.astype(BF16)


def _att_in(stream, mod, g1, w, qg, kg, cos_t, sin_t, bd, pads, n_xt):
    rows, d = stream.shape
    tm = TOKEN_TILE
    nt = rows // tm
    ncol = w.shape[1]
    row = lambda i: (i, 0)
    const = lambda i: (0, 0)
    tab = lambda i: (jnp.where(i == nt - 1, n_xt, i % n_xt), 0)
    return pl.pallas_call(
        _att_in_body,
        grid=(nt,),
        in_specs=[pl.BlockSpec((tm, d), row),
                  pl.BlockSpec((1, 1, 6 * d), lambda i: (i // n_xt, 0, 0)),
                  pl.BlockSpec((1, d), const),
                  pl.BlockSpec((d, ncol), const),
                  pl.BlockSpec((1, 256), const),
                  pl.BlockSpec((1, 256), const),
                  pl.BlockSpec((tm, LANE), tab),
                  pl.BlockSpec((tm, LANE), tab),
                  pl.BlockSpec((256, 256), const),
                  pl.BlockSpec((SUBLANE, LANE), const)],
        out_specs=[pl.BlockSpec((tm, ATT_Q_COLS), row),
                   pl.BlockSpec((tm, ATT_K_COLS), row),
                   pl.BlockSpec((1, ATT_K_COLS, tm), lambda i: (i, 0, 0))],
        out_shape=[jax.ShapeDtypeStruct((rows, ATT_Q_COLS), BF16),
                   jax.ShapeDtypeStruct((rows, ATT_K_COLS), BF16),
                   jax.ShapeDtypeStruct((nt, ATT_K_COLS, tm), BF16)],
        compiler_params=_cparams("arbitrary"),
        name="att_in",
    )(stream, mod, g1, w, qg, kg, cos_t, sin_t, bd, pads)


def _store_heads(o_ref, outs_t):
    keep = lax.broadcasted_iota(jnp.int32, (1, LANE), 1) < HEAD_DIM
    outs = [jnp.where(keep, o.T, 0.0) for o in outs_t]
    for pair in range(GROUP // 2):
        packed = outs[2 * pair] + pltpu.roll(outs[2 * pair + 1], HEAD_DIM, axis=1)
        o_ref[:, pair * LANE:(pair + 1) * LANE] = packed.astype(BF16)


def _flash_safe_body(q_ref, kc_ref, vtc_ref, kx_ref, vtx_ref, o_ref):
    is_latent = pl.program_id(2) < pl.num_programs(2) - 1
    tq = q_ref.shape[0]
    qs = [q_ref[:, g * LANE:(g + 1) * LANE] for g in range(GROUP)]

    def step(kt, vt, carry):
        new = []
        for g in range(GROUP):
            m, acc = carry[g]
            st = _dot_nt(kt, qs[g])
            m_new = jnp.maximum(m, jnp.max(st, axis=0, keepdims=True))
            p = jnp.exp2(st - m_new)
            acc_new = jnp.exp2(m - m_new) * acc + _dot(vt, p.astype(BF16))
            new.append((m_new, acc_new))
        return tuple(new)

    init = tuple((jnp.full((1, tq), -jnp.inf, F32), jnp.zeros((LANE, tq), F32)) for _ in range(GROUP))
    carry = step(kc_ref[...], vtc_ref[0], init)
    carry = lax.fori_loop(
        0, jnp.where(is_latent, vtx_ref.shape[0], 0),
        lambda j, cr: step(kx_ref[pl.ds(pl.multiple_of(j * KV_TILE, KV_TILE), KV_TILE), :], vtx_ref[j], cr),
        carry)
    _store_heads(o_ref, [acc / acc[CONST_LANE:CONST_LANE + 1, :] for _, acc in carry])


def _flash_bounded_body(q_ref, kc_ref, vtc_ref, kx_ref, vtx_ref, o_ref, qa_ref, acc_ref):
    tq = q_ref.shape[0]
    for g in range(GROUP):
        qa_ref[g * tq:(g + 1) * tq, :] = q_ref[:, g * LANE:(g + 1) * LANE]

    def chunk(kt, vt):
        st = _dot_nt(kt, qa_ref[...])
        return _dot(vt, jnp.exp2(st).astype(BF16))

    acc_ref[...] = chunk(kc_ref[...], vtc_ref[0])

    def body(j, carry):
        acc_ref[...] += chunk(kx_ref[pl.ds(pl.multiple_of(j * KV_TILE, KV_TILE), KV_TILE), :], vtx_ref[j])
        return carry

    @pl.when(pl.program_id(2) < pl.num_programs(2) - 1)
    def _():
        lax.fori_loop(0, vtx_ref.shape[0], body, 0, unroll=8)

    outs = []
    for g in range(GROUP):
        a = acc_ref[:, g * tq:(g + 1) * tq]
        outs.append(a * (1.0 / a[CONST_LANE:CONST_LANE + 1, :]))
    _store_heads(o_ref, outs)


def _flash(q, k, vt, nb, t, lc, bounded):
    rows = q.shape[0]
    tq = SEQ_BLOCK
    n_xt = t // KV_TILE
    nq = t // tq
    ctx_q0 = nb * nq
    ctx_tile = nb * t // KV_TILE
    qmap = lambda b, h, i: (jnp.where(i == nq, ctx_q0 + b, b * nq + i), h)
    in_specs = [pl.BlockSpec((tq, GROUP * LANE), qmap),
                pl.BlockSpec((lc, LANE), lambda b, h, i: (nb * t // lc + b, h)),
                pl.BlockSpec((1, LANE, lc), lambda b, h, i: (ctx_tile, h, b)),
                pl.BlockSpec((t, LANE), lambda b, h, i: (b, h)),
                pl.BlockSpec((n_xt, LANE, KV_TILE), lambda b, h, i: (b, h, 0))]
    if bounded:
        body = _flash_bounded_body
        scratch = [pltpu.VMEM((GROUP * tq, LANE), BF16), pltpu.VMEM((LANE, GROUP * tq), F32)]
    else:
        body = _flash_safe_body
        scratch = []
    return pl.pallas_call(
        body,
        grid=(nb, KV_HEADS, nq + 1),
        in_specs=in_specs,
        out_specs=pl.BlockSpec((tq, GROUP * HEAD_DIM), qmap),
        out_shape=jax.ShapeDtypeStruct((rows, ATT_HEADS * HEAD_DIM), BF16),
        scratch_shapes=scratch,
        compiler_params=_cparams("arbitrary", "arbitrary", "arbitrary"),
        name="flash_bounded" if bounded else "flash_safe",
    )(q, k, vt, k, vt)


def _outproj_mlp_body(y_ref, x_ref, mod_ref, g_ref, wo_ref, w1_ref, w2_ref, out_ref):
    d = D_MODEL
    mod = lambda k: mod_ref[0, :, k * d:(k + 1) * d]
    x1 = x_ref[...] + mod(2) * _dot(y_ref[...], wo_ref[...])
    h2 = _normmod(x1, g_ref[...], mod(3), mod(4)).astype(BF16)
    acc = jnp.zeros_like(x1)
    for j in range(MLP_HIDDEN // d):
        u = jnp.maximum(_dot(h2, w1_ref[:, j * d:(j + 1) * d]), 0.0)
        acc = acc + _dot((u * u).astype(BF16), w2_ref[j * d:(j + 1) * d, :])
    out_ref[...] = x1 + mod(5) * acc


def _outproj_mlp(y, stream, mod, g2, wo, w1, w2, n_tiles, n_xt):
    d = D_MODEL
    tm = TOKEN_TILE
    row = lambda i: (i, 0)
    const = lambda i: (0, 0)
    return pl.pallas_call(
        _outproj_mlp_body,
        grid=(n_tiles,),
        in_specs=[pl.BlockSpec((tm, d), row),
                  pl.BlockSpec((tm, d), row),
                  pl.BlockSpec((1, 1, 6 * d), lambda i: (i // n_xt, 0, 0)),
                  pl.BlockSpec((1, d), const),
                  pl.BlockSpec((d, d), const),
                  pl.BlockSpec((d, MLP_HIDDEN), const),
                  pl.BlockSpec((MLP_HIDDEN, d), const)],
        out_specs=pl.BlockSpec((tm, d), row),
        out_shape=jax.ShapeDtypeStruct((n_tiles * tm, d), F32),
        compiler_params=_cparams("arbitrary"),
        name="outproj_mlp",
    )(y, stream, mod, g2, wo, w1, w2)


def _pad_heads(w, n_heads, width):
    d = w.shape[0]
    w = w.reshape(d, n_heads, width)
    return jnp.pad(w, ((0, 0), (0, 0), (0, LANE - width))).reshape(d, n_heads * LANE)


def _hyb_weights(w_in, gate_b):
    sizes = (ML_HEADS * ML_DK, ML_HEADS * ML_DK, ML_HEADS * ML_DV, ML_HEADS * ML_DV, 4 * ML_HEADS,
             SC_WIDTH, SC_WIDTH, SC_WIDTH)
    parts, c0 = [], 0
    for s in sizes:
        parts.append(w_in[:, c0:c0 + s])
        c0 += s
    wq, wk, wv, wo, wg, wb, wc, wu = parts
    wg = jnp.pad(wg, ((0, 0), (0, LANE - wg.shape[1])))
    w = jnp.concatenate([_pad_heads(wq, ML_HEADS, ML_DK), _pad_heads(wk, ML_HEADS, ML_DK), wv, wo, wg, wb, wc, wu],
                        axis=1).astype(BF16)
    gb = jnp.pad(gate_b.astype(F32), (0, LANE - gate_b.shape[0])).reshape(1, LANE)
    return w, gb


def _att_weights(w_in):
    nq = ATT_HEADS * HEAD_DIM
    nk = KV_HEADS * HEAD_DIM
    return jnp.concatenate([_pad_heads(w_in[:, :nq], ATT_HEADS, HEAD_DIM),
                            _pad_heads(w_in[:, nq:nq + nk], KV_HEADS, HEAD_DIM),
                            _pad_heads(w_in[:, nq + nk:], KV_HEADS, HEAD_DIM)], axis=1).astype(BF16)


def _rope_tables(t, n_ident):
    half, quarter = HEAD_DIM // 2, HEAD_DIM // 4
    inv = ROPE_THETA ** (-jnp.arange(0, half, 2, dtype=F32) / half)
    pos = jnp.arange(t)
    r = (pos // GRID_W).astype(F32)[:, None] * inv
    c = (pos % GRID_W).astype(F32)[:, None] * inv
    one = jnp.ones((t, LANE - HEAD_DIM), F32)
    cos_t = jnp.concatenate([jnp.cos(r), jnp.cos(r), jnp.cos(c), jnp.cos(c), one], axis=1)
    sin_t = jnp.concatenate([-jnp.sin(r), jnp.sin(r), -jnp.sin(c), jnp.sin(c), 0.0 * one], axis=1)
    assert quarter * 4 == HEAD_DIM
    cos_t = jnp.concatenate([cos_t, jnp.ones((n_ident, LANE), F32)], axis=0)
    sin_t = jnp.concatenate([sin_t, jnp.zeros((n_ident, LANE), F32)], axis=0)
    return cos_t, sin_t


def kernel(x, c, ctx, c_ctx, ada_w, ada_b, norm1_g, norm2_g, mlp_w1, mlp_w2, hyb_w_in, hyb_gate_b, mlstm_norm_g,
           conv_w, hyb_w_out, att_w_in, q_norm_g, k_norm_g, att_w_out):
    nb, t, d = x.shape
    lc = ctx.shape[1]
    assert d == D_MODEL and nb * lc == TOKEN_TILE and lc == SEQ_BLOCK and t % TOKEN_TILE == 0
    assert nb + 1 <= SUBLANE
    n_xt = t // TOKEN_TILE
    n_tiles = nb * n_xt + 1

    stream = jnp.concatenate([x.reshape(nb * t, d), ctx.reshape(nb * lc, d)], axis=0).astype(F32)
    cvec = jnp.concatenate([c, c_ctx[None, :], jnp.zeros((SUBLANE - nb - 1, d), c.dtype)], axis=0).astype(F32)
    mods = _adaln(cvec, ada_w.astype(F32), ada_b.astype(F32))

    cos_t, sin_t = _rope_tables(t, TOKEN_TILE)
    blk = lax.broadcasted_iota(jnp.int32, (256, 256), 0) // LANE == lax.broadcasted_iota(jnp.int32, (256, 256), 1) // LANE
    bd = blk.astype(BF16)
    pad_gain = lambda g: jnp.tile(jnp.pad(g.astype(F32), (0, LANE - HEAD_DIM)), 2).reshape(1, 2 * LANE)
    seq_starts = tuple(b * t for b in range(nb)) + tuple(nb * t + b * lc for b in range(nb))
    seq_ends = tuple((b + 1) * t - 1 for b in range(nb)) + tuple(nb * t + (b + 1) * lc - 1 for b in range(nb))

    for layer in range(DEPTH):
        last = layer == DEPTH - 1
        mod = mods[layer, :nb + 1].reshape(nb + 1, 1, 6 * d)
        g1 = norm1_g[layer].astype(F32).reshape(1, d)
        g2 = norm2_g[layer].astype(F32).reshape(1, d)
        if layer % 2 == 0:
            e = layer // 2
            w, gb = _hyb_weights(hyb_w_in[e], hyb_gate_b[e])
            q, k, v, o, gate, cb, z = _hyb_in(stream, mod, g1, w, gb, n_xt)
            hf = _mlstm(q, k, v, gate, nb, t, rev=False)
            hb = _mlstm(q, k, v, gate, nb, t, rev=True)
            cw = jnp.pad(conv_w[e].astype(F32), ((0, SUBLANE - CONV_K), (0, 0)))
            y = _hyb_combine(hf, hb, o, cb, z, mlstm_norm_g[e].astype(F32).reshape(1, -1), cw, seq_starts, seq_ends)
            wo = hyb_w_out[e].astype(BF16)
        else:
            a = layer // 2
            bound = (HEAD_DIM ** 0.5 * LOG2E * BOUND_MARGIN) * jnp.max(jnp.abs(q_norm_g[a])) * jnp.max(jnp.abs(k_norm_g[a]))
            bound = bound.astype(F32)
            lane_const = (jnp.arange(LANE) == CONST_LANE).astype(F32)
            pads = jnp.stack([-bound * lane_const, lane_const, lane_const] + [0.0 * lane_const] * (SUBLANE - 3))
            q, k, vt = _att_in(stream, mod, g1, _att_weights(att_w_in[a]), pad_gain(q_norm_g[a]),
                               pad_gain(k_norm_g[a]), cos_t, sin_t, bd, pads, n_xt)
            y = lax.cond(bound <= MAX_BOUND_LOG2,
                         lambda: _flash(q, k, vt, nb, t, lc, bounded=True),
                         lambda: _flash(q, k, vt, nb, t, lc, bounded=False))
            wo = att_w_out[a].astype(BF16)
        stream = _outproj_mlp(y, stream, mod, g2, wo, mlp_w1[layer].astype(BF16), mlp_w2[layer].astype(BF16),
                              nb * n_xt if last else n_tiles, n_xt)
    return stream[:nb * t].reshape(nb, t, d).astype(x.dtype)
```

```python
import functools

import jax
import jax.numpy as jnp
from jax import lax
from jax.experimental import pallas as pl
from jax.experimental.pallas import tpu as pltpu

F32 = jnp.float32
BF16 = jnp.bfloat16

D_MODEL = 1024
DEPTH = 4
GRID_W = 64
EPS = 1e-6
ML_HEADS = 4
ML_DK = 64
ML_DV = 128
ML_CHUNK = 64
SC_WIDTH = D_MODEL // 2
CONV_K = 3
ATT_HEADS = 16
KV_HEADS = 4
HEAD_DIM = 64
GROUP = ATT_HEADS // KV_HEADS
ROPE_THETA = 10000.0
MLP_HIDDEN = 4 * D_MODEL

LANE = 128
SUBLANE = 8
TOKEN_TILE = 512
SEQ_BLOCK = 256
SCAN_CHUNK = SEQ_BLOCK
KV_TILE = TOKEN_TILE
VMEM_LIMIT_BYTES = 56 * 1024 * 1024

HYB_COLS = (4 * LANE, 4 * LANE, 4 * ML_DV, 4 * ML_DV, LANE, SC_WIDTH, SC_WIDTH, SC_WIDTH)
ATT_Q_COLS = ATT_HEADS * LANE
ATT_K_COLS = KV_HEADS * LANE


def _cparams(*sem):
    return pltpu.CompilerParams(dimension_semantics=sem, vmem_limit_bytes=VMEM_LIMIT_BYTES)


def _normmod(xf, g, shift, scale):
    var = jnp.mean(xf * xf, axis=-1, keepdims=True)
    return xf * lax.rsqrt(var + EPS) * g * (1.0 + scale) + shift


def _dot(a, b):
    return jnp.dot(a, b, preferred_element_type=F32)


def _dot_nt(a, b):
    return lax.dot_general(a, b, (((1,), (1,)), ((), ())), preferred_element_type=F32)


def _adaln_body(c_ref, w_ref, b_ref, o_ref):
    cv = c_ref[...]
    s = cv * jax.nn.sigmoid(cv)
    o_ref[0] = _dot(s.astype(BF16), w_ref[0].astype(BF16)) + b_ref[0]


def _adaln(cvec, ada_w, ada_b):
    depth, d, n = ada_w.shape
    bn = n // 4
    return pl.pallas_call(
        _adaln_body,
        grid=(depth, n // bn),
        in_specs=[pl.BlockSpec((SUBLANE, d), lambda l, j: (0, 0)),
                  pl.BlockSpec((1, d, bn), lambda l, j: (l, 0, j)),
                  pl.BlockSpec((1, 1, bn), lambda l, j: (l, 0, j))],
        out_specs=pl.BlockSpec((1, SUBLANE, bn), lambda l, j: (l, 0, j)),
        out_shape=jax.ShapeDtypeStruct((depth, SUBLANE, n), F32),
        compiler_params=_cparams("arbitrary", "arbitrary"),
        name="adaln",
    )(cvec, ada_w, ada_b.reshape(depth, 1, n))


def _hyb_in_body(x_ref, mod_ref, g_ref, w_ref, gb_ref, q_ref, k_ref, v_ref, o_ref, gate_ref, cb_ref, z_ref):
    d = D_MODEL
    h = _normmod(x_ref[...], g_ref[...], mod_ref[0, :, 0:d], mod_ref[0, :, d:2 * d]).astype(BF16)
    c0 = 0
    q_ref[...] = (_dot(h, w_ref[:, c0:c0 + 512]) * (ML_DK ** -0.5)).astype(BF16)
    c0 += 512
    k_ref[...] = _dot(h, w_ref[:, c0:c0 + 512]).astype(BF16)
    c0 += 512
    v_ref[...] = _dot(h, w_ref[:, c0:c0 + 512])
    c0 += 512
    o_ref[...] = _dot(h, w_ref[:, c0:c0 + 512])
    c0 += 512
    g = _dot(h, w_ref[:, c0:c0 + LANE]) + gb_ref[...]
    c0 += LANE
    lane = lax.broadcasted_iota(jnp.int32, g.shape, 1)
    is_forget = ((lane // ML_HEADS) % 2) == 1
    logsig = jnp.minimum(g, 0.0) - jnp.log1p(jnp.exp(-jnp.abs(g)))
    gate_ref[...] = jnp.where(is_forget, logsig, g)
    cb_ref[...] = _dot(h, w_ref[:, c0:c0 + 512])
    c0 += 512
    gc = _dot(h, w_ref[:, c0:c0 + 512])
    c0 += 512
    z_ref[...] = gc * _dot(h, w_ref[:, c0:c0 + 512])


def _hyb_in(stream, mod, g1, w, gate_b, n_xt):
    rows, d = stream.shape
    tm = TOKEN_TILE
    ncol = w.shape[1]
    row = lambda i: (i, 0)
    const = lambda i: (0, 0)
    outs = [(512, BF16), (512, BF16), (512, F32), (512, F32), (LANE, F32), (512, F32), (512, F32)]
    return pl.pallas_call(
        _hyb_in_body,
        grid=(rows // tm,),
        in_specs=[pl.BlockSpec((tm, d), row),
                  pl.BlockSpec((1, 1, 6 * d), lambda i: (i // n_xt, 0, 0)),
                  pl.BlockSpec((1, d), const),
                  pl.BlockSpec((d, ncol), const),
                  pl.BlockSpec((1, LANE), const)],
        out_specs=[pl.BlockSpec((tm, c), row) for c, _ in outs],
        out_shape=[jax.ShapeDtypeStruct((rows, c), t) for c, t in outs],
        compiler_params=_cparams("arbitrary"),
        name="hyb_in",
    )(stream, mod, g1, w, gate_b)


def _split3(a):
    hi = a.astype(BF16)
    r = a - hi.astype(F32)
    mid = r.astype(BF16)
    lo = (r - mid.astype(F32)).astype(BF16)
    return hi, mid, lo


def _mlstm_body(q_ref, k_ref, v_ref, gate_ref, h_ref, c_ref, m_ref, *, rev):
    L = SCAN_CHUNK

    @pl.when(pl.program_id(1) == 0)
    def _():
        c_ref[...] = jnp.zeros_like(c_ref)
        m_ref[...] = jnp.zeros_like(m_ref)

    jj = lax.broadcasted_iota(jnp.int32, (L, L), 0)
    ss = lax.broadcasted_iota(jnp.int32, (L, L), 1)
    lower = ss <= jj
    upper = ss >= jj
    mask = upper if rev else lower
    tc = mask.astype(BF16)
    tr = (lower if rev else upper).astype(BF16)
    ones_v = jnp.ones((L, ML_DV), F32)
    n_chunks = q_ref.shape[0] // L
    order = range(n_chunks - 1, -1, -1) if rev else range(n_chunks)
    lane0 = 2 * ML_HEADS if rev else 0

    for c in order:
        r0 = c * L
        G = gate_ref[r0:r0 + L, :]
        GT = G.T
        bc = sum(_dot(tc, p) for p in _split3(G))
        br = sum(_dot(p, tr) for p in _split3(GT))
        ig_sh = pltpu.roll(G, ML_HEADS, axis=1)
        b_end = bc[0:1, :] if rev else bc[L - 1:L, :]
        m0 = m_ref[0:1, :]
        logw = b_end - bc + ig_sh
        m_loc = jnp.max(logw, axis=0, keepdims=True)
        w_all = jnp.exp(logw - m_loc)
        m_new = jnp.maximum(b_end + m0, m_loc)
        a2_all = jnp.exp(b_end + m0 - m_new)
        g2_all = jnp.exp(m_loc - m_new)
        m_inter_all = bc + m0
        m_ref[0:1, :] = m_new

        for h in range(ML_HEADS):
            il = lane0 + h
            fl = il + ML_HEADS
            cs = slice(h * LANE, (h + 1) * LANE)
            qh = q_ref[r0:r0 + L, cs]
            kh = k_ref[r0:r0 + L, cs]
            vh = v_ref[r0:r0 + L, cs]
            bcol = bc[:, fl:fl + 1]
            brow = br[fl:fl + 1, :]
            igrow = GT[il:il + 1, :]
            logd = jnp.where(mask, bcol - brow + igrow, -jnp.inf)
            m_inter = m_inter_all[:, fl:fl + 1]
            m = jnp.maximum(m_inter, jnp.max(logd, axis=1, keepdims=True))
            dmat = jnp.exp(logd - m)
            a = jnp.exp(m_inter - m)
            s = _dot_nt(qh, kh) * dmat
            vaug = jnp.concatenate([vh, ones_v], axis=1)
            c_old = c_ref[h]
            num_aug = _dot(s.astype(BF16), vaug.astype(BF16)) + a * _dot(qh, c_old.astype(BF16))
            num = num_aug[:, :ML_DV]
            den = num_aug[:, ML_DV:]
            h_ref[r0:r0 + L, cs] = num / jnp.maximum(jnp.abs(den), jnp.exp(-m))
            w = w_all[:, fl:fl + 1]
            wv = (w * vaug).astype(BF16)
            khT = kh.astype(F32).T.astype(BF16)
            c_loc = _dot(khT, wv)
            c_ref[h] = a2_all[:, fl:fl + 1] * c_old + g2_all[:, fl:fl + 1] * c_loc


def _mlstm(q, k, v, gate, nb, t, rev):
    rows = q.shape[0]
    blk = SEQ_BLOCK
    nxb = t // blk
    ctx_blk0 = nb * nxb

    def imap(b, i):
        xi = (nxb - i) if rev else (i - 1)
        return (jnp.where(i == 0, ctx_blk0 + b, b * nxb + xi), 0)

    spec = lambda c: pl.BlockSpec((blk, c), imap)
    return pl.pallas_call(
        functools.partial(_mlstm_body, rev=rev),
        grid=(nb, nxb + 1),
        in_specs=[spec(512), spec(512), spec(512), spec(LANE)],
        out_specs=spec(512),
        out_shape=jax.ShapeDtypeStruct((rows, 512), F32),
        scratch_shapes=[pltpu.VMEM((ML_HEADS, LANE, 2 * ML_DV), F32), pltpu.VMEM((SUBLANE, LANE), F32)],
        compiler_params=_cparams("arbitrary", "arbitrary"),
        name="mlstm_bwd" if rev else "mlstm_fwd",
    )(q, k, v, gate)


def _hyb_comb_body(hf_ref, hb_ref, o_ref, cb_ref, z_ref, zp_ref, zn_ref, mg_ref, cw_ref, out_ref, *, starts, ends):
    tm = hf_ref.shape[0]
    i = pl.program_id(0)
    for h in range(ML_HEADS):
        cs = slice(h * ML_DV, (h + 1) * ML_DV)
        blk = hf_ref[:, cs] + hb_ref[:, cs]
        var = jnp.mean(blk * blk, axis=-1, keepdims=True)
        hn = blk * lax.rsqrt(var + EPS) * mg_ref[:, cs]
        out_ref[:, cs] = (hn * jax.nn.sigmoid(o_ref[:, cs])).astype(BF16)
    z = z_ref[...]
    loc = lax.broadcasted_iota(jnp.int32, (tm, 1), 0)
    row = loc + i * tm
    is_start = functools.reduce(jnp.logical_or, [row == r for r in starts])
    is_end = functools.reduce(jnp.logical_or, [row == r for r in ends])
    zprev = jnp.where(loc == 0, zp_ref[SUBLANE - 1:SUBLANE, :], pltpu.roll(z, 1, axis=0))
    zprev = jnp.where(is_start, 0.0, zprev)
    znext = jnp.where(loc == tm - 1, zn_ref[0:1, :], pltpu.roll(z, tm - 1, axis=0))
    znext = jnp.where(is_end, 0.0, znext)
    conv = cw_ref[0:1, :] * zprev + cw_ref[1:2, :] * z + cw_ref[2:3, :] * znext
    out_ref[:, 4 * ML_DV:] = (cb_ref[...] * conv).astype(BF16)


def _hyb_combine(hf, hb, o, cb, z, mg, cw, starts, ends):
    rows = hf.shape[0]
    tm = TOKEN_TILE
    per = tm // SUBLANE
    nblk = rows // SUBLANE
    row = lambda i: (i, 0)
    const = lambda i: (0, 0)
    return pl.pallas_call(
        functools.partial(_hyb_comb_body, starts=starts, ends=ends),
        grid=(rows // tm,),
        in_specs=[pl.BlockSpec((tm, 512), row)] * 5 + [
            pl.BlockSpec((SUBLANE, 512), lambda i: (jnp.maximum(i * per - 1, 0), 0)),
            pl.BlockSpec((SUBLANE, 512), lambda i: (jnp.minimum((i + 1) * per, nblk - 1), 0)),
            pl.BlockSpec((1, 512), const),
            pl.BlockSpec((SUBLANE, 512), const)],
        out_specs=pl.BlockSpec((tm, D_MODEL), row),
        out_shape=jax.ShapeDtypeStruct((rows, D_MODEL), BF16),
        compiler_params=_cparams("arbitrary"),
        name="hyb_combine",
    )(hf, hb, o, cb, z, z, z, mg, cw)


CONST_LANE = HEAD_DIM
LOG2E = 1.4426950408889634
BOUND_MARGIN = 1.02
MAX_BOUND_LOG2 = 48.0


def _att_in_body(x_ref, mod_ref, g_ref, w_ref, qg_ref, kg_ref, cos_ref, sin_ref, bd_ref, pads_ref,
                 q_ref, k_ref, vt_ref):
    d = D_MODEL
    h = _normmod(x_ref[...], g_ref[...], mod_ref[0, :, 0:d], mod_ref[0, :, d:2 * d]).astype(BF16)
    cosv = cos_ref[...]
    sinv = sin_ref[...]
    bd = bd_ref[...]
    lane = lax.broadcasted_iota(jnp.int32, cosv.shape, 1)
    upper_half = ((lane // (HEAD_DIM // 4)) % 2) == 1

    def norm_rope(p, gain, scale, pad):
        ssq = _dot((p * p).astype(BF16), bd)
        pn = p * lax.rsqrt(ssq * (1.0 / HEAD_DIM) + EPS) * gain
        outs = []
        for half in range(2):
            xh = pn[:, half * LANE:(half + 1) * LANE]
            partner = jnp.where(upper_half, pltpu.roll(xh, HEAD_DIM // 4, axis=1),
                                pltpu.roll(xh, LANE - HEAD_DIM // 4, axis=1))
            outs.append((xh * cosv + partner * sinv) * scale + pad)
        return jnp.concatenate(outs, axis=1)

    q_scale = HEAD_DIM ** -0.5 * LOG2E
    for j in range(ATT_Q_COLS // 256):
        p = _dot(h, w_ref[:, j * 256:(j + 1) * 256])
        q_ref[:, j * 256:(j + 1) * 256] = norm_rope(p, qg_ref[...], q_scale, pads_ref[0:1, :]).astype(BF16)
    for j in range(ATT_K_COLS // 256):
        c0 = ATT_Q_COLS + j * 256
        p = _dot(h, w_ref[:, c0:c0 + 256])
        k_ref[:, j * 256:(j + 1) * 256] = norm_rope(p, kg_ref[...], 1.0, pads_ref[1:2, :]).astype(BF16)
    c0 = ATT_Q_COLS + ATT_K_COLS
    vpad = pads_ref[2:3, :]
    v = _dot(h, w_ref[:, c0:c0 + ATT_K_COLS]) + jnp.concatenate([vpad] * KV_HEADS, axis=1)
    vt_ref[0] = v.T.astype(BF16)


def _att_in(stream, mod, g1, w, qg, kg, cos_t, sin_t, bd, pads, n_xt):
    rows, d = stream.shape
    tm = TOKEN_TILE
    nt = rows // tm
    ncol = w.shape[1]
    row = lambda i: (i, 0)
    const = lambda i: (0, 0)
    tab = lambda i: (jnp.where(i == nt - 1, n_xt, i % n_xt), 0)
    return pl.pallas_call(
        _att_in_body,
        grid=(nt,),
        in_specs=[pl.BlockSpec((tm, d), row),
                  pl.BlockSpec((1, 1, 6 * d), lambda i: (i // n_xt, 0, 0)),
                  pl.BlockSpec((1, d), const),
                  pl.BlockSpec((d, ncol), const),
                  pl.BlockSpec((1, 256), const),
                  pl.BlockSpec((1, 256), const),
                  pl.BlockSpec((tm, LANE), tab),
                  pl.BlockSpec((tm, LANE), tab),
                  pl.BlockSpec((256, 256), const),
                  pl.BlockSpec((SUBLANE, LANE), const)],
        out_specs=[pl.BlockSpec((tm, ATT_Q_COLS), row),
                   pl.BlockSpec((tm, ATT_K_COLS), row),
                   pl.BlockSpec((1, ATT_K_COLS, tm), lambda i: (i, 0, 0))],
        out_shape=[jax.ShapeDtypeStruct((rows, ATT_Q_COLS), BF16),
                   jax.ShapeDtypeStruct((rows, ATT_K_COLS), BF16),
                   jax.ShapeDtypeStruct((nt, ATT_K_COLS, tm), BF16)],
        compiler_params=_cparams("arbitrary"),
        name="att_in",
    )(stream, mod, g1, w, qg, kg, cos_t, sin_t, bd, pads)


def _store_heads(o_ref, outs_t):
    keep = lax.broadcasted_iota(jnp.int32, (1, LANE), 1) < HEAD_DIM
    outs = [jnp.where(keep, o.T, 0.0) for o in outs_t]
    for pair in range(GROUP // 2):
        packed = outs[2 * pair] + pltpu.roll(outs[2 * pair + 1], HEAD_DIM, axis=1)
        o_ref[:, pair * LANE:(pair + 1) * LANE] = packed.astype(BF16)


def _flash_safe_body(q_ref, kc_ref, vtc_ref, kx_ref, vtx_ref, o_ref):
    is_latent = pl.program_id(2) < pl.num_programs(2) - 1
    tq = q_ref.shape[0]
    qs = [q_ref[:, g * LANE:(g + 1) * LANE] for g in range(GROUP)]

    def step(kt, vt, carry):
        new = []
        for g in range(GROUP):
            m, acc = carry[g]
            st = _dot_nt(kt, qs[g])
            m_new = jnp.maximum(m, jnp.max(st, axis=0, keepdims=True))
            p = jnp.exp2(st - m_new)
            acc_new = jnp.exp2(m - m_new) * acc + _dot(vt, p.astype(BF16))
            new.append((m_new, acc_new))
        return tuple(new)

    init = tuple((jnp.full((1, tq), -jnp.inf, F32), jnp.zeros((LANE, tq), F32)) for _ in range(GROUP))
    carry = step(kc_ref[...], vtc_ref[0], init)
    carry = lax.fori_loop(
        0, jnp.where(is_latent, vtx_ref.shape[0], 0),
        lambda j, cr: step(kx_ref[pl.ds(pl.multiple_of(j * KV_TILE, KV_TILE), KV_TILE), :], vtx_ref[j], cr),
        carry)
    _store_heads(o_ref, [acc / acc[CONST_LANE:CONST_LANE + 1, :] for _, acc in carry])


def _flash_bounded_body(q_ref, kc_ref, vtc_ref, kx_ref, vtx_ref, o_ref, qa_ref, acc_ref):
    tq = q_ref.shape[0]
    for g in range(GROUP):
        qa_ref[g * tq:(g + 1) * tq, :] = q_ref[:, g * LANE:(g + 1) * LANE]

    def chunk(kt, vt):
        st = _dot_nt(kt, qa_ref[...])
        return _dot(vt, jnp.exp2(st).astype(BF16))

    acc_ref[...] = chunk(kc_ref[...], vtc_ref[0])

    def body(j, carry):
        acc_ref[...] += chunk(kx_ref[pl.ds(pl.multiple_of(j * KV_TILE, KV_TILE), KV_TILE), :], vtx_ref[j])
        return carry

    @pl.when(pl.program_id(2) < pl.num_programs(2) - 1)
    def _():
        lax.fori_loop(0, vtx_ref.shape[0], body, 0, unroll=8)

    outs = []
    for g in range(GROUP):
        a = acc_ref[:, g * tq:(g + 1) * tq]
        outs.append(a * (1.0 / a[CONST_LANE:CONST_LANE + 1, :]))
    _store_heads(o_ref, outs)


def _flash(q, k, vt, nb, t, lc, bounded):
    rows = q.shape[0]
    tq = SEQ_BLOCK
    n_xt = t // KV_TILE
    nq = t // tq
    ctx_q0 = nb * nq
    ctx_tile = nb * t // KV_TILE
    qmap = lambda b, h, i: (jnp.where(i == nq, ctx_q0 + b, b * nq + i), h)
    in_specs = [pl.BlockSpec((tq, GROUP * LANE), qmap),
                pl.BlockSpec((lc, LANE), lambda b, h, i: (nb * t // lc + b, h)),
                pl.BlockSpec((1, LANE, lc), lambda b, h, i: (ctx_tile, h, b)),
                pl.BlockSpec((t, LANE), lambda b, h, i: (b, h)),
                pl.BlockSpec((n_xt, LANE, KV_TILE), lambda b, h, i: (b, h, 0))]
    if bounded:
        body = _flash_bounded_body
        scratch = [pltpu.VMEM((GROUP * tq, LANE), BF16), pltpu.VMEM((LANE, GROUP * tq), F32)]
    else:
        body = _flash_safe_body
        scratch = []
    return pl.pallas_call(
        body,
        grid=(nb, KV_HEADS, nq + 1),
        in_specs=in_specs,
        out_specs=pl.BlockSpec((tq, GROUP * HEAD_DIM), qmap),
        out_shape=jax.ShapeDtypeStruct((rows, ATT_HEADS * HEAD_DIM), BF16),
        scratch_shapes=scratch,
        compiler_params=_cparams("arbitrary", "arbitrary", "arbitrary"),
        name="flash_bounded" if bounded else "flash_safe",
    )(q, k, vt, k, vt)


def _outproj_mlp_body(y_ref, x_ref, mod_ref, g_ref, wo_ref, w1_ref, w2_ref, out_ref):
    d = D_MODEL
    mod = lambda k: mod_ref[0, :, k * d:(k + 1) * d]
    x1 = x_ref[...] + mod(2) * _dot(y_ref[...], wo_ref[...])
    h2 = _normmod(x1, g_ref[...], mod(3), mod(4)).astype(BF16)
    acc = jnp.zeros_like(x1)
    for j in range(MLP_HIDDEN // d):
        u = jnp.maximum(_dot(h2, w1_ref[:, j * d:(j + 1) * d]), 0.0)
        acc = acc + _dot((u * u).astype(BF16), w2_ref[j * d:(j + 1) * d, :])
    out_ref[...] = x1 + mod(5) * acc


def _outproj_mlp(y, stream, mod, g2, wo, w1, w2, n_tiles, n_xt):
    d = D_MODEL
    tm = TOKEN_TILE
    row = lambda i: (i, 0)
    const = lambda i: (0, 0)
    return pl.pallas_call(
        _outproj_mlp_body,
        grid=(n_tiles,),
        in_specs=[pl.BlockSpec((tm, d), row),
                  pl.BlockSpec((tm, d), row),
                  pl.BlockSpec((1, 1, 6 * d), lambda i: (i // n_xt, 0, 0)),
                  pl.BlockSpec((1, d), const),
                  pl.BlockSpec((d, d), const),
                  pl.BlockSpec((d, MLP_HIDDEN), const),
                  pl.BlockSpec((MLP_HIDDEN, d), const)],
        out_specs=pl.BlockSpec((tm, d), row),
        out_shape=jax.ShapeDtypeStruct((n_tiles * tm, d), F32),
        compiler_params=_cparams("arbitrary"),
        name="outproj_mlp",
    )(y, stream, mod, g2, wo, w1, w2)


def _pad_heads(w, n_heads, width):
    d = w.shape[0]
    w = w.reshape(d, n_heads, width)
    return jnp.pad(w, ((0, 0), (0, 0), (0, LANE - width))).reshape(d, n_heads * LANE)


def _hyb_weights(w_in, gate_b):
    sizes = (ML_HEADS * ML_DK, ML_HEADS * ML_DK, ML_HEADS * ML_DV, ML_HEADS * ML_DV, 4 * ML_HEADS,
             SC_WIDTH, SC_WIDTH, SC_WIDTH)
    parts, c0 = [], 0
    for s in sizes:
        parts.append(w_in[:, c0:c0 + s])
        c0 += s
    wq, wk, wv, wo, wg, wb, wc, wu = parts
    wg = jnp.pad(wg, ((0, 0), (0, LANE - wg.shape[1])))
    w = jnp.concatenate([_pad_heads(wq, ML_HEADS, ML_DK), _pad_heads(wk, ML_HEADS, ML_DK), wv, wo, wg, wb, wc, wu],
                        axis=1).astype(BF16)
    gb = jnp.pad(gate_b.astype(F32), (0, LANE - gate_b.shape[0])).reshape(1, LANE)
    return w, gb


def _att_weights(w_in):
    nq = ATT_HEADS * HEAD_DIM
    nk = KV_HEADS * HEAD_DIM
    return jnp.concatenate([_pad_heads(w_in[:, :nq], ATT_HEADS, HEAD_DIM),
                            _pad_heads(w_in[:, nq:nq + nk], KV_HEADS, HEAD_DIM),
                            _pad_heads(w_in[:, nq + nk:], KV_HEADS, HEAD_DIM)], axis=1).astype(BF16)


def _rope_tables(t, n_ident):
    half, quarter = HEAD_DIM // 2, HEAD_DIM // 4
    inv = ROPE_THETA ** (-jnp.arange(0, half, 2, dtype=F32) / half)
    pos = jnp.arange(t)
    r = (pos // GRID_W).astype(F32)[:, None] * inv
    c = (pos % GRID_W).astype(F32)[:, None] * inv
    one = jnp.ones((t, LANE - HEAD_DIM), F32)
    cos_t = jnp.concatenate([jnp.cos(r), jnp.cos(r), jnp.cos(c), jnp.cos(c), one], axis=1)
    sin_t = jnp.concatenate([-jnp.sin(r), jnp.sin(r), -jnp.sin(c), jnp.sin(c), 0.0 * one], axis=1)
    assert quarter * 4 == HEAD_DIM
    cos_t = jnp.concatenate([cos_t, jnp.ones((n_ident, LANE), F32)], axis=0)
    sin_t = jnp.concatenate([sin_t, jnp.zeros((n_ident, LANE), F32)], axis=0)
    return cos_t, sin_t


def kernel(x, c, ctx, c_ctx, ada_w, ada_b, norm1_g, norm2_g, mlp_w1, mlp_w2, hyb_w_in, hyb_gate_b, mlstm_norm_g,
           conv_w, hyb_w_out, att_w_in, q_norm_g, k_norm_g, att_w_out):
    nb, t, d = x.shape
    lc = ctx.shape[1]
    assert d == D_MODEL and nb * lc == TOKEN_TILE and lc == SEQ_BLOCK and t % TOKEN_TILE == 0
    assert nb + 1 <= SUBLANE
    n_xt = t // TOKEN_TILE
    n_tiles = nb * n_xt + 1

    stream = jnp.concatenate([x.reshape(nb * t, d), ctx.reshape(nb * lc, d)], axis=0).astype(F32)
    cvec = jnp.concatenate([c, c_ctx[None, :], jnp.zeros((SUBLANE - nb - 1, d), c.dtype)], axis=0).astype(F32)
    mods = _adaln(cvec, ada_w.astype(F32), ada_b.astype(F32))

    cos_t, sin_t = _rope_tables(t, TOKEN_TILE)
    blk = lax.broadcasted_iota(jnp.int32, (256, 256), 0) // LANE == lax.broadcasted_iota(jnp.int32, (256, 256), 1) // LANE
    bd = blk.astype(BF16)
    pad_gain = lambda g: jnp.tile(jnp.pad(g.astype(F32), (0, LANE - HEAD_DIM)), 2).reshape(1, 2 * LANE)
    seq_starts = tuple(b * t for b in range(nb)) + tuple(nb * t + b * lc for b in range(nb))
    seq_ends = tuple((b + 1) * t - 1 for b in range(nb)) + tuple(nb * t + (b + 1) * lc - 1 for b in range(nb))

    for layer in range(DEPTH):
        last = layer == DEPTH - 1
        mod = mods[layer, :nb + 1].reshape(nb + 1, 1, 6 * d)
        g1 = norm1_g[layer].astype(F32).reshape(1, d)
        g2 = norm2_g[layer].astype(F32).reshape(1, d)
        if layer % 2 == 0:
            e = layer // 2
            w, gb = _hyb_weights(hyb_w_in[e], hyb_gate_b[e])
            q, k, v, o, gate, cb, z = _hyb_in(stream, mod, g1, w, gb, n_xt)
            hf = _mlstm(q, k, v, gate, nb, t, rev=False)
            hb = _mlstm(q, k, v, gate, nb, t, rev=True)
            cw = jnp.pad(conv_w[e].astype(F32), ((0, SUBLANE - CONV_K), (0, 0)))
            y = _hyb_combine(hf, hb, o, cb, z, mlstm_norm_g[e].astype(F32).reshape(1, -1), cw, seq_starts, seq_ends)
            wo = hyb_w_out[e].astype(BF16)
        else:
            a = layer // 2
            bound = (HEAD_DIM ** 0.5 * LOG2E * BOUND_MARGIN) * jnp.max(jnp.abs(q_norm_g[a])) * jnp.max(jnp.abs(k_norm_g[a]))
            bound = bound.astype(F32)
            lane_const = (jnp.arange(LANE) == CONST_LANE).astype(F32)
            pads = jnp.stack([-bound * lane_const, lane_const, lane_const] + [0.0 * lane_const] * (SUBLANE - 3))
            q, k, vt = _att_in(stream, mod, g1, _att_weights(att_w_in[a]), pad_gain(q_norm_g[a]),
                               pad_gain(k_norm_g[a]), cos_t, sin_t, bd, pads, n_xt)
            y = lax.cond(bound <= MAX_BOUND_LOG2,
                         lambda: _flash(q, k, vt, nb, t, lc, bounded=True),
                         lambda: _flash(q, k, vt, nb, t, lc, bounded=False))
            wo = att_w_out[a].astype(BF16)
        stream = _outproj_mlp(y, stream, mod, g2, wo, mlp_w1[layer].astype(BF16), mlp_w2[layer].astype(BF16),
                              nb * n_xt if last else n_tiles, n_xt)
    return stream[:nb * t].reshape(nb, t, d).astype(x.dtype)
```

```python
import functools

import jax
import jax.numpy as jnp
from jax import lax
from jax.experimental import pallas as pl
from jax.experimental.pallas import tpu as pltpu

F32 = jnp.float32
BF16 = jnp.bfloat16

D_MODEL = 1024
DEPTH = 4
GRID_W = 64
EPS = 1e-6
ML_HEADS = 4
ML_DK = 64
ML_DV = 128
ML_CHUNK = 64
SC_WIDTH = D_MODEL // 2
CONV_K = 3
ATT_HEADS = 16
KV_HEADS = 4
HEAD_DIM = 64
GROUP = ATT_HEADS // KV_HEADS
ROPE_THETA = 10000.0
MLP_HIDDEN = 4 * D_MODEL

LANE = 128
SUBLANE = 8
TOKEN_TILE = 512
SEQ_BLOCK = 256
SCAN_CHUNK = SEQ_BLOCK
KV_TILE = TOKEN_TILE
VMEM_LIMIT_BYTES = 56 * 1024 * 1024

HYB_COLS = (4 * LANE, 4 * LANE, 4 * ML_DV, 4 * ML_DV, LANE, SC_WIDTH, SC_WIDTH, SC_WIDTH)
ATT_Q_COLS = ATT_HEADS * LANE
ATT_K_COLS = KV_HEADS * LANE


def _cparams(*sem):
    return pltpu.CompilerParams(dimension_semantics=sem, vmem_limit_bytes=VMEM_LIMIT_BYTES)


def _normmod(xf, g, shift, scale):
    var = jnp.mean(xf * xf, axis=-1, keepdims=True)
    return xf * lax.rsqrt(var + EPS) * g * (1.0 + scale) + shift


def _dot(a, b):
    return jnp.dot(a, b, preferred_element_type=F32)


def _dot_nt(a, b):
    return lax.dot_general(a, b, (((1,), (1,)), ((), ())), preferred_element_type=F32)


def _adaln_body(c_ref, w_ref, b_ref, o_ref):
    cv = c_ref[...]
    s = cv * jax.nn.sigmoid(cv)
    o_ref[0] = _dot(s.astype(BF16), w_ref[0].astype(BF16)) + b_ref[0]


def _adaln(cvec, ada_w, ada_b):
    depth, d, n = ada_w.shape
    bn = n // 4
    return pl.pallas_call(
        _adaln_body,
        grid=(depth, n // bn),
        in_specs=[pl.BlockSpec((SUBLANE, d), lambda l, j: (0, 0)),
                  pl.BlockSpec((1, d, bn), lambda l, j: (l, 0, j)),
                  pl.BlockSpec((1, 1, bn), lambda l, j: (l, 0, j))],
        out_specs=pl.BlockSpec((1, SUBLANE, bn), lambda l, j: (l, 0, j)),
        out_shape=jax.ShapeDtypeStruct((depth, SUBLANE, n), F32),
        compiler_params=_cparams("arbitrary", "arbitrary"),
        name="adaln",
    )(cvec, ada_w, ada_b.reshape(depth, 1, n))


def _hyb_in_body(x_ref, mod_ref, g_ref, w_ref, gb_ref, q_ref, k_ref, v_ref, o_ref, gate_ref, cb_ref, z_ref):
    d = D_MODEL
    h = _normmod(x_ref[...], g_ref[...], mod_ref[0, :, 0:d], mod_ref[0, :, d:2 * d]).astype(BF16)
    c0 = 0
    q_ref[...] = (_dot(h, w_ref[:, c0:c0 + 512]) * (ML_DK ** -0.5)).astype(BF16)
    c0 += 512
    k_ref[...] = _dot(h, w_ref[:, c0:c0 + 512]).astype(BF16)
    c0 += 512
    v_ref[...] = _dot(h, w_ref[:, c0:c0 + 512])
    c0 += 512
    o_ref[...] = _dot(h, w_ref[:, c0:c0 + 512])
    c0 += 512
    g = _dot(h, w_ref[:, c0:c0 + LANE]) + gb_ref[...]
    c0 += LANE
    lane = lax.broadcasted_iota(jnp.int32, g.shape, 1)
    is_forget = ((lane // ML_HEADS) % 2) == 1
    logsig = jnp.minimum(g, 0.0) - jnp.log1p(jnp.exp(-jnp.abs(g)))
    gate_ref[...] = jnp.where(is_forget, logsig, g)
    cb_ref[...] = _dot(h, w_ref[:, c0:c0 + 512])
    c0 += 512
    gc = _dot(h, w_ref[:, c0:c0 + 512])
    c0 += 512
    z_ref[...] = gc * _dot(h, w_ref[:, c0:c0 + 512])


def _hyb_in(stream, mod, g1, w, gate_b, n_xt):
    rows, d = stream.shape
    tm = TOKEN_TILE
    ncol = w.shape[1]
    row = lambda i: (i, 0)
    const = lambda i: (0, 0)
    outs = [(512, BF16), (512, BF16), (512, F32), (512, F32), (LANE, F32), (512, F32), (512, F32)]
    return pl.pallas_call(
        _hyb_in_body,
        grid=(rows // tm,),
        in_specs=[pl.BlockSpec((tm, d), row),
                  pl.BlockSpec((1, 1, 6 * d), lambda i: (i // n_xt, 0, 0)),
                  pl.BlockSpec((1, d), const),
                  pl.BlockSpec((d, ncol), const),
                  pl.BlockSpec((1, LANE), const)],
        out_specs=[pl.BlockSpec((tm, c), row) for c, _ in outs],
        out_shape=[jax.ShapeDtypeStruct((rows, c), t) for c, t in outs],
        compiler_params=_cparams("arbitrary"),
        name="hyb_in",
    )(stream, mod, g1, w, gate_b)


def _split3(a):
    hi = a.astype(BF16)
    r = a - hi.astype(F32)
    mid = r.astype(BF16)
    lo = (r - mid.astype(F32)).astype(BF16)
    return hi, mid, lo


def _mlstm_body(q_ref, k_ref, v_ref, gate_ref, h_ref, c_ref, m_ref, *, rev):
    L = SCAN_CHUNK

    @pl.when(pl.program_id(1) == 0)
    def _():
        c_ref[...] = jnp.zeros_like(c_ref)
        m_ref[...] = jnp.zeros_like(m_ref)

    jj = lax.broadcasted_iota(jnp.int32, (L, L), 0)
    ss = lax.broadcasted_iota(jnp.int32, (L, L), 1)
    lower = ss <= jj
    upper = ss >= jj
    mask = upper if rev else lower
    tc = mask.astype(BF16)
    tr = (lower if rev else upper).astype(BF16)
    ones_v = jnp.ones((L, ML_DV), F32)
    n_chunks = q_ref.shape[0] // L
    order = range(n_chunks - 1, -1, -1) if rev else range(n_chunks)
    lane0 = 2 * ML_HEADS if rev else 0

    for c in order:
        r0 = c * L
        G = gate_ref[r0:r0 + L, :]
        GT = G.T
        bc = sum(_dot(tc, p) for p in _split3(G))
        br = sum(_dot(p, tr) for p in _split3(GT))
        ig_sh = pltpu.roll(G, ML_HEADS, axis=1)
        b_end = bc[0:1, :] if rev else bc[L - 1:L, :]
        m0 = m_ref[0:1, :]
        logw = b_end - bc + ig_sh
        m_loc = jnp.max(logw, axis=0, keepdims=True)
        w_all = jnp.exp(logw - m_loc)
        m_new = jnp.maximum(b_end + m0, m_loc)
        a2_all = jnp.exp(b_end + m0 - m_new)
        g2_all = jnp.exp(m_loc - m_new)
        m_inter_all = bc + m0
        m_ref[0:1, :] = m_new

        for h in range(ML_HEADS):
            il = lane0 + h
            fl = il + ML_HEADS
            cs = slice(h * LANE, (h + 1) * LANE)
            qh = q_ref[r0:r0 + L, cs]
            kh = k_ref[r0:r0 + L, cs]
            vh = v_ref[r0:r0 + L, cs]
            bcol = bc[:, fl:fl + 1]
            brow = br[fl:fl + 1, :]
            igrow = GT[il:il + 1, :]
            logd = jnp.where(mask, bcol - brow + igrow, -jnp.inf)
            m_inter = m_inter_all[:, fl:fl + 1]
            m = jnp.maximum(m_inter, jnp.max(logd, axis=1, keepdims=True))
            dmat = jnp.exp(logd - m)
            a = jnp.exp(m_inter - m)
            s = _dot_nt(qh, kh) * dmat
            vaug = jnp.concatenate([vh, ones_v], axis=1)
            c_old = c_ref[h]
            num_aug = _dot(s.astype(BF16), vaug.astype(BF16)) + a * _dot(qh, c_old.astype(BF16))
            num = num_aug[:, :ML_DV]
            den = num_aug[:, ML_DV:]
            h_ref[r0:r0 + L, cs] = num / jnp.maximum(jnp.abs(den), jnp.exp(-m))
            w = w_all[:, fl:fl + 1]
            wv = (w * vaug).astype(BF16)
            khT = kh.astype(F32).T.astype(BF16)
            c_loc = _dot(khT, wv)
            c_ref[h] = a2_all[:, fl:fl + 1] * c_old + g2_all[:, fl:fl + 1] * c_loc


def _mlstm(q, k, v, gate, nb, t, rev):
    rows = q.shape[0]
    blk = SEQ_BLOCK
    nxb = t // blk
    ctx_blk0 = nb * nxb

    def imap(b, i):
        xi = (nxb - i) if rev else (i - 1)
        return (jnp.where(i == 0, ctx_blk0 + b, b * nxb + xi), 0)

    spec = lambda c: pl.BlockSpec((blk, c), imap)
    return pl.pallas_call(
        functools.partial(_mlstm_body, rev=rev),
        grid=(nb, nxb + 1),
        in_specs=[spec(512), spec(512), spec(512), spec(LANE)],
        out_specs=spec(512),
        out_shape=jax.ShapeDtypeStruct((rows, 512), F32),
        scratch_shapes=[pltpu.VMEM((ML_HEADS, LANE, 2 * ML_DV), F32), pltpu.VMEM((SUBLANE, LANE), F32)],
        compiler_params=_cparams("arbitrary", "arbitrary"),
        name="mlstm_bwd" if rev else "mlstm_fwd",
    )(q, k, v, gate)


def _hyb_comb_body(hf_ref, hb_ref, o_ref, cb_ref, z_ref, zp_ref, zn_ref, mg_ref, cw_ref, out_ref, *, starts, ends):
    tm = hf_ref.shape[0]
    i = pl.program_id(0)
    for h in range(ML_HEADS):
        cs = slice(h * ML_DV, (h + 1) * ML_DV)
        blk = hf_ref[:, cs] + hb_ref[:, cs]
        var = jnp.mean(blk * blk, axis=-1, keepdims=True)
        hn = blk * lax.rsqrt(var + EPS) * mg_ref[:, cs]
        out_ref[:, cs] = (hn * jax.nn.sigmoid(o_ref[:, cs])).astype(BF16)
    z = z_ref[...]
    loc = lax.broadcasted_iota(jnp.int32, (tm, 1), 0)
    row = loc + i * tm
    is_start = functools.reduce(jnp.logical_or, [row == r for r in starts])
    is_end = functools.reduce(jnp.logical_or, [row == r for r in ends])
    zprev = jnp.where(loc == 0, zp_ref[SUBLANE - 1:SUBLANE, :], pltpu.roll(z, 1, axis=0))
    zprev = jnp.where(is_start, 0.0, zprev)
    znext = jnp.where(loc == tm - 1, zn_ref[0:1, :], pltpu.roll(z, tm - 1, axis=0))
    znext = jnp.where(is_end, 0.0, znext)
    conv = cw_ref[0:1, :] * zprev + cw_ref[1:2, :] * z + cw_ref[2:3, :] * znext
    out_ref[:, 4 * ML_DV:] = (cb_ref[...] * conv).astype(BF16)


def _hyb_combine(hf, hb, o, cb, z, mg, cw, starts, ends):
    rows = hf.shape[0]
    tm = TOKEN_TILE
    per = tm // SUBLANE
    nblk = rows // SUBLANE
    row = lambda i: (i, 0)
    const = lambda i: (0, 0)
    return pl.pallas_call(
        functools.partial(_hyb_comb_body, starts=starts, ends=ends),
        grid=(rows // tm,),
        in_specs=[pl.BlockSpec((tm, 512), row)] * 5 + [
            pl.BlockSpec((SUBLANE, 512), lambda i: (jnp.maximum(i * per - 1, 0), 0)),
            pl.BlockSpec((SUBLANE, 512), lambda i: (jnp.minimum((i + 1) * per, nblk - 1), 0)),
            pl.BlockSpec((1, 512), const),
            pl.BlockSpec((SUBLANE, 512), const)],
        out_specs=pl.BlockSpec((tm, D_MODEL), row),
        out_shape=jax.ShapeDtypeStruct((rows, D_MODEL), BF16),
        compiler_params=_cparams("arbitrary"),
        name="hyb_combine",
    )(hf, hb, o, cb, z, z, z, mg, cw)


CONST_LANE = HEAD_DIM
LOG2E = 1.4426950408889634
BOUND_MARGIN = 1.02
MAX_BOUND_LOG2 = 48.0


def _att_in_body(x_ref, mod_ref, g_ref, w_ref, qg_ref, kg_ref, cos_ref, sin_ref, bd_ref, pads_ref,
                 q_ref, k_ref, vt_ref):
    d = D_MODEL
    h = _normmod(x_ref[...], g_ref[...], mod_ref[0, :, 0:d], mod_ref[0, :, d:2 * d]).astype(BF16)
    cosv = cos_ref[...]
    sinv = sin_ref[...]
    bd = bd_ref[...]
    lane = lax.broadcasted_iota(jnp.int32, cosv.shape, 1)
    upper_half = ((lane // (HEAD_DIM // 4)) % 2) == 1
    first_head = lane < HEAD_DIM

    def norm_rope(p, gain, scale, pad, out_ref, c_out):
        ssq = _dot((p * p).astype(BF16), bd)
        pn = p * (lax.rsqrt(ssq * (1.0 / HEAD_DIM) + EPS) * scale) * gain
        for half in range(2):
            xh = pn[:, half * LANE:(half + 1) * LANE]
            partner = jnp.where(upper_half, pltpu.roll(xh, HEAD_DIM // 4, axis=1),
                                pltpu.roll(xh, LANE - HEAD_DIM // 4, axis=1))
            y = xh * cosv + partner * sinv
            c0 = c_out + 2 * half * LANE
            out_ref[:, c0:c0 + LANE] = jnp.where(first_head, y, pad).astype(BF16)
            out_ref[:, c0 + LANE:c0 + 2 * LANE] = jnp.where(first_head, pltpu.roll(y, HEAD_DIM, axis=1), pad).astype(BF16)

    nq, nk = ATT_HEADS * HEAD_DIM, KV_HEADS * HEAD_DIM
    q_scale = HEAD_DIM ** -0.5 * LOG2E
    for j in range(nq // 256):
        p = _dot(h, w_ref[:, j * 256:(j + 1) * 256])
        norm_rope(p, qg_ref[...], q_scale, pads_ref[0:1, :], q_ref, 2 * j * 256)
    norm_rope(_dot(h, w_ref[:, nq:nq + nk]), kg_ref[...], 1.0, pads_ref[1:2, :], k_ref, 0)
    vt = _dot(h, w_ref[:, nq + nk:nq + 2 * nk]).T.astype(BF16)
    tail_row = lax.broadcasted_iota(jnp.int32, (LANE - HEAD_DIM, vt.shape[1]), 0)
    tail = jnp.where(tail_row == 0, 1.0, 0.0).astype(BF16)
    for hh in range(KV_HEADS):
        r0 = hh * LANE
        vt_ref[0, r0:r0 + HEAD_DIM, :] = vt[hh * HEAD_DIM:(hh + 1) * HEAD_DIM, :]
        vt_ref[0, r0 + HEAD_DIM:r0 + LANE, :] = tail


def _att_in(stream, mod, g1, w, qg, kg, cos_t, sin_t, bd, pads, n_xt):
    rows, d = stream.shape
    tm = TOKEN_TILE
    nt = rows // tm
    ncol = w.shape[1]
    row = lambda i: (i, 0)
    const = lambda i: (0, 0)
    tab = lambda i: (jnp.where(i == nt - 1, n_xt, i % n_xt), 0)
    return pl.pallas_call(
        _att_in_body,
        grid=(nt,),
        in_specs=[pl.BlockSpec((tm, d), row),
                  pl.BlockSpec((1, 1, 6 * d), lambda i: (i // n_xt, 0, 0)),
                  pl.BlockSpec((1, d), const),
                  pl.BlockSpec((d, ncol), const),
                  pl.BlockSpec((1, 256), const),
                  pl.BlockSpec((1, 256), const),
                  pl.BlockSpec((tm, LANE), tab),
                  pl.BlockSpec((tm, LANE), tab),
                  pl.BlockSpec((256, 256), const),
                  pl.BlockSpec((SUBLANE, LANE), const)],
        out_specs=[pl.BlockSpec((tm, ATT_Q_COLS), row),
                   pl.BlockSpec((tm, ATT_K_COLS), row),
                   pl.BlockSpec((1, ATT_K_COLS, tm), lambda i: (i, 0, 0))],
        out_shape=[jax.ShapeDtypeStruct((rows, ATT_Q_COLS), BF16),
                   jax.ShapeDtypeStruct((rows, ATT_K_COLS), BF16),
                   jax.ShapeDtypeStruct((nt, ATT_K_COLS, tm), BF16)],
        compiler_params=_cparams("arbitrary"),
        name="att_in",
    )(stream, mod, g1, w, qg, kg, cos_t, sin_t, bd, pads)


def _store_heads(o_ref, accs):
    outs = [a[:HEAD_DIM, :] * (1.0 / a[CONST_LANE:CONST_LANE + 1, :]) for a in accs]
    for pair in range(GROUP // 2):
        both = jnp.concatenate(outs[2 * pair:2 * pair + 2], axis=0)
        o_ref[:, pair * LANE:(pair + 1) * LANE] = both.T.astype(BF16)


def _flash_safe_body(q_ref, kc_ref, vtc_ref, kx_ref, vtx_ref, o_ref):
    is_latent = pl.program_id(2) < pl.num_programs(2) - 1
    tq = q_ref.shape[0]
    qs = [q_ref[:, g * LANE:(g + 1) * LANE] for g in range(GROUP)]

    def step(kt, vt, carry):
        new = []
        for g in range(GROUP):
            m, acc = carry[g]
            st = _dot_nt(kt, qs[g])
            m_new = jnp.maximum(m, jnp.max(st, axis=0, keepdims=True))
            p = jnp.exp2(st - m_new)
            acc_new = jnp.exp2(m - m_new) * acc + _dot(vt, p.astype(BF16))
            new.append((m_new, acc_new))
        return tuple(new)

    init = tuple((jnp.full((1, tq), -jnp.inf, F32), jnp.zeros((LANE, tq), F32)) for _ in range(GROUP))
    carry = step(kc_ref[...], vtc_ref[0], init)
    carry = lax.fori_loop(
        0, jnp.where(is_latent, vtx_ref.shape[0], 0),
        lambda j, cr: step(kx_ref[pl.ds(pl.multiple_of(j * KV_TILE, KV_TILE), KV_TILE), :], vtx_ref[j], cr),
        carry)
    _store_heads(o_ref, [acc for _, acc in carry])


def _flash_bounded_body(q_ref, kc_ref, vtc_ref, kx_ref, vtx_ref, o_ref, qa_ref, acc_ref):
    tq = q_ref.shape[0]
    for g in range(GROUP):
        qa_ref[g * tq:(g + 1) * tq, :] = q_ref[:, g * LANE:(g + 1) * LANE]

    def chunk(kt, vt):
        st = _dot_nt(kt, qa_ref[...])
        return _dot(vt, jnp.exp2(st).astype(BF16))

    is_latent = pl.program_id(2) < pl.num_programs(2) - 1

    @pl.when(is_latent)
    def _():
        acc_ref[...] = chunk(kc_ref[...], vtc_ref[0])
        for j in range(vtx_ref.shape[0]):
            acc_ref[...] += chunk(kx_ref[j * KV_TILE:(j + 1) * KV_TILE, :], vtx_ref[j])

    @pl.when(jnp.logical_not(is_latent))
    def _():
        acc_ref[...] = chunk(kc_ref[...], vtc_ref[0])

    _store_heads(o_ref, [acc_ref[:, g * tq:(g + 1) * tq] for g in range(GROUP)])


def _flash(q, k, vt, nb, t, lc, bounded):
    rows = q.shape[0]
    tq = SEQ_BLOCK
    n_xt = t // KV_TILE
    nq = t // tq
    ctx_q0 = nb * nq
    ctx_tile = nb * t // KV_TILE
    qmap = lambda b, h, i: (jnp.where(i == nq, ctx_q0 + b, b * nq + i), h)
    in_specs = [pl.BlockSpec((tq, GROUP * LANE), qmap),
                pl.BlockSpec((lc, LANE), lambda b, h, i: (nb * t // lc + b, h)),
                pl.BlockSpec((1, LANE, lc), lambda b, h, i: (ctx_tile, h, b)),
                pl.BlockSpec((t, LANE), lambda b, h, i: (b, h)),
                pl.BlockSpec((n_xt, LANE, KV_TILE), lambda b, h, i: (b, h, 0))]
    if bounded:
        body = _flash_bounded_body
        scratch = [pltpu.VMEM((GROUP * tq, LANE), BF16), pltpu.VMEM((LANE, GROUP * tq), F32)]
    else:
        body = _flash_safe_body
        scratch = []
    return pl.pallas_call(
        body,
        grid=(nb, KV_HEADS, nq + 1),
        in_specs=in_specs,
        out_specs=pl.BlockSpec((tq, GROUP * HEAD_DIM), qmap),
        out_shape=jax.ShapeDtypeStruct((rows, ATT_HEADS * HEAD_DIM), BF16),
        scratch_shapes=scratch,
        compiler_params=_cparams("arbitrary", "arbitrary", "arbitrary"),
        name="flash_bounded" if bounded else "flash_safe",
    )(q, k, vt, k, vt)


def _outproj_mlp_body(y_ref, x_ref, mod_ref, g_ref, wo_ref, w1_ref, w2_ref, out_ref):
    d = D_MODEL
    mod = lambda k: mod_ref[0, :, k * d:(k + 1) * d]
    x1 = x_ref[...] + mod(2) * _dot(y_ref[...], wo_ref[...])
    h2 = _normmod(x1, g_ref[...], mod(3), mod(4)).astype(BF16)
    acc = jnp.zeros_like(x1)
    for j in range(MLP_HIDDEN // d):
        u = jnp.maximum(_dot(h2, w1_ref[:, j * d:(j + 1) * d]), 0.0)
        acc = acc + _dot((u * u).astype(BF16), w2_ref[j * d:(j + 1) * d, :])
    out_ref[...] = x1 + mod(5) * acc


def _outproj_mlp(y, stream, mod, g2, wo, w1, w2, n_tiles, n_xt):
    d = D_MODEL
    tm = TOKEN_TILE
    row = lambda i: (i, 0)
    const = lambda i: (0, 0)
    return pl.pallas_call(
        _outproj_mlp_body,
        grid=(n_tiles,),
        in_specs=[pl.BlockSpec((tm, d), row),
                  pl.BlockSpec((tm, d), row),
                  pl.BlockSpec((1, 1, 6 * d), lambda i: (i // n_xt, 0, 0)),
                  pl.BlockSpec((1, d), const),
                  pl.BlockSpec((d, d), const),
                  pl.BlockSpec((d, MLP_HIDDEN), const),
                  pl.BlockSpec((MLP_HIDDEN, d), const)],
        out_specs=pl.BlockSpec((tm, d), row),
        out_shape=jax.ShapeDtypeStruct((n_tiles * tm, d), F32),
        compiler_params=_cparams("arbitrary"),
        name="outproj_mlp",
    )(y, stream, mod, g2, wo, w1, w2)


def _pad_heads(w, n_heads, width):
    d = w.shape[0]
    w = w.reshape(d, n_heads, width)
    return jnp.pad(w, ((0, 0), (0, 0), (0, LANE - width))).reshape(d, n_heads * LANE)


def _hyb_weights(w_in, gate_b):
    sizes = (ML_HEADS * ML_DK, ML_HEADS * ML_DK, ML_HEADS * ML_DV, ML_HEADS * ML_DV, 4 * ML_HEADS,
             SC_WIDTH, SC_WIDTH, SC_WIDTH)
    parts, c0 = [], 0
    for s in sizes:
        parts.append(w_in[:, c0:c0 + s])
        c0 += s
    wq, wk, wv, wo, wg, wb, wc, wu = parts
    wg = jnp.pad(wg, ((0, 0), (0, LANE - wg.shape[1])))
    w = jnp.concatenate([_pad_heads(wq, ML_HEADS, ML_DK), _pad_heads(wk, ML_HEADS, ML_DK), wv, wo, wg, wb, wc, wu],
                        axis=1).astype(BF16)
    gb = jnp.pad(gate_b.astype(F32), (0, LANE - gate_b.shape[0])).reshape(1, LANE)
    return w, gb


def _rope_tables(t, n_ident):
    half = HEAD_DIM // 2
    inv = ROPE_THETA ** (-jnp.arange(0, half, 2, dtype=F32) / half)
    n_rows = t // GRID_W
    r = jnp.arange(n_rows, dtype=F32)[:, None] * inv
    c = jnp.arange(GRID_W, dtype=F32)[:, None] * inv
    grid = lambda a: jnp.broadcast_to(a[:, None, :], (n_rows, GRID_W, a.shape[1]))
    across = lambda a: jnp.broadcast_to(a[None, :, :], (n_rows, GRID_W, a.shape[1]))
    cr, sr, cc, sc = grid(jnp.cos(r)), grid(jnp.sin(r)), across(jnp.cos(c)), across(jnp.sin(c))
    cos_t = jnp.concatenate([cr, cr, cc, cc] * 2, axis=-1).reshape(t, LANE)
    sin_t = jnp.concatenate([-sr, sr, -sc, sc] * 2, axis=-1).reshape(t, LANE)
    cos_t = jnp.concatenate([cos_t, jnp.ones((n_ident, LANE), F32)], axis=0)
    sin_t = jnp.concatenate([sin_t, jnp.zeros((n_ident, LANE), F32)], axis=0)
    return cos_t, sin_t


def kernel(x, c, ctx, c_ctx, ada_w, ada_b, norm1_g, norm2_g, mlp_w1, mlp_w2, hyb_w_in, hyb_gate_b, mlstm_norm_g,
           conv_w, hyb_w_out, att_w_in, q_norm_g, k_norm_g, att_w_out):
    nb, t, d = x.shape
    lc = ctx.shape[1]
    assert d == D_MODEL and nb * lc == TOKEN_TILE and lc == SEQ_BLOCK and t % TOKEN_TILE == 0
    assert nb + 1 <= SUBLANE
    n_xt = t // TOKEN_TILE
    n_tiles = nb * n_xt + 1

    stream = jnp.concatenate([x.reshape(nb * t, d), ctx.reshape(nb * lc, d)], axis=0).astype(F32)
    cvec = jnp.concatenate([c, c_ctx[None, :], jnp.zeros((SUBLANE - nb - 1, d), c.dtype)], axis=0).astype(F32)
    mods = _adaln(cvec, ada_w.astype(F32), ada_b.astype(F32))

    cos_t, sin_t = _rope_tables(t, TOKEN_TILE)
    head_of = lambda axis: lax.broadcasted_iota(jnp.int32, (256, 256), axis) // HEAD_DIM
    bd = (head_of(0) == head_of(1)).astype(BF16)
    pad_gain = lambda g: jnp.tile(g.astype(F32), 256 // HEAD_DIM).reshape(1, 256)
    seq_starts = tuple(b * t for b in range(nb)) + tuple(nb * t + b * lc for b in range(nb))
    seq_ends = tuple((b + 1) * t - 1 for b in range(nb)) + tuple(nb * t + (b + 1) * lc - 1 for b in range(nb))

    for layer in range(DEPTH):
        last = layer == DEPTH - 1
        mod = mods[layer, :nb + 1].reshape(nb + 1, 1, 6 * d)
        g1 = norm1_g[layer].astype(F32).reshape(1, d)
        g2 = norm2_g[layer].astype(F32).reshape(1, d)
        if layer % 2 == 0:
            e = layer // 2
            w, gb = _hyb_weights(hyb_w_in[e], hyb_gate_b[e])
            q, k, v, o, gate, cb, z = _hyb_in(stream, mod, g1, w, gb, n_xt)
            hf = _mlstm(q, k, v, gate, nb, t, rev=False)
            hb = _mlstm(q, k, v, gate, nb, t, rev=True)
            cw = jnp.pad(conv_w[e].astype(F32), ((0, SUBLANE - CONV_K), (0, 0)))
            y = _hyb_combine(hf, hb, o, cb, z, mlstm_norm_g[e].astype(F32).reshape(1, -1), cw, seq_starts, seq_ends)
            wo = hyb_w_out[e].astype(BF16)
        else:
            a = layer // 2
            bound = (HEAD_DIM ** 0.5 * LOG2E * BOUND_MARGIN) * jnp.max(jnp.abs(q_norm_g[a])) * jnp.max(jnp.abs(k_norm_g[a]))
            bound = bound.astype(F32)
            lane_const = (jnp.arange(LANE) == CONST_LANE).astype(F32)
            pads = jnp.stack([-bound * lane_const, lane_const] + [0.0 * lane_const] * (SUBLANE - 2))
            q, k, vt = _att_in(stream, mod, g1, att_w_in[a].astype(BF16), pad_gain(q_norm_g[a]),
                               pad_gain(k_norm_g[a]), cos_t, sin_t, bd, pads, n_xt)
            y = lax.cond(bound <= MAX_BOUND_LOG2,
                         lambda: _flash(q, k, vt, nb, t, lc, bounded=True),
                         lambda: _flash(q, k, vt, nb, t, lc, bounded=False))
            wo = att_w_out[a].astype(BF16)
        stream = _outproj_mlp(y, stream, mod, g2, wo, mlp_w1[layer].astype(BF16), mlp_w2[layer].astype(BF16),
                              nb * n_xt if last else n_tiles, n_xt)
    return stream[:nb * t].reshape(nb, t, d).astype(x.dtype)
```

```python
import functools

import jax
import jax.numpy as jnp
from jax import lax
from jax.experimental import pallas as pl
from jax.experimental.pallas import tpu as pltpu

F32 = jnp.float32
BF16 = jnp.bfloat16

D_MODEL = 1024
DEPTH = 4
GRID_W = 64
EPS = 1e-6
ML_HEADS = 4
ML_DK = 64
ML_DV = 128
ML_CHUNK = 64
SC_WIDTH = D_MODEL // 2
CONV_K = 3
ATT_HEADS = 16
KV_HEADS = 4
HEAD_DIM = 64
GROUP = ATT_HEADS // KV_HEADS
ROPE_THETA = 10000.0
MLP_HIDDEN = 4 * D_MODEL

LANE = 128
SUBLANE = 8
TOKEN_TILE = 512
SEQ_BLOCK = 256
SCAN_CHUNK = SEQ_BLOCK
KV_TILE = TOKEN_TILE
VMEM_LIMIT_BYTES = 56 * 1024 * 1024

HYB_COLS = (4 * LANE, 4 * LANE, 4 * ML_DV, 4 * ML_DV, LANE, SC_WIDTH, SC_WIDTH, SC_WIDTH)
ATT_Q_COLS = ATT_HEADS * LANE
ATT_K_COLS = KV_HEADS * LANE


def _cparams(*sem):
    return pltpu.CompilerParams(dimension_semantics=sem, vmem_limit_bytes=VMEM_LIMIT_BYTES)


def _normmod(xf, g, shift, scale):
    var = jnp.mean(xf * xf, axis=-1, keepdims=True)
    return xf * lax.rsqrt(var + EPS) * g * (1.0 + scale) + shift


def _src_specs(src, tm):
    d = src[0].shape[1]
    if len(src) == 1:
        return [pl.BlockSpec((tm, d), lambda i: (i, 0))]
    n_main = src[0].shape[0] // tm
    assert src[1].shape[0] == tm
    return [pl.BlockSpec((tm, d), lambda i: (jnp.minimum(i, n_main - 1), 0)), pl.BlockSpec((tm, d), lambda i: (0, 0))]


def _src_rows(src_refs):
    if len(src_refs) == 1:
        return src_refs[0][...]
    on_ctx_tile = pl.program_id(0) == pl.num_programs(0) - 1
    return jnp.where(on_ctx_tile, src_refs[1][...], src_refs[0][...])


def _dot(a, b):
    return jnp.dot(a, b, preferred_element_type=F32)


def _dot_nt(a, b):
    return lax.dot_general(a, b, (((1,), (1,)), ((), ())), preferred_element_type=F32)


def _adaln_body(c_ref, w_ref, b_ref, o_ref):
    cv = c_ref[...]
    s = cv * jax.nn.sigmoid(cv)
    o_ref[0] = _dot(s.astype(BF16), w_ref[0].astype(BF16)) + b_ref[0]


def _adaln(cvec, ada_w, ada_b):
    depth, d, n = ada_w.shape
    bn = n // 4
    return pl.pallas_call(
        _adaln_body,
        grid=(depth, n // bn),
        in_specs=[pl.BlockSpec((SUBLANE, d), lambda l, j: (0, 0)),
                  pl.BlockSpec((1, d, bn), lambda l, j: (l, 0, j)),
                  pl.BlockSpec((1, 1, bn), lambda l, j: (l, 0, j))],
        out_specs=pl.BlockSpec((1, SUBLANE, bn), lambda l, j: (l, 0, j)),
        out_shape=jax.ShapeDtypeStruct((depth, SUBLANE, n), F32),
        compiler_params=_cparams("arbitrary", "arbitrary"),
        name="adaln",
    )(cvec, ada_w, ada_b.reshape(depth, 1, n))


def _hyb_in_body(*refs, n_src):
    mod_ref, g_ref, w_ref, gb_ref, q_ref, k_ref, v_ref, o_ref, gate_ref, cb_ref, z_ref = refs[n_src:]
    d = D_MODEL
    h = _normmod(_src_rows(refs[:n_src]), g_ref[...], mod_ref[0, :, 0:d], mod_ref[0, :, d:2 * d]).astype(BF16)
    c0 = 0
    q_ref[...] = (_dot(h, w_ref[:, c0:c0 + 512]) * (ML_DK ** -0.5)).astype(BF16)
    c0 += 512
    k_ref[...] = _dot(h, w_ref[:, c0:c0 + 512]).astype(BF16)
    c0 += 512
    v_ref[...] = _dot(h, w_ref[:, c0:c0 + 512])
    c0 += 512
    o_ref[...] = _dot(h, w_ref[:, c0:c0 + 512])
    c0 += 512
    g = _dot(h, w_ref[:, c0:c0 + LANE]) + gb_ref[...]
    c0 += LANE
    lane = lax.broadcasted_iota(jnp.int32, g.shape, 1)
    is_forget = ((lane // ML_HEADS) % 2) == 1
    logsig = jnp.minimum(g, 0.0) - jnp.log1p(jnp.exp(-jnp.abs(g)))
    gate_ref[...] = jnp.where(is_forget, logsig, g)
    cb_ref[...] = _dot(h, w_ref[:, c0:c0 + 512])
    c0 += 512
    gc = _dot(h, w_ref[:, c0:c0 + 512])
    c0 += 512
    z_ref[...] = gc * _dot(h, w_ref[:, c0:c0 + 512])


def _hyb_in(src, mod, g1, w, gate_b, n_xt):
    rows = sum(s.shape[0] for s in src)
    d = D_MODEL
    tm = TOKEN_TILE
    ncol = w.shape[1]
    row = lambda i: (i, 0)
    const = lambda i: (0, 0)
    outs = [(512, BF16), (512, BF16), (512, F32), (512, F32), (LANE, F32), (512, F32), (512, F32)]
    return pl.pallas_call(
        functools.partial(_hyb_in_body, n_src=len(src)),
        grid=(rows // tm,),
        in_specs=_src_specs(src, tm) + [
                  pl.BlockSpec((1, 1, 6 * d), lambda i: (i // n_xt, 0, 0)),
                  pl.BlockSpec((1, d), const),
                  pl.BlockSpec((d, ncol), const),
                  pl.BlockSpec((1, LANE), const)],
        out_specs=[pl.BlockSpec((tm, c), row) for c, _ in outs],
        out_shape=[jax.ShapeDtypeStruct((rows, c), t) for c, t in outs],
        compiler_params=_cparams("arbitrary"),
        name="hyb_in",
    )(*src, mod, g1, w, gate_b)


def _split3(a):
    hi = a.astype(BF16)
    r = a - hi.astype(F32)
    mid = r.astype(BF16)
    lo = (r - mid.astype(F32)).astype(BF16)
    return hi, mid, lo


def _mlstm_body(q_ref, k_ref, v_ref, gate_ref, h_ref, c_ref, m_ref, *, rev):
    L = SCAN_CHUNK

    @pl.when(pl.program_id(1) == 0)
    def _():
        c_ref[...] = jnp.zeros_like(c_ref)
        m_ref[...] = jnp.zeros_like(m_ref)

    jj = lax.broadcasted_iota(jnp.int32, (L, L), 0)
    ss = lax.broadcasted_iota(jnp.int32, (L, L), 1)
    lower = ss <= jj
    upper = ss >= jj
    mask = upper if rev else lower
    tc = mask.astype(BF16)
    tr = (lower if rev else upper).astype(BF16)
    ones_v = jnp.ones((L, ML_DV), F32)
    n_chunks = q_ref.shape[0] // L
    order = range(n_chunks - 1, -1, -1) if rev else range(n_chunks)
    lane0 = 2 * ML_HEADS if rev else 0

    for c in order:
        r0 = c * L
        G = gate_ref[r0:r0 + L, :]
        GT = G.T
        bc = sum(_dot(tc, p) for p in _split3(G))
        br = sum(_dot(p, tr) for p in _split3(GT))
        ig_sh = pltpu.roll(G, ML_HEADS, axis=1)
        b_end = bc[0:1, :] if rev else bc[L - 1:L, :]
        m0 = m_ref[0:1, :]
        logw = b_end - bc + ig_sh
        m_loc = jnp.max(logw, axis=0, keepdims=True)
        w_all = jnp.exp(logw - m_loc)
        m_new = jnp.maximum(b_end + m0, m_loc)
        a2_all = jnp.exp(b_end + m0 - m_new)
        g2_all = jnp.exp(m_loc - m_new)
        m_inter_all = bc + m0
        m_ref[0:1, :] = m_new

        for h in range(ML_HEADS):
            il = lane0 + h
            fl = il + ML_HEADS
            cs = slice(h * LANE, (h + 1) * LANE)
            qh = q_ref[r0:r0 + L, cs]
            kh = k_ref[r0:r0 + L, cs]
            vh = v_ref[r0:r0 + L, cs]
            bcol = bc[:, fl:fl + 1]
            brow = br[fl:fl + 1, :]
            igrow = GT[il:il + 1, :]
            logd = jnp.where(mask, bcol - brow + igrow, -jnp.inf)
            m_inter = m_inter_all[:, fl:fl + 1]
            m = jnp.maximum(m_inter, jnp.max(logd, axis=1, keepdims=True))
            dmat = jnp.exp(logd - m)
            a = jnp.exp(m_inter - m)
            s = _dot_nt(qh, kh) * dmat
            vaug = jnp.concatenate([vh, ones_v], axis=1)
            c_old = c_ref[h]
            num_aug = _dot(s.astype(BF16), vaug.astype(BF16)) + a * _dot(qh, c_old.astype(BF16))
            num = num_aug[:, :ML_DV]
            den = num_aug[:, ML_DV:]
            h_ref[r0:r0 + L, cs] = num / jnp.maximum(jnp.abs(den), jnp.exp(-m))
            w = w_all[:, fl:fl + 1]
            wv = (w * vaug).astype(BF16)
            khT = kh.astype(F32).T.astype(BF16)
            c_loc = _dot(khT, wv)
            c_ref[h] = a2_all[:, fl:fl + 1] * c_old + g2_all[:, fl:fl + 1] * c_loc


def _mlstm(q, k, v, gate, nb, t, rev):
    rows = q.shape[0]
    blk = SEQ_BLOCK
    nxb = t // blk
    ctx_blk0 = nb * nxb

    def imap(b, i):
        xi = (nxb - i) if rev else (i - 1)
        return (jnp.where(i == 0, ctx_blk0 + b, b * nxb + xi), 0)

    spec = lambda c: pl.BlockSpec((blk, c), imap)
    return pl.pallas_call(
        functools.partial(_mlstm_body, rev=rev),
        grid=(nb, nxb + 1),
        in_specs=[spec(512), spec(512), spec(512), spec(LANE)],
        out_specs=spec(512),
        out_shape=jax.ShapeDtypeStruct((rows, 512), F32),
        scratch_shapes=[pltpu.VMEM((ML_HEADS, LANE, 2 * ML_DV), F32), pltpu.VMEM((SUBLANE, LANE), F32)],
        compiler_params=_cparams("arbitrary", "arbitrary"),
        name="mlstm_bwd" if rev else "mlstm_fwd",
    )(q, k, v, gate)


def _hyb_mix(hf_ref, hb_ref, o_ref, cb_ref, z_ref, zp_ref, zn_ref, mg_ref, cw_ref, starts, ends):
    tm = hf_ref.shape[0]
    i = pl.program_id(0)
    parts = []
    for h in range(ML_HEADS):
        cs = slice(h * ML_DV, (h + 1) * ML_DV)
        blk = hf_ref[:, cs] + hb_ref[:, cs]
        var = jnp.mean(blk * blk, axis=-1, keepdims=True)
        hn = blk * lax.rsqrt(var + EPS) * mg_ref[:, cs]
        parts.append((hn * jax.nn.sigmoid(o_ref[:, cs])).astype(BF16))
    z = z_ref[...]
    loc = lax.broadcasted_iota(jnp.int32, (tm, 1), 0)
    row = loc + i * tm
    is_start = functools.reduce(jnp.logical_or, [row == r for r in starts])
    is_end = functools.reduce(jnp.logical_or, [row == r for r in ends])
    zprev = jnp.where(loc == 0, zp_ref[SUBLANE - 1:SUBLANE, :], pltpu.roll(z, 1, axis=0))
    zprev = jnp.where(is_start, 0.0, zprev)
    znext = jnp.where(loc == tm - 1, zn_ref[0:1, :], pltpu.roll(z, tm - 1, axis=0))
    znext = jnp.where(is_end, 0.0, znext)
    conv = cw_ref[0:1, :] * zprev + cw_ref[1:2, :] * z + cw_ref[2:3, :] * znext
    parts.append((cb_ref[...] * conv).astype(BF16))
    return jnp.concatenate(parts, axis=1)


def _hyb_mix_specs(rows, tm):
    per = tm // SUBLANE
    nblk = rows // SUBLANE
    const = lambda i: (0, 0)
    return [pl.BlockSpec((tm, 512), lambda i: (i, 0))] * 5 + [
        pl.BlockSpec((SUBLANE, 512), lambda i: (jnp.maximum(i * per - 1, 0), 0)),
        pl.BlockSpec((SUBLANE, 512), lambda i: (jnp.minimum((i + 1) * per, nblk - 1), 0)),
        pl.BlockSpec((1, 512), const),
        pl.BlockSpec((SUBLANE, 512), const)]


CONST_LANE = HEAD_DIM
LOG2E = 1.4426950408889634
BOUND_MARGIN = 1.02
MAX_BOUND_LOG2 = 48.0


def _att_in_body(x_ref, mod_ref, g_ref, w_ref, qg_ref, kg_ref, cos_ref, sin_ref, bd_ref, pads_ref,
                 q_ref, k_ref, vt_ref):
    d = D_MODEL
    h = _normmod(x_ref[...], g_ref[...], mod_ref[0, :, 0:d], mod_ref[0, :, d:2 * d]).astype(BF16)
    cosv = cos_ref[...]
    sinv = sin_ref[...]
    bd = bd_ref[...]
    lane = lax.broadcasted_iota(jnp.int32, cosv.shape, 1)
    upper_half = ((lane // (HEAD_DIM // 4)) % 2) == 1
    first_head = lane < HEAD_DIM

    def norm_rope(p, gain, scale, pad, out_ref, c_out):
        ssq = _dot((p * p).astype(BF16), bd)
        pn = p * (lax.rsqrt(ssq * (1.0 / HEAD_DIM) + EPS) * scale) * gain
        for half in range(2):
            xh = pn[:, half * LANE:(half + 1) * LANE]
            partner = jnp.where(upper_half, pltpu.roll(xh, HEAD_DIM // 4, axis=1),
                                pltpu.roll(xh, LANE - HEAD_DIM // 4, axis=1))
            y = xh * cosv + partner * sinv
            c0 = c_out + 2 * half * LANE
            out_ref[:, c0:c0 + LANE] = jnp.where(first_head, y, pad).astype(BF16)
            out_ref[:, c0 + LANE:c0 + 2 * LANE] = jnp.where(first_head, pltpu.roll(y, HEAD_DIM, axis=1), pad).astype(BF16)

    nq, nk = ATT_HEADS * HEAD_DIM, KV_HEADS * HEAD_DIM
    q_scale = HEAD_DIM ** -0.5 * LOG2E
    for j in range(nq // 256):
        p = _dot(h, w_ref[:, j * 256:(j + 1) * 256])
        norm_rope(p, qg_ref[...], q_scale, pads_ref[0:1, :], q_ref, 2 * j * 256)
    norm_rope(_dot(h, w_ref[:, nq:nq + nk]), kg_ref[...], 1.0, pads_ref[1:2, :], k_ref, 0)
    vt = _dot(h, w_ref[:, nq + nk:nq + 2 * nk]).T.astype(BF16)
    tail_row = lax.broadcasted_iota(jnp.int32, (LANE - HEAD_DIM, vt.shape[1]), 0)
    tail = jnp.where(tail_row == 0, 1.0, 0.0).astype(BF16)
    for hh in range(KV_HEADS):
        r0 = hh * LANE
        vt_ref[0, r0:r0 + HEAD_DIM, :] = vt[hh * HEAD_DIM:(hh + 1) * HEAD_DIM, :]
        vt_ref[0, r0 + HEAD_DIM:r0 + LANE, :] = tail


def _att_in(stream, mod, g1, w, qg, kg, cos_t, sin_t, bd, pads, n_xt):
    rows, d = stream.shape
    tm = TOKEN_TILE
    nt = rows // tm
    ncol = w.shape[1]
    row = lambda i: (i, 0)
    const = lambda i: (0, 0)
    tab = lambda i: (jnp.where(i == nt - 1, n_xt, i % n_xt), 0)
    return pl.pallas_call(
        _att_in_body,
        grid=(nt,),
        in_specs=[pl.BlockSpec((tm, d), row),
                  pl.BlockSpec((1, 1, 6 * d), lambda i: (i // n_xt, 0, 0)),
                  pl.BlockSpec((1, d), const),
                  pl.BlockSpec((d, ncol), const),
                  pl.BlockSpec((1, 256), const),
                  pl.BlockSpec((1, 256), const),
                  pl.BlockSpec((tm, LANE), tab),
                  pl.BlockSpec((tm, LANE), tab),
                  pl.BlockSpec((256, 256), const),
                  pl.BlockSpec((SUBLANE, LANE), const)],
        out_specs=[pl.BlockSpec((tm, ATT_Q_COLS), row),
                   pl.BlockSpec((tm, ATT_K_COLS), row),
                   pl.BlockSpec((1, ATT_K_COLS, tm), lambda i: (i, 0, 0))],
        out_shape=[jax.ShapeDtypeStruct((rows, ATT_Q_COLS), BF16),
                   jax.ShapeDtypeStruct((rows, ATT_K_COLS), BF16),
                   jax.ShapeDtypeStruct((nt, ATT_K_COLS, tm), BF16)],
        compiler_params=_cparams("arbitrary"),
        name="att_in",
    )(stream, mod, g1, w, qg, kg, cos_t, sin_t, bd, pads)


def _store_heads(o_ref, accs):
    outs = [a[:HEAD_DIM, :] * (1.0 / a[CONST_LANE:CONST_LANE + 1, :]) for a in accs]
    for pair in range(GROUP // 2):
        both = jnp.concatenate(outs[2 * pair:2 * pair + 2], axis=0)
        o_ref[:, pair * LANE:(pair + 1) * LANE] = both.T.astype(BF16)


def _flash_safe_body(q_ref, kc_ref, vtc_ref, kx_ref, vtx_ref, o_ref):
    is_latent = pl.program_id(2) < pl.num_programs(2) - 1
    tq = q_ref.shape[0]
    qs = [q_ref[:, g * LANE:(g + 1) * LANE] for g in range(GROUP)]

    def step(kt, vt, carry):
        new = []
        for g in range(GROUP):
            m, acc = carry[g]
            st = _dot_nt(kt, qs[g])
            m_new = jnp.maximum(m, jnp.max(st, axis=0, keepdims=True))
            p = jnp.exp2(st - m_new)
            acc_new = jnp.exp2(m - m_new) * acc + _dot(vt, p.astype(BF16))
            new.append((m_new, acc_new))
        return tuple(new)

    init = tuple((jnp.full((1, tq), -jnp.inf, F32), jnp.zeros((LANE, tq), F32)) for _ in range(GROUP))
    carry = step(kc_ref[...], vtc_ref[0], init)
    carry = lax.fori_loop(
        0, jnp.where(is_latent, vtx_ref.shape[0], 0),
        lambda j, cr: step(kx_ref[pl.ds(pl.multiple_of(j * KV_TILE, KV_TILE), KV_TILE), :], vtx_ref[j], cr),
        carry)
    _store_heads(o_ref, [acc for _, acc in carry])


def _flash_bounded_body(q_ref, kc_ref, vtc_ref, kx_ref, vtx_ref, o_ref, qa_ref, acc_ref):
    tq = q_ref.shape[0]
    for g in range(GROUP):
        qa_ref[g * tq:(g + 1) * tq, :] = q_ref[:, g * LANE:(g + 1) * LANE]

    def chunk(kt, vt):
        st = _dot_nt(kt, qa_ref[...])
        return _dot(vt, jnp.exp2(st).astype(BF16))

    is_latent = pl.program_id(2) < pl.num_programs(2) - 1

    @pl.when(is_latent)
    def _():
        acc_ref[...] = chunk(kc_ref[...], vtc_ref[0])
        for j in range(vtx_ref.shape[0]):
            acc_ref[...] += chunk(kx_ref[j * KV_TILE:(j + 1) * KV_TILE, :], vtx_ref[j])

    @pl.when(jnp.logical_not(is_latent))
    def _():
        acc_ref[...] = chunk(kc_ref[...], vtc_ref[0])

    _store_heads(o_ref, [acc_ref[:, g * tq:(g + 1) * tq] for g in range(GROUP)])


def _flash(q, k, vt, nb, t, lc, bounded):
    rows = q.shape[0]
    tq = SEQ_BLOCK
    n_xt = t // KV_TILE
    nq = t // tq
    ctx_q0 = nb * nq
    ctx_tile = nb * t // KV_TILE
    qmap = lambda b, h, i: (jnp.where(i == nq, ctx_q0 + b, b * nq + i), h)
    in_specs = [pl.BlockSpec((tq, GROUP * LANE), qmap),
                pl.BlockSpec((lc, LANE), lambda b, h, i: (nb * t // lc + b, h)),
                pl.BlockSpec((1, LANE, lc), lambda b, h, i: (ctx_tile, h, b)),
                pl.BlockSpec((t, LANE), lambda b, h, i: (b, h)),
                pl.BlockSpec((n_xt, LANE, KV_TILE), lambda b, h, i: (b, h, 0))]
    if bounded:
        body = _flash_bounded_body
        scratch = [pltpu.VMEM((GROUP * tq, LANE), BF16), pltpu.VMEM((LANE, GROUP * tq), F32)]
    else:
        body = _flash_safe_body
        scratch = []
    return pl.pallas_call(
        body,
        grid=(nb, KV_HEADS, nq + 1),
        in_specs=in_specs,
        out_specs=pl.BlockSpec((tq, GROUP * HEAD_DIM), qmap),
        out_shape=jax.ShapeDtypeStruct((rows, ATT_HEADS * HEAD_DIM), BF16),
        scratch_shapes=scratch,
        compiler_params=_cparams("arbitrary", "arbitrary", "arbitrary"),
        name="flash_bounded" if bounded else "flash_safe",
    )(q, k, vt, k, vt)


def _outproj_mlp_body(*refs, n_src, seq_edges):
    mod_ref, g_ref, wo_ref, w1_ref, w2_ref, out_ref = refs[-6:]
    mix_refs = refs[n_src:-6]
    y = mix_refs[0][...] if seq_edges is None else _hyb_mix(*mix_refs, *seq_edges)
    d = D_MODEL
    mod = lambda k: mod_ref[0, :, k * d:(k + 1) * d]
    x1 = _src_rows(refs[:n_src]) + mod(2) * _dot(y, wo_ref[...])
    h2 = _normmod(x1, g_ref[...], mod(3), mod(4)).astype(BF16)
    acc = jnp.zeros_like(x1)
    for j in range(MLP_HIDDEN // d):
        u = jnp.maximum(_dot(h2, w1_ref[:, j * d:(j + 1) * d]), 0.0)
        acc = acc + _dot((u * u).astype(BF16), w2_ref[j * d:(j + 1) * d, :])
    out_ref[...] = x1 + mod(5) * acc


def _outproj_mlp(mix, src, mod, g2, wo, w1, w2, n_tiles, n_xt, seq_edges=None):
    d = D_MODEL
    tm = TOKEN_TILE
    row = lambda i: (i, 0)
    const = lambda i: (0, 0)
    mix_specs = [pl.BlockSpec((tm, d), row)] if seq_edges is None else _hyb_mix_specs(mix[0].shape[0], tm)
    return pl.pallas_call(
        functools.partial(_outproj_mlp_body, n_src=len(src), seq_edges=seq_edges),
        grid=(n_tiles,),
        in_specs=_src_specs(src, tm) + mix_specs + [
                  pl.BlockSpec((1, 1, 6 * d), lambda i: (i // n_xt, 0, 0)),
                  pl.BlockSpec((1, d), const),
                  pl.BlockSpec((d, d), const),
                  pl.BlockSpec((d, MLP_HIDDEN), const),
                  pl.BlockSpec((MLP_HIDDEN, d), const)],
        out_specs=pl.BlockSpec((tm, d), row),
        out_shape=jax.ShapeDtypeStruct((n_tiles * tm, d), F32),
        compiler_params=_cparams("arbitrary"),
        name="outproj_mlp",
    )(*src, *mix, mod, g2, wo, w1, w2)


def _pad_heads(w, n_heads, width):
    d = w.shape[0]
    w = w.reshape(d, n_heads, width)
    return jnp.pad(w, ((0, 0), (0, 0), (0, LANE - width))).reshape(d, n_heads * LANE)


def _hyb_weights(w_in, gate_b):
    sizes = (ML_HEADS * ML_DK, ML_HEADS * ML_DK, ML_HEADS * ML_DV, ML_HEADS * ML_DV, 4 * ML_HEADS,
             SC_WIDTH, SC_WIDTH, SC_WIDTH)
    parts, c0 = [], 0
    for s in sizes:
        parts.append(w_in[:, c0:c0 + s])
        c0 += s
    wq, wk, wv, wo, wg, wb, wc, wu = parts
    wg = jnp.pad(wg, ((0, 0), (0, LANE - wg.shape[1])))
    w = jnp.concatenate([_pad_heads(wq, ML_HEADS, ML_DK), _pad_heads(wk, ML_HEADS, ML_DK), wv, wo, wg, wb, wc, wu],
                        axis=1).astype(BF16)
    gb = jnp.pad(gate_b.astype(F32), (0, LANE - gate_b.shape[0])).reshape(1, LANE)
    return w, gb


def _rope_tables(t, n_ident):
    half = HEAD_DIM // 2
    inv = ROPE_THETA ** (-jnp.arange(0, half, 2, dtype=F32) / half)
    n_rows = t // GRID_W
    r = jnp.arange(n_rows, dtype=F32)[:, None] * inv
    c = jnp.arange(GRID_W, dtype=F32)[:, None] * inv
    z_r, z_c = jnp.zeros_like(r), jnp.zeros_like(c)
    lanes = lambda q0, q1, q2, q3: jnp.concatenate([q0, q1, q2, q3] * 2, axis=-1)
    outer = lambda by_row, by_col: (by_row[:, None, :] + by_col[None, :, :]).reshape(t, LANE)
    cos_t = outer(lanes(jnp.cos(r), jnp.cos(r), z_r, z_r), lanes(z_c, z_c, jnp.cos(c), jnp.cos(c)))
    sin_t = outer(lanes(-jnp.sin(r), jnp.sin(r), z_r, z_r), lanes(z_c, z_c, -jnp.sin(c), jnp.sin(c)))
    cos_t = jnp.concatenate([cos_t, jnp.ones((n_ident, LANE), F32)], axis=0)
    sin_t = jnp.concatenate([sin_t, jnp.zeros((n_ident, LANE), F32)], axis=0)
    return cos_t, sin_t


def kernel(x, c, ctx, c_ctx, ada_w, ada_b, norm1_g, norm2_g, mlp_w1, mlp_w2, hyb_w_in, hyb_gate_b, mlstm_norm_g,
           conv_w, hyb_w_out, att_w_in, q_norm_g, k_norm_g, att_w_out):
    nb, t, d = x.shape
    lc = ctx.shape[1]
    assert d == D_MODEL and nb * lc == TOKEN_TILE and lc == SEQ_BLOCK and t % TOKEN_TILE == 0
    assert nb + 1 <= SUBLANE
    n_xt = t // TOKEN_TILE
    n_tiles = nb * n_xt + 1

    src = (x.reshape(nb * t, d).astype(F32), ctx.reshape(nb * lc, d).astype(F32))
    cvec = jnp.concatenate([c, c_ctx[None, :], jnp.zeros((SUBLANE - nb - 1, d), c.dtype)], axis=0).astype(F32)
    mods = _adaln(cvec, ada_w.astype(F32), ada_b.astype(F32))

    cos_t, sin_t = _rope_tables(t, TOKEN_TILE)
    head_of = lambda axis: lax.broadcasted_iota(jnp.int32, (256, 256), axis) // HEAD_DIM
    bd = (head_of(0) == head_of(1)).astype(BF16)
    pad_gain = lambda g: jnp.tile(g.astype(F32), 256 // HEAD_DIM).reshape(1, 256)
    seq_starts = tuple(b * t for b in range(nb)) + tuple(nb * t + b * lc for b in range(nb))
    seq_ends = tuple((b + 1) * t - 1 for b in range(nb)) + tuple(nb * t + (b + 1) * lc - 1 for b in range(nb))

    for layer in range(DEPTH):
        last = layer == DEPTH - 1
        mod = mods[layer, :nb + 1].reshape(nb + 1, 1, 6 * d)
        g1 = norm1_g[layer].astype(F32).reshape(1, d)
        g2 = norm2_g[layer].astype(F32).reshape(1, d)
        if layer % 2 == 0:
            e = layer // 2
            w, gb = _hyb_weights(hyb_w_in[e], hyb_gate_b[e])
            q, k, v, o, gate, cb, z = _hyb_in(src, mod, g1, w, gb, n_xt)
            hf = _mlstm(q, k, v, gate, nb, t, rev=False)
            hb = _mlstm(q, k, v, gate, nb, t, rev=True)
            cw = jnp.pad(conv_w[e].astype(F32), ((0, SUBLANE - CONV_K), (0, 0)))
            mix = (hf, hb, o, cb, z, z, z, mlstm_norm_g[e].astype(F32).reshape(1, -1), cw)
            seq_edges = (seq_starts, seq_ends)
            wo = hyb_w_out[e].astype(BF16)
        else:
            a = layer // 2
            bound = (HEAD_DIM ** 0.5 * LOG2E * BOUND_MARGIN) * jnp.max(jnp.abs(q_norm_g[a])) * jnp.max(jnp.abs(k_norm_g[a]))
            bound = bound.astype(F32)
            lane_const = (jnp.arange(LANE) == CONST_LANE).astype(F32)
            pads = jnp.stack([-bound * lane_const, lane_const] + [0.0 * lane_const] * (SUBLANE - 2))
            q, k, vt = _att_in(src[0], mod, g1, att_w_in[a].astype(BF16), pad_gain(q_norm_g[a]),
                               pad_gain(k_norm_g[a]), cos_t, sin_t, bd, pads, n_xt)
            y = lax.cond(bound <= MAX_BOUND_LOG2,
                         lambda: _flash(q, k, vt, nb, t, lc, bounded=True),
                         lambda: _flash(q, k, vt, nb, t, lc, bounded=False))
            mix = (y,)
            seq_edges = None
            wo = att_w_out[a].astype(BF16)
        src = (_outproj_mlp(mix, src, mod, g2, wo, mlp_w1[layer].astype(BF16), mlp_w2[layer].astype(BF16),
                            nb * n_xt if last else n_tiles, n_xt, seq_edges),)
    return src[0].reshape(nb, t, d).astype(x.dtype)
```

```python
import functools

import jax
import jax.numpy as jnp
from jax import lax
from jax.experimental import pallas as pl
from jax.experimental.pallas import tpu as pltpu

F32 = jnp.float32
BF16 = jnp.bfloat16

D_MODEL = 1024
DEPTH = 4
GRID_W = 64
EPS = 1e-6
ML_HEADS = 4
ML_DK = 64
ML_DV = 128
ML_CHUNK = 64
SC_WIDTH = D_MODEL // 2
CONV_K = 3
ATT_HEADS = 16
KV_HEADS = 4
HEAD_DIM = 64
GROUP = ATT_HEADS // KV_HEADS
ROPE_THETA = 10000.0
MLP_HIDDEN = 4 * D_MODEL

LANE = 128
SUBLANE = 8
TOKEN_TILE = 512
SEQ_BLOCK = 256
SCAN_CHUNK = SEQ_BLOCK
KV_TILE = TOKEN_TILE
VMEM_LIMIT_BYTES = 56 * 1024 * 1024

HYB_COLS = (4 * LANE, 4 * LANE, 4 * ML_DV, 4 * ML_DV, LANE, SC_WIDTH, SC_WIDTH, SC_WIDTH)
ATT_Q_COLS = ATT_HEADS * LANE
ATT_K_COLS = KV_HEADS * LANE


def _cparams(*sem):
    return pltpu.CompilerParams(dimension_semantics=sem, vmem_limit_bytes=VMEM_LIMIT_BYTES)


def _normmod(xf, g, shift, scale):
    var = jnp.mean(xf * xf, axis=-1, keepdims=True)
    return xf * lax.rsqrt(var + EPS) * g * (1.0 + scale) + shift


def _src_specs(src, tm):
    d = src[0].shape[1]
    if len(src) == 1:
        return [pl.BlockSpec((tm, d), lambda i: (i, 0))]
    n_main = src[0].shape[0] // tm
    assert src[1].shape[0] == tm
    return [pl.BlockSpec((tm, d), lambda i: (jnp.minimum(i, n_main - 1), 0)), pl.BlockSpec((tm, d), lambda i: (0, 0))]


def _src_rows(src_refs):
    if len(src_refs) == 1:
        return src_refs[0][...]
    on_ctx_tile = pl.program_id(0) == pl.num_programs(0) - 1
    return jnp.where(on_ctx_tile, src_refs[1][...], src_refs[0][...])


def _dot(a, b):
    return jnp.dot(a, b, preferred_element_type=F32)


def _dot_nt(a, b):
    return lax.dot_general(a, b, (((1,), (1,)), ((), ())), preferred_element_type=F32)


def _adaln_body(c_ref, w_ref, b_ref, o_ref):
    cv = c_ref[...]
    s = cv * jax.nn.sigmoid(cv)
    o_ref[0] = _dot(s.astype(BF16), w_ref[0].astype(BF16)) + b_ref[0]


def _adaln(cvec, ada_w, ada_b):
    depth, d, n = ada_w.shape
    bn = n // 4
    return pl.pallas_call(
        _adaln_body,
        grid=(depth, n // bn),
        in_specs=[pl.BlockSpec((SUBLANE, d), lambda l, j: (0, 0)),
                  pl.BlockSpec((1, d, bn), lambda l, j: (l, 0, j)),
                  pl.BlockSpec((1, 1, bn), lambda l, j: (l, 0, j))],
        out_specs=pl.BlockSpec((1, SUBLANE, bn), lambda l, j: (l, 0, j)),
        out_shape=jax.ShapeDtypeStruct((depth, SUBLANE, n), F32),
        compiler_params=_cparams("arbitrary", "arbitrary"),
        name="adaln",
    )(cvec, ada_w, ada_b.reshape(depth, 1, n))


def _hyb_in_body(*refs, n_src):
    mod_ref, g_ref, w_ref, gb_ref, q_ref, k_ref, vt_ref, o_ref, gate_ref, cb_ref, z_ref = refs[n_src:]
    d = D_MODEL
    h = _normmod(_src_rows(refs[:n_src]), g_ref[...], mod_ref[0, :, 0:d], mod_ref[0, :, d:2 * d]).astype(BF16)
    c0 = 0
    q_ref[...] = (_dot(h, w_ref[:, c0:c0 + 512]) * (ML_DK ** -0.5)).astype(BF16)
    c0 += 512
    k_ref[...] = _dot(h, w_ref[:, c0:c0 + 512]).astype(BF16)
    c0 += 512
    vt_ref[0] = _dot(h, w_ref[:, c0:c0 + 512]).T
    c0 += 512
    o_ref[...] = _dot(h, w_ref[:, c0:c0 + 512])
    c0 += 512
    g = _dot(h, w_ref[:, c0:c0 + LANE]) + gb_ref[...]
    c0 += LANE
    lane = lax.broadcasted_iota(jnp.int32, g.shape, 1)
    is_forget = ((lane // ML_HEADS) % 2) == 1
    logsig = jnp.minimum(g, 0.0) - jnp.log1p(jnp.exp(-jnp.abs(g)))
    gate_ref[...] = jnp.where(is_forget, logsig, g)
    cb_ref[...] = _dot(h, w_ref[:, c0:c0 + 512])
    c0 += 512
    gc = _dot(h, w_ref[:, c0:c0 + 512])
    c0 += 512
    z_ref[...] = gc * _dot(h, w_ref[:, c0:c0 + 512])


def _hyb_in(src, mod, g1, w, gate_b, n_xt):
    rows = sum(s.shape[0] for s in src)
    d = D_MODEL
    tm = TOKEN_TILE
    ncol = w.shape[1]
    row = lambda i: (i, 0)
    const = lambda i: (0, 0)
    outs = [(512, BF16), (512, BF16), None, (512, F32), (LANE, F32), (512, F32), (512, F32)]
    nt = rows // tm
    vt_cols = ML_HEADS * ML_DV
    return pl.pallas_call(
        functools.partial(_hyb_in_body, n_src=len(src)),
        grid=(nt,),
        in_specs=_src_specs(src, tm) + [
                  pl.BlockSpec((1, 1, 6 * d), lambda i: (i // n_xt, 0, 0)),
                  pl.BlockSpec((1, d), const),
                  pl.BlockSpec((d, ncol), const),
                  pl.BlockSpec((1, LANE), const)],
        out_specs=[pl.BlockSpec((tm, o[0]), row) if o else pl.BlockSpec((1, vt_cols, tm), lambda i: (i, 0, 0))
                   for o in outs],
        out_shape=[jax.ShapeDtypeStruct((rows, o[0]), o[1]) if o else jax.ShapeDtypeStruct((nt, vt_cols, tm), F32)
                   for o in outs],
        compiler_params=_cparams("arbitrary"),
        name="hyb_in",
    )(*src, mod, g1, w, gate_b)


def _split3(a):
    hi = a.astype(BF16)
    r = a - hi.astype(F32)
    mid = r.astype(BF16)
    lo = (r - mid.astype(F32)).astype(BF16)
    return hi, mid, lo


def _mlstm_body(q_ref, k_ref, vt_ref, gate_ref, h_ref, c_ref, m_ref, *, rev):
    L = SCAN_CHUNK

    @pl.when(pl.program_id(1) == 0)
    def _():
        c_ref[...] = jnp.zeros_like(c_ref)
        m_ref[...] = jnp.zeros_like(m_ref)

    s_idx = lax.broadcasted_iota(jnp.int32, (L, L), 0)
    j_idx = lax.broadcasted_iota(jnp.int32, (L, L), 1)
    visible = (s_idx >= j_idx) if rev else (s_idx <= j_idx)
    tr = visible.astype(BF16)
    tc = jnp.logical_not(visible).astype(BF16) + (s_idx == j_idx).astype(BF16)
    ones_rows = jnp.ones((ML_DV, L), F32)
    lane0 = 2 * ML_HEADS if rev else 0
    last = 0 if rev else L - 1

    G = gate_ref[...]
    GT = G.T
    bc = sum(_dot(tc, p) for p in _split3(G))
    br = sum(_dot(p, tr) for p in _split3(GT))
    ig_sh = pltpu.roll(G, ML_HEADS, axis=1)
    b_end = bc[last:last + 1, :]
    m0 = m_ref[0:1, :]
    src_col = ig_sh - bc
    m_loc = jnp.max(b_end + src_col, axis=0, keepdims=True)
    m_new = jnp.maximum(b_end + m0, m_loc)
    a2_all = jnp.exp(b_end + m0 - m_new)
    g2_all = jnp.exp(m_loc - m_new)
    m_ref[0:1, :] = m_new

    heads = range(ML_HEADS)
    fls = [lane0 + ML_HEADS + h for h in heads]
    cols = [slice(h * LANE, (h + 1) * LANE) for h in heads]
    qs = [q_ref[:, cs] for cs in cols]
    ks = [k_ref[:, cs] for cs in cols]
    vaugs = [jnp.concatenate([vt_ref[0, cs, :], ones_rows], axis=0) for cs in cols]
    brows = [br[fl:fl + 1, :] for fl in fls]
    c_olds = [c_ref[h] for h in heads]
    kqs = [_dot_nt(ks[h], qs[h]) for h in heads]
    inters = [_dot_nt(c_olds[h].astype(BF16), qs[h]) for h in heads]
    c_locs = []
    for h, fl in zip(heads, fls):
        src_row = GT[fl - ML_HEADS:fl - ML_HEADS + 1, :] - brows[h]
        w_row = jnp.exp(b_end[:, fl:fl + 1] + src_row - m_loc[:, fl:fl + 1])
        c_locs.append(_dot((vaugs[h] * w_row).astype(BF16), ks[h]))
    ms, m_inters, s_ts = [], [], []
    for h, fl in zip(heads, fls):
        logd = jnp.where(visible, brows[h] + src_col[:, fl:fl + 1], -jnp.inf)
        m_inter = brows[h] + m0[:, fl:fl + 1]
        m = jnp.maximum(m_inter, jnp.max(logd, axis=0, keepdims=True))
        s_ts.append((kqs[h] * jnp.exp(logd - m)).astype(BF16))
        ms.append(m)
        m_inters.append(m_inter)
    for h, fl in zip(heads, fls):
        num_t = _dot(vaugs[h].astype(BF16), s_ts[h]) + jnp.exp(m_inters[h] - ms[h]) * inters[h]
        h_t = num_t[:ML_DV, :] / jnp.maximum(jnp.abs(num_t[ML_DV:, :]), jnp.exp(-ms[h]))
        h_ref[:, cols[h]] = h_t.T
        c_ref[h] = a2_all[:, fl:fl + 1] * c_olds[h] + g2_all[:, fl:fl + 1] * c_locs[h]


def _mlstm(q, k, vt, gate, nb, t, rev):
    rows = q.shape[0]
    blk = SEQ_BLOCK
    per_tile = TOKEN_TILE // blk
    nxb = t // blk
    ctx_blk0 = nb * nxb

    def block(b, i):
        xi = (nxb - i) if rev else (i - 1)
        return jnp.where(i == 0, ctx_blk0 + b, b * nxb + xi)

    spec = lambda c: pl.BlockSpec((blk, c), lambda b, i: (block(b, i), 0))
    vt_spec = pl.BlockSpec((1, ML_HEADS * ML_DV, blk), lambda b, i: (block(b, i) // per_tile, 0, block(b, i) % per_tile))
    return pl.pallas_call(
        functools.partial(_mlstm_body, rev=rev),
        grid=(nb, nxb + 1),
        in_specs=[spec(512), spec(512), vt_spec, spec(LANE)],
        out_specs=spec(512),
        out_shape=jax.ShapeDtypeStruct((rows, 512), F32),
        scratch_shapes=[pltpu.VMEM((ML_HEADS, 2 * ML_DV, LANE), F32), pltpu.VMEM((SUBLANE, LANE), F32)],
        compiler_params=_cparams("arbitrary", "arbitrary"),
        name="mlstm_bwd" if rev else "mlstm_fwd",
    )(q, k, vt, gate)


def _hyb_mix(hf_ref, hb_ref, o_ref, cb_ref, z_ref, zp_ref, zn_ref, mg_ref, cw_ref, starts, ends):
    tm = hf_ref.shape[0]
    i = pl.program_id(0)
    parts = []
    for h in range(ML_HEADS):
        cs = slice(h * ML_DV, (h + 1) * ML_DV)
        blk = hf_ref[:, cs] + hb_ref[:, cs]
        var = jnp.mean(blk * blk, axis=-1, keepdims=True)
        hn = blk * lax.rsqrt(var + EPS) * mg_ref[:, cs]
        parts.append((hn * jax.nn.sigmoid(o_ref[:, cs])).astype(BF16))
    z = z_ref[...]
    loc = lax.broadcasted_iota(jnp.int32, (tm, 1), 0)
    row = loc + i * tm
    is_start = functools.reduce(jnp.logical_or, [row == r for r in starts])
    is_end = functools.reduce(jnp.logical_or, [row == r for r in ends])
    zprev = jnp.where(loc == 0, zp_ref[SUBLANE - 1:SUBLANE, :], pltpu.roll(z, 1, axis=0))
    zprev = jnp.where(is_start, 0.0, zprev)
    znext = jnp.where(loc == tm - 1, zn_ref[0:1, :], pltpu.roll(z, tm - 1, axis=0))
    znext = jnp.where(is_end, 0.0, znext)
    conv = cw_ref[0:1, :] * zprev + cw_ref[1:2, :] * z + cw_ref[2:3, :] * znext
    parts.append((cb_ref[...] * conv).astype(BF16))
    return jnp.concatenate(parts, axis=1)


def _hyb_mix_specs(rows, tm):
    per = tm // SUBLANE
    nblk = rows // SUBLANE
    const = lambda i: (0, 0)
    return [pl.BlockSpec((tm, 512), lambda i: (i, 0))] * 5 + [
        pl.BlockSpec((SUBLANE, 512), lambda i: (jnp.maximum(i * per - 1, 0), 0)),
        pl.BlockSpec((SUBLANE, 512), lambda i: (jnp.minimum((i + 1) * per, nblk - 1), 0)),
        pl.BlockSpec((1, 512), const),
        pl.BlockSpec((SUBLANE, 512), const)]


CONST_LANE = HEAD_DIM
LOG2E = 1.4426950408889634
BOUND_MARGIN = 1.02
MAX_BOUND_LOG2 = 48.0


def _att_in_body(x_ref, mod_ref, g_ref, w_ref, qg_ref, kg_ref, cos_ref, sin_ref, bd_ref, pads_ref,
                 q_ref, k_ref, vt_ref):
    d = D_MODEL
    h = _normmod(x_ref[...], g_ref[...], mod_ref[0, :, 0:d], mod_ref[0, :, d:2 * d]).astype(BF16)
    cosv = cos_ref[...]
    sinv = sin_ref[...]
    bd = bd_ref[...]
    lane = lax.broadcasted_iota(jnp.int32, cosv.shape, 1)
    upper_half = ((lane // (HEAD_DIM // 4)) % 2) == 1
    first_head = lane < HEAD_DIM

    def norm_rope(p, ssq, gain, scale, pad, out_ref, c_out):
        pn = p * (lax.rsqrt(ssq * (1.0 / HEAD_DIM) + EPS) * scale) * gain
        for half in range(2):
            xh = pn[:, half * LANE:(half + 1) * LANE]
            partner = jnp.where(upper_half, pltpu.roll(xh, HEAD_DIM // 4, axis=1),
                                pltpu.roll(xh, LANE - HEAD_DIM // 4, axis=1))
            y = xh * cosv + partner * sinv
            c0 = c_out + 2 * half * LANE
            out_ref[:, c0:c0 + LANE] = jnp.where(first_head, y, pad).astype(BF16)
            out_ref[:, c0 + LANE:c0 + 2 * LANE] = jnp.where(first_head, pltpu.roll(y, HEAD_DIM, axis=1), pad).astype(BF16)

    nq, nk = ATT_HEADS * HEAD_DIM, KV_HEADS * HEAD_DIM
    q_scale = HEAD_DIM ** -0.5 * LOG2E
    n_blk = (nq + nk) // 256
    ps = [_dot(h, w_ref[:, j * 256:(j + 1) * 256]) for j in range(n_blk)]
    v = _dot(h, w_ref[:, nq + nk:nq + 2 * nk])
    ssqs = [_dot((p * p).astype(BF16), bd) for p in ps]
    for j in range(nq // 256):
        norm_rope(ps[j], ssqs[j], qg_ref[...], q_scale, pads_ref[0:1, :], q_ref, 2 * j * 256)
    norm_rope(ps[-1], ssqs[-1], kg_ref[...], 1.0, pads_ref[1:2, :], k_ref, 0)
    vt = v.T.astype(BF16)
    tail_row = lax.broadcasted_iota(jnp.int32, (LANE - HEAD_DIM, vt.shape[1]), 0)
    tail = jnp.where(tail_row == 0, 1.0, 0.0).astype(BF16)
    for hh in range(KV_HEADS):
        r0 = hh * LANE
        vt_ref[0, r0:r0 + HEAD_DIM, :] = vt[hh * HEAD_DIM:(hh + 1) * HEAD_DIM, :]
        vt_ref[0, r0 + HEAD_DIM:r0 + LANE, :] = tail


def _att_in(stream, mod, g1, w, qg, kg, cos_t, sin_t, bd, pads, n_xt):
    rows, d = stream.shape
    tm = TOKEN_TILE
    nt = rows // tm
    ncol = w.shape[1]
    row = lambda i: (i, 0)
    const = lambda i: (0, 0)
    tab = lambda i: (jnp.where(i == nt - 1, n_xt, i % n_xt), 0)
    return pl.pallas_call(
        _att_in_body,
        grid=(nt,),
        in_specs=[pl.BlockSpec((tm, d), row),
                  pl.BlockSpec((1, 1, 6 * d), lambda i: (i // n_xt, 0, 0)),
                  pl.BlockSpec((1, d), const),
                  pl.BlockSpec((d, ncol), const),
                  pl.BlockSpec((1, 256), const),
                  pl.BlockSpec((1, 256), const),
                  pl.BlockSpec((tm, LANE), tab),
                  pl.BlockSpec((tm, LANE), tab),
                  pl.BlockSpec((256, 256), const),
                  pl.BlockSpec((SUBLANE, LANE), const)],
        out_specs=[pl.BlockSpec((tm, ATT_Q_COLS), row),
                   pl.BlockSpec((tm, ATT_K_COLS), row),
                   pl.BlockSpec((1, ATT_K_COLS, tm), lambda i: (i, 0, 0))],
        out_shape=[jax.ShapeDtypeStruct((rows, ATT_Q_COLS), BF16),
                   jax.ShapeDtypeStruct((rows, ATT_K_COLS), BF16),
                   jax.ShapeDtypeStruct((nt, ATT_K_COLS, tm), BF16)],
        compiler_params=_cparams("arbitrary"),
        name="att_in",
    )(stream, mod, g1, w, qg, kg, cos_t, sin_t, bd, pads)


def _store_heads(o_ref, accs):
    outs = [a[:HEAD_DIM, :] * (1.0 / a[CONST_LANE:CONST_LANE + 1, :]) for a in accs]
    for pair in range(GROUP // 2):
        both = jnp.concatenate(outs[2 * pair:2 * pair + 2], axis=0)
        o_ref[:, pair * LANE:(pair + 1) * LANE] = both.T.astype(BF16)


def _flash_safe_body(q_ref, kc_ref, vtc_ref, kx_ref, vtx_ref, o_ref):
    is_latent = pl.program_id(2) < pl.num_programs(2) - 1
    tq = q_ref.shape[0]
    qs = [q_ref[:, g * LANE:(g + 1) * LANE] for g in range(GROUP)]

    def step(kt, vt, carry):
        new = []
        for g in range(GROUP):
            m, acc = carry[g]
            st = _dot_nt(kt, qs[g])
            m_new = jnp.maximum(m, jnp.max(st, axis=0, keepdims=True))
            p = jnp.exp2(st - m_new)
            acc_new = jnp.exp2(m - m_new) * acc + _dot(vt, p.astype(BF16))
            new.append((m_new, acc_new))
        return tuple(new)

    init = tuple((jnp.full((1, tq), -jnp.inf, F32), jnp.zeros((LANE, tq), F32)) for _ in range(GROUP))
    carry = step(kc_ref[...], vtc_ref[0], init)
    carry = lax.fori_loop(
        0, jnp.where(is_latent, vtx_ref.shape[0], 0),
        lambda j, cr: step(kx_ref[pl.ds(pl.multiple_of(j * KV_TILE, KV_TILE), KV_TILE), :], vtx_ref[j], cr),
        carry)
    _store_heads(o_ref, [acc for _, acc in carry])


def _flash_bounded_body(q_ref, kc_ref, vtc_ref, kx_ref, vtx_ref, o_ref, qa_ref, acc_ref):
    tq = q_ref.shape[0]
    for g in range(GROUP):
        qa_ref[g * tq:(g + 1) * tq, :] = q_ref[:, g * LANE:(g + 1) * LANE]

    def chunk(kt, vt):
        st = _dot_nt(kt, qa_ref[...])
        return _dot(vt, jnp.exp2(st).astype(BF16))

    is_latent = pl.program_id(2) < pl.num_programs(2) - 1

    @pl.when(is_latent)
    def _():
        n = vtx_ref.shape[0]
        scores = lambda j: _dot_nt(kx_ref[j * KV_TILE:(j + 1) * KV_TILE, :], qa_ref[...])
        st = _dot_nt(kc_ref[...], qa_ref[...])
        for j in range(n + 1):
            st_next = scores(j) if j < n else None
            pv = _dot(vtc_ref[0] if j == 0 else vtx_ref[j - 1], jnp.exp2(st).astype(BF16))
            if j == 0:
                acc_ref[...] = pv
            else:
                acc_ref[...] += pv
            st = st_next

    @pl.when(jnp.logical_not(is_latent))
    def _():
        acc_ref[...] = chunk(kc_ref[...], vtc_ref[0])

    _store_heads(o_ref, [acc_ref[:, g * tq:(g + 1) * tq] for g in range(GROUP)])


def _flash(q, k, vt, nb, t, lc, bounded):
    rows = q.shape[0]
    tq = SEQ_BLOCK
    n_xt = t // KV_TILE
    nq = t // tq
    ctx_q0 = nb * nq
    ctx_tile = nb * t // KV_TILE
    qmap = lambda b, h, i: (jnp.where(i == nq, ctx_q0 + b, b * nq + i), h)
    in_specs = [pl.BlockSpec((tq, GROUP * LANE), qmap),
                pl.BlockSpec((lc, LANE), lambda b, h, i: (nb * t // lc + b, h)),
                pl.BlockSpec((1, LANE, lc), lambda b, h, i: (ctx_tile, h, b)),
                pl.BlockSpec((t, LANE), lambda b, h, i: (b, h)),
                pl.BlockSpec((n_xt, LANE, KV_TILE), lambda b, h, i: (b, h, 0))]
    if bounded:
        body = _flash_bounded_body
        scratch = [pltpu.VMEM((GROUP * tq, LANE), BF16), pltpu.VMEM((LANE, GROUP * tq), F32)]
    else:
        body = _flash_safe_body
        scratch = []
    return pl.pallas_call(
        body,
        grid=(nb, KV_HEADS, nq + 1),
        in_specs=in_specs,
        out_specs=pl.BlockSpec((tq, GROUP * HEAD_DIM), qmap),
        out_shape=jax.ShapeDtypeStruct((rows, ATT_HEADS * HEAD_DIM), BF16),
        scratch_shapes=scratch,
        compiler_params=_cparams("arbitrary", "arbitrary", "arbitrary"),
        name="flash_bounded" if bounded else "flash_safe",
    )(q, k, vt, k, vt)


def _outproj_mlp_body(*refs, n_src, seq_edges):
    mod_ref, g_ref, wo_ref, w1_ref, w2_ref, out_ref = refs[-6:]
    mix_refs = refs[n_src:-6]
    y = mix_refs[0][...] if seq_edges is None else _hyb_mix(*mix_refs, *seq_edges)
    d = D_MODEL
    mod = lambda k: mod_ref[0, :, k * d:(k + 1) * d]
    x1 = _src_rows(refs[:n_src]) + mod(2) * _dot(y, wo_ref[...])
    h2 = _normmod(x1, g_ref[...], mod(3), mod(4)).astype(BF16)
    acc = jnp.zeros_like(x1)
    for j in range(MLP_HIDDEN // d):
        u = jnp.maximum(_dot(h2, w1_ref[:, j * d:(j + 1) * d]), 0.0)
        acc = acc + _dot((u * u).astype(BF16), w2_ref[j * d:(j + 1) * d, :])
    out_ref[...] = x1 + mod(5) * acc


def _outproj_mlp(mix, src, mod, g2, wo, w1, w2, n_tiles, n_xt, seq_edges=None):
    d = D_MODEL
    tm = TOKEN_TILE
    row = lambda i: (i, 0)
    const = lambda i: (0, 0)
    mix_specs = [pl.BlockSpec((tm, d), row)] if seq_edges is None else _hyb_mix_specs(mix[0].shape[0], tm)
    return pl.pallas_call(
        functools.partial(_outproj_mlp_body, n_src=len(src), seq_edges=seq_edges),
        grid=(n_tiles,),
        in_specs=_src_specs(src, tm) + mix_specs + [
                  pl.BlockSpec((1, 1, 6 * d), lambda i: (i // n_xt, 0, 0)),
                  pl.BlockSpec((1, d), const),
                  pl.BlockSpec((d, d), const),
                  pl.BlockSpec((d, MLP_HIDDEN), const),
                  pl.BlockSpec((MLP_HIDDEN, d), const)],
        out_specs=pl.BlockSpec((tm, d), row),
        out_shape=jax.ShapeDtypeStruct((n_tiles * tm, d), F32),
        compiler_params=_cparams("arbitrary"),
        name="outproj_mlp",
    )(*src, *mix, mod, g2, wo, w1, w2)


def _pad_heads(w, n_heads, width):
    d = w.shape[0]
    w = w.reshape(d, n_heads, width)
    return jnp.pad(w, ((0, 0), (0, 0), (0, LANE - width))).reshape(d, n_heads * LANE)


def _hyb_weights(w_in, gate_b):
    sizes = (ML_HEADS * ML_DK, ML_HEADS * ML_DK, ML_HEADS * ML_DV, ML_HEADS * ML_DV, 4 * ML_HEADS,
             SC_WIDTH, SC_WIDTH, SC_WIDTH)
    parts, c0 = [], 0
    for s in sizes:
        parts.append(w_in[:, c0:c0 + s])
        c0 += s
    wq, wk, wv, wo, wg, wb, wc, wu = parts
    wg = jnp.pad(wg, ((0, 0), (0, LANE - wg.shape[1])))
    w = jnp.concatenate([_pad_heads(wq, ML_HEADS, ML_DK), _pad_heads(wk, ML_HEADS, ML_DK), wv, wo, wg, wb, wc, wu],
                        axis=1).astype(BF16)
    gb = jnp.pad(gate_b.astype(F32), (0, LANE - gate_b.shape[0])).reshape(1, LANE)
    return w, gb


def _rope_tables(t, n_ident):
    half = HEAD_DIM // 2
    inv = ROPE_THETA ** (-jnp.arange(0, half, 2, dtype=F32) / half)
    n_rows = t // GRID_W
    r = jnp.arange(n_rows, dtype=F32)[:, None] * inv
    c = jnp.arange(GRID_W, dtype=F32)[:, None] * inv
    z_r, z_c = jnp.zeros_like(r), jnp.zeros_like(c)
    lanes = lambda q0, q1, q2, q3: jnp.concatenate([q0, q1, q2, q3] * 2, axis=-1)
    outer = lambda by_row, by_col: (by_row[:, None, :] + by_col[None, :, :]).reshape(t, LANE)
    cos_t = outer(lanes(jnp.cos(r), jnp.cos(r), z_r, z_r), lanes(z_c, z_c, jnp.cos(c), jnp.cos(c)))
    sin_t = outer(lanes(-jnp.sin(r), jnp.sin(r), z_r, z_r), lanes(z_c, z_c, -jnp.sin(c), jnp.sin(c)))
    cos_t = jnp.concatenate([cos_t, jnp.ones((n_ident, LANE), F32)], axis=0)
    sin_t = jnp.concatenate([sin_t, jnp.zeros((n_ident, LANE), F32)], axis=0)
    return cos_t, sin_t


def kernel(x, c, ctx, c_ctx, ada_w, ada_b, norm1_g, norm2_g, mlp_w1, mlp_w2, hyb_w_in, hyb_gate_b, mlstm_norm_g,
           conv_w, hyb_w_out, att_w_in, q_norm_g, k_norm_g, att_w_out):
    nb, t, d = x.shape
    lc = ctx.shape[1]
    assert d == D_MODEL and nb * lc == TOKEN_TILE and lc == SEQ_BLOCK and t % TOKEN_TILE == 0
    assert nb + 1 <= SUBLANE
    n_xt = t // TOKEN_TILE
    n_tiles = nb * n_xt + 1

    src = (x.reshape(nb * t, d).astype(F32), ctx.reshape(nb * lc, d).astype(F32))
    cvec = jnp.concatenate([c, c_ctx[None, :], jnp.zeros((SUBLANE - nb - 1, d), c.dtype)], axis=0).astype(F32)
    mods = _adaln(cvec, ada_w.astype(F32), ada_b.astype(F32))

    cos_t, sin_t = _rope_tables(t, TOKEN_TILE)
    head_of = lambda axis: lax.broadcasted_iota(jnp.int32, (256, 256), axis) // HEAD_DIM
    bd = (head_of(0) == head_of(1)).astype(BF16)
    pad_gain = lambda g: jnp.tile(g.astype(F32), 256 // HEAD_DIM).reshape(1, 256)
    seq_starts = tuple(b * t for b in range(nb)) + tuple(nb * t + b * lc for b in range(nb))
    seq_ends = tuple((b + 1) * t - 1 for b in range(nb)) + tuple(nb * t + (b + 1) * lc - 1 for b in range(nb))

    for layer in range(DEPTH):
        last = layer == DEPTH - 1
        mod = mods[layer, :nb + 1].reshape(nb + 1, 1, 6 * d)
        g1 = norm1_g[layer].astype(F32).reshape(1, d)
        g2 = norm2_g[layer].astype(F32).reshape(1, d)
        if layer % 2 == 0:
            e = layer // 2
            w, gb = _hyb_weights(hyb_w_in[e], hyb_gate_b[e])
            q, k, vt, o, gate, cb, z = _hyb_in(src, mod, g1, w, gb, n_xt)
            hf = _mlstm(q, k, vt, gate, nb, t, rev=False)
            hb = _mlstm(q, k, vt, gate, nb, t, rev=True)
            cw = jnp.pad(conv_w[e].astype(F32), ((0, SUBLANE - CONV_K), (0, 0)))
            mix = (hf, hb, o, cb, z, z, z, mlstm_norm_g[e].astype(F32).reshape(1, -1), cw)
            seq_edges = (seq_starts, seq_ends)
            wo = hyb_w_out[e].astype(BF16)
        else:
            a = layer // 2
            bound = (HEAD_DIM ** 0.5 * LOG2E * BOUND_MARGIN) * jnp.max(jnp.abs(q_norm_g[a])) * jnp.max(jnp.abs(k_norm_g[a]))
            bound = bound.astype(F32)
            lane_const = (jnp.arange(LANE) == CONST_LANE).astype(F32)
            pads = jnp.stack([-bound * lane_const, lane_const] + [0.0 * lane_const] * (SUBLANE - 2))
            q, k, vt = _att_in(src[0], mod, g1, att_w_in[a].astype(BF16), pad_gain(q_norm_g[a]),
                               pad_gain(k_norm_g[a]), cos_t, sin_t, bd, pads, n_xt)
            y = lax.cond(bound <= MAX_BOUND_LOG2,
                         lambda: _flash(q, k, vt, nb, t, lc, bounded=True),
                         lambda: _flash(q, k, vt, nb, t, lc, bounded=False))
            mix = (y,)
            seq_edges = None
            wo = att_w_out[a].astype(BF16)
        src = (_outproj_mlp(mix, src, mod, g2, wo, mlp_w1[layer].astype(BF16), mlp_w2[layer].astype(BF16),
                            nb * n_xt if last else n_tiles, n_xt, seq_edges),)
    return src[0].reshape(nb, t, d).astype(x.dtype)
```

```python
import functools

import jax
import jax.numpy as jnp
from jax import lax
from jax.experimental import pallas as pl
from jax.experimental.pallas import tpu as pltpu

F32 = jnp.float32
BF16 = jnp.bfloat16

D_MODEL = 1024
DEPTH = 4
GRID_W = 64
EPS = 1e-6
ML_HEADS = 4
ML_DK = 64
ML_DV = 128
ML_CHUNK = 64
SC_WIDTH = D_MODEL // 2
CONV_K = 3
ATT_HEADS = 16
KV_HEADS = 4
HEAD_DIM = 64
GROUP = ATT_HEADS // KV_HEADS
ROPE_THETA = 10000.0
MLP_HIDDEN = 4 * D_MODEL

LANE = 128
SUBLANE = 8
TOKEN_TILE = 512
SEQ_BLOCK = 256
SCAN_CHUNK = SEQ_BLOCK
KV_TILE = TOKEN_TILE
FLASH_CHUNK = 512
VMEM_LIMIT_BYTES = 56 * 1024 * 1024

HYB_COLS = (4 * LANE, 4 * LANE, 4 * ML_DV, 4 * ML_DV, LANE, SC_WIDTH, SC_WIDTH, SC_WIDTH)
ATT_Q_COLS = ATT_HEADS * LANE
ATT_K_COLS = KV_HEADS * LANE


def _cparams(*sem):
    return pltpu.CompilerParams(dimension_semantics=sem, vmem_limit_bytes=VMEM_LIMIT_BYTES)


def _normmod(xf, g, shift, scale):
    var = jnp.mean(xf * xf, axis=-1, keepdims=True)
    return xf * lax.rsqrt(var + EPS) * g * (1.0 + scale) + shift


def _src_specs(src, tm):
    d = src[0].shape[1]
    if len(src) == 1:
        return [pl.BlockSpec((tm, d), lambda i: (i, 0))]
    n_main = src[0].shape[0] // tm
    assert src[1].shape[0] == tm
    return [pl.BlockSpec((tm, d), lambda i: (jnp.minimum(i, n_main - 1), 0)), pl.BlockSpec((tm, d), lambda i: (0, 0))]


def _src_rows(src_refs):
    if len(src_refs) == 1:
        return src_refs[0][...]
    on_ctx_tile = pl.program_id(0) == pl.num_programs(0) - 1
    return jnp.where(on_ctx_tile, src_refs[1][...], src_refs[0][...])


def _dot(a, b):
    return jnp.dot(a, b, preferred_element_type=F32)


def _dot_nt(a, b):
    return lax.dot_general(a, b, (((1,), (1,)), ((), ())), preferred_element_type=F32)


def _adaln_body(c_ref, w_ref, b_ref, o_ref):
    cv = c_ref[...]
    s = cv * jax.nn.sigmoid(cv)
    o_ref[0] = _dot(s.astype(BF16), w_ref[0].astype(BF16)) + b_ref[0]


def _adaln(cvec, ada_w, ada_b):
    depth, d, n = ada_w.shape
    bn = n // 4
    return pl.pallas_call(
        _adaln_body,
        grid=(depth, n // bn),
        in_specs=[pl.BlockSpec((SUBLANE, d), lambda l, j: (0, 0)),
                  pl.BlockSpec((1, d, bn), lambda l, j: (l, 0, j)),
                  pl.BlockSpec((1, 1, bn), lambda l, j: (l, 0, j))],
        out_specs=pl.BlockSpec((1, SUBLANE, bn), lambda l, j: (l, 0, j)),
        out_shape=jax.ShapeDtypeStruct((depth, SUBLANE, n), F32),
        compiler_params=_cparams("arbitrary", "arbitrary"),
        name="adaln",
    )(cvec, ada_w, ada_b.reshape(depth, 1, n))


def _hyb_in_body(*refs, n_src):
    mod_ref, g_ref, w_ref, gb_ref, q_ref, k_ref, vt_ref, o_ref, gate_ref, cb_ref, z_ref = refs[n_src:]
    d = D_MODEL
    h = _normmod(_src_rows(refs[:n_src]), g_ref[...], mod_ref[0, :, 0:d], mod_ref[0, :, d:2 * d]).astype(BF16)
    c0 = 0
    q_ref[...] = (_dot(h, w_ref[:, c0:c0 + 512]) * (ML_DK ** -0.5)).astype(BF16)
    c0 += 512
    k_ref[...] = _dot(h, w_ref[:, c0:c0 + 512]).astype(BF16)
    c0 += 512
    vt_ref[0] = _dot(h, w_ref[:, c0:c0 + 512]).T
    c0 += 512
    o_ref[...] = _dot(h, w_ref[:, c0:c0 + 512])
    c0 += 512
    g = _dot(h, w_ref[:, c0:c0 + LANE]) + gb_ref[...]
    c0 += LANE
    lane = lax.broadcasted_iota(jnp.int32, g.shape, 1)
    is_forget = ((lane // ML_HEADS) % 2) == 1
    logsig = jnp.minimum(g, 0.0) - jnp.log1p(jnp.exp(-jnp.abs(g)))
    gate_ref[...] = jnp.where(is_forget, logsig, g)
    cb_ref[...] = _dot(h, w_ref[:, c0:c0 + 512])
    c0 += 512
    gc = _dot(h, w_ref[:, c0:c0 + 512])
    c0 += 512
    z_ref[...] = gc * _dot(h, w_ref[:, c0:c0 + 512])


def _hyb_in(src, mod, g1, w, gate_b, n_xt):
    rows = sum(s.shape[0] for s in src)
    d = D_MODEL
    tm = TOKEN_TILE
    ncol = w.shape[1]
    row = lambda i: (i, 0)
    const = lambda i: (0, 0)
    outs = [(512, BF16), (512, BF16), None, (512, F32), (LANE, F32), (512, F32), (512, F32)]
    nt = rows // tm
    vt_cols = ML_HEADS * ML_DV
    return pl.pallas_call(
        functools.partial(_hyb_in_body, n_src=len(src)),
        grid=(nt,),
        in_specs=_src_specs(src, tm) + [
                  pl.BlockSpec((1, 1, 6 * d), lambda i: (i // n_xt, 0, 0)),
                  pl.BlockSpec((1, d), const),
                  pl.BlockSpec((d, ncol), const),
                  pl.BlockSpec((1, LANE), const)],
        out_specs=[pl.BlockSpec((tm, o[0]), row) if o else pl.BlockSpec((1, vt_cols, tm), lambda i: (i, 0, 0))
                   for o in outs],
        out_shape=[jax.ShapeDtypeStruct((rows, o[0]), o[1]) if o else jax.ShapeDtypeStruct((nt, vt_cols, tm), F32)
                   for o in outs],
        compiler_params=_cparams("arbitrary"),
        name="hyb_in",
    )(*src, mod, g1, w, gate_b)


def _split3(a):
    hi = a.astype(BF16)
    r = a - hi.astype(F32)
    mid = r.astype(BF16)
    lo = (r - mid.astype(F32)).astype(BF16)
    return hi, mid, lo


def _mlstm_body(qf_ref, kf_ref, vtf_ref, gf_ref, qb_ref, kb_ref, vtb_ref, gb_ref, hf_ref, hb_ref, c_ref, m_ref):
    L = SCAN_CHUNK

    @pl.when(pl.program_id(1) == 0)
    def _():
        c_ref[...] = jnp.zeros_like(c_ref)
        m_ref[...] = jnp.zeros_like(m_ref)

    s_idx = lax.broadcasted_iota(jnp.int32, (L, L), 0)
    j_idx = lax.broadcasted_iota(jnp.int32, (L, L), 1)
    diag = (s_idx == j_idx).astype(BF16)
    ones_rows = jnp.ones((ML_DV, L), F32)

    chains = []
    for d, (q_ref, k_ref, vt_ref, gate_ref, h_ref) in enumerate(((qf_ref, kf_ref, vtf_ref, gf_ref, hf_ref),
                                                                 (qb_ref, kb_ref, vtb_ref, gb_ref, hb_ref))):
        rev = d == 1
        visible = (s_idx >= j_idx) if rev else (s_idx <= j_idx)
        tr = visible.astype(BF16)
        tc = jnp.logical_not(visible).astype(BF16) + diag
        lane0 = 2 * ML_HEADS if rev else 0
        last = 0 if rev else L - 1
        G = gate_ref[...]
        GT = G.T
        bc = sum(_dot(tc, p) for p in _split3(G))
        br = sum(_dot(p, tr) for p in _split3(GT))
        ig_sh = pltpu.roll(G, ML_HEADS, axis=1)
        b_end = bc[last:last + 1, :]
        m0 = m_ref[d, 0:1, :]
        src_col = ig_sh - bc
        m_loc = jnp.max(b_end + src_col, axis=0, keepdims=True)
        m_new = jnp.maximum(b_end + m0, m_loc)
        a2_all = jnp.exp(b_end + m0 - m_new)
        g2_all = jnp.exp(m_loc - m_new)
        m_ref[d, 0:1, :] = m_new
        for h in range(ML_HEADS):
            fl = lane0 + ML_HEADS + h
            cs = slice(h * LANE, (h + 1) * LANE)
            brow = br[fl:fl + 1, :]
            chains.append(dict(
                d=d, h=h, cs=cs, h_ref=h_ref, visible=visible, brow=brow,
                q=q_ref[:, cs], k=k_ref[:, cs],
                vaug=jnp.concatenate([vt_ref[0, cs, :], ones_rows], axis=0),
                c_old=c_ref[d, h],
                w_row=jnp.exp(b_end[:, fl:fl + 1] + GT[fl - ML_HEADS:fl - ML_HEADS + 1, :] - brow - m_loc[:, fl:fl + 1]),
                src=src_col[:, fl:fl + 1], m_inter=brow + m0[:, fl:fl + 1],
                a2=a2_all[:, fl:fl + 1], g2=g2_all[:, fl:fl + 1]))

    for c in chains:
        c["kq"] = _dot_nt(c["k"], c["q"])
        c["inter"] = _dot_nt(c["c_old"].astype(BF16), c["q"])
        c["c_loc"] = _dot((c["vaug"] * c["w_row"]).astype(BF16), c["k"])
    for c in chains:
        logd = jnp.where(c["visible"], c["brow"] + c["src"], -jnp.inf)
        c["m"] = jnp.maximum(c["m_inter"], jnp.max(logd, axis=0, keepdims=True))
        c["s_t"] = (c["kq"] * jnp.exp(logd - c["m"])).astype(BF16)
    for c in chains:
        num_t = _dot(c["vaug"].astype(BF16), c["s_t"]) + jnp.exp(c["m_inter"] - c["m"]) * c["inter"]
        h_t = num_t[:ML_DV, :] / jnp.maximum(jnp.abs(num_t[ML_DV:, :]), jnp.exp(-c["m"]))
        c["h_ref"][:, c["cs"]] = h_t.T
        c_ref[c["d"], c["h"]] = c["a2"] * c["c_old"] + c["g2"] * c["c_loc"]


def _mlstm(q, k, vt, gate, nb, t):
    rows = q.shape[0]
    blk = SEQ_BLOCK
    per_tile = TOKEN_TILE // blk
    nxb = t // blk
    ctx_blk0 = nb * nxb

    def specs(rev):
        def block(b, i):
            xi = (nxb - i) if rev else (i - 1)
            return jnp.where(i == 0, ctx_blk0 + b, b * nxb + xi)
        spec = lambda c: pl.BlockSpec((blk, c), lambda b, i: (block(b, i), 0))
        vt_spec = pl.BlockSpec((1, ML_HEADS * ML_DV, blk),
                               lambda b, i: (block(b, i) // per_tile, 0, block(b, i) % per_tile))
        return [spec(512), spec(512), vt_spec, spec(LANE)], spec(512)

    (in_f, out_f), (in_b, out_b) = specs(False), specs(True)
    return pl.pallas_call(
        _mlstm_body,
        grid=(nb, nxb + 1),
        in_specs=in_f + in_b,
        out_specs=[out_f, out_b],
        out_shape=[jax.ShapeDtypeStruct((rows, 512), F32)] * 2,
        scratch_shapes=[pltpu.VMEM((2, ML_HEADS, 2 * ML_DV, LANE), F32), pltpu.VMEM((2, SUBLANE, LANE), F32)],
        compiler_params=_cparams("arbitrary", "arbitrary"),
        name="mlstm",
    )(q, k, vt, gate, q, k, vt, gate)


def _hyb_mix(hf_ref, hb_ref, o_ref, cb_ref, z_ref, zp_ref, zn_ref, mg_ref, cw_ref, starts, ends):
    tm = hf_ref.shape[0]
    i = pl.program_id(0)
    parts = []
    for h in range(ML_HEADS):
        cs = slice(h * ML_DV, (h + 1) * ML_DV)
        blk = hf_ref[:, cs] + hb_ref[:, cs]
        var = jnp.mean(blk * blk, axis=-1, keepdims=True)
        hn = blk * lax.rsqrt(var + EPS) * mg_ref[:, cs]
        parts.append((hn * jax.nn.sigmoid(o_ref[:, cs])).astype(BF16))
    z = z_ref[...]
    loc = lax.broadcasted_iota(jnp.int32, (tm, 1), 0)
    row = loc + i * tm
    is_start = functools.reduce(jnp.logical_or, [row == r for r in starts])
    is_end = functools.reduce(jnp.logical_or, [row == r for r in ends])
    zprev = jnp.where(loc == 0, zp_ref[SUBLANE - 1:SUBLANE, :], pltpu.roll(z, 1, axis=0))
    zprev = jnp.where(is_start, 0.0, zprev)
    znext = jnp.where(loc == tm - 1, zn_ref[0:1, :], pltpu.roll(z, tm - 1, axis=0))
    znext = jnp.where(is_end, 0.0, znext)
    conv = cw_ref[0:1, :] * zprev + cw_ref[1:2, :] * z + cw_ref[2:3, :] * znext
    parts.append((cb_ref[...] * conv).astype(BF16))
    return jnp.concatenate(parts, axis=1)


def _hyb_mix_specs(rows, tm):
    per = tm // SUBLANE
    nblk = rows // SUBLANE
    const = lambda i: (0, 0)
    return [pl.BlockSpec((tm, 512), lambda i: (i, 0))] * 5 + [
        pl.BlockSpec((SUBLANE, 512), lambda i: (jnp.maximum(i * per - 1, 0), 0)),
        pl.BlockSpec((SUBLANE, 512), lambda i: (jnp.minimum((i + 1) * per, nblk - 1), 0)),
        pl.BlockSpec((1, 512), const),
        pl.BlockSpec((SUBLANE, 512), const)]


CONST_LANE = HEAD_DIM
V_ROWS = HEAD_DIM + 16
LOG2E = 1.4426950408889634
BOUND_MARGIN = 1.02
MAX_BOUND_LOG2 = 48.0


def _att_in_body(x_ref, mod_ref, g_ref, w_ref, qg_ref, kg_ref, cos_ref, sin_ref, bd_ref, pads_ref,
                 q_ref, k_ref, vt_ref):
    d = D_MODEL
    h = _normmod(x_ref[...], g_ref[...], mod_ref[0, :, 0:d], mod_ref[0, :, d:2 * d]).astype(BF16)
    cosv = cos_ref[...]
    sinv = sin_ref[...]
    bd = bd_ref[...]
    lane = lax.broadcasted_iota(jnp.int32, cosv.shape, 1)
    upper_half = ((lane // (HEAD_DIM // 4)) % 2) == 1
    first_head = lane < HEAD_DIM

    def norm_rope(p, ssq, gain, scale, pad, out_ref, c_out):
        pn = p * (lax.rsqrt(ssq * (1.0 / HEAD_DIM) + EPS) * scale) * gain
        for half in range(2):
            xh = pn[:, half * LANE:(half + 1) * LANE]
            partner = jnp.where(upper_half, pltpu.roll(xh, HEAD_DIM // 4, axis=1),
                                pltpu.roll(xh, LANE - HEAD_DIM // 4, axis=1))
            y = xh * cosv + partner * sinv
            c0 = c_out + 2 * half * LANE
            out_ref[:, c0:c0 + LANE] = jnp.where(first_head, y, pad).astype(BF16)
            out_ref[:, c0 + LANE:c0 + 2 * LANE] = jnp.where(first_head, pltpu.roll(y, HEAD_DIM, axis=1), pad).astype(BF16)

    nq, nk = ATT_HEADS * HEAD_DIM, KV_HEADS * HEAD_DIM
    q_scale = HEAD_DIM ** -0.5 * LOG2E
    n_blk = (nq + nk) // 256
    ps = [_dot(h, w_ref[:, j * 256:(j + 1) * 256]) for j in range(n_blk)]
    v = _dot(h, w_ref[:, nq + nk:nq + 2 * nk])
    ssqs = [_dot((p * p).astype(BF16), bd) for p in ps]
    for j in range(nq // 256):
        norm_rope(ps[j], ssqs[j], qg_ref[...], q_scale, pads_ref[0:1, :], q_ref, 2 * j * 256)
    norm_rope(ps[-1], ssqs[-1], kg_ref[...], 1.0, pads_ref[1:2, :], k_ref, 0)
    vt = v.T.astype(BF16)
    tail_row = lax.broadcasted_iota(jnp.int32, (V_ROWS - HEAD_DIM, vt.shape[1]), 0)
    tail = jnp.where(tail_row == 0, 1.0, 0.0).astype(BF16)
    for hh in range(KV_HEADS):
        r0 = hh * V_ROWS
        vt_ref[0, r0:r0 + HEAD_DIM, :] = vt[hh * HEAD_DIM:(hh + 1) * HEAD_DIM, :]
        vt_ref[0, r0 + HEAD_DIM:r0 + V_ROWS, :] = tail


def _att_in(stream, mod, g1, w, qg, kg, cos_t, sin_t, bd, pads, n_xt):
    rows, d = stream.shape
    tm = TOKEN_TILE
    nt = rows // tm
    ncol = w.shape[1]
    row = lambda i: (i, 0)
    const = lambda i: (0, 0)
    tab = lambda i: (jnp.where(i == nt - 1, n_xt, i % n_xt), 0)
    return pl.pallas_call(
        _att_in_body,
        grid=(nt,),
        in_specs=[pl.BlockSpec((tm, d), row),
                  pl.BlockSpec((1, 1, 6 * d), lambda i: (i // n_xt, 0, 0)),
                  pl.BlockSpec((1, d), const),
                  pl.BlockSpec((d, ncol), const),
                  pl.BlockSpec((1, 256), const),
                  pl.BlockSpec((1, 256), const),
                  pl.BlockSpec((tm, LANE), tab),
                  pl.BlockSpec((tm, LANE), tab),
                  pl.BlockSpec((256, 256), const),
                  pl.BlockSpec((SUBLANE, LANE), const)],
        out_specs=[pl.BlockSpec((tm, ATT_Q_COLS), row),
                   pl.BlockSpec((tm, ATT_K_COLS), row),
                   pl.BlockSpec((1, KV_HEADS * V_ROWS, tm), lambda i: (i, 0, 0))],
        out_shape=[jax.ShapeDtypeStruct((rows, ATT_Q_COLS), BF16),
                   jax.ShapeDtypeStruct((rows, ATT_K_COLS), BF16),
                   jax.ShapeDtypeStruct((nt, KV_HEADS * V_ROWS, tm), BF16)],
        compiler_params=_cparams("arbitrary"),
        name="att_in",
    )(stream, mod, g1, w, qg, kg, cos_t, sin_t, bd, pads)


def _store_heads(o_ref, accs):
    outs = [a[:HEAD_DIM, :] * (1.0 / a[CONST_LANE:CONST_LANE + 1, :]) for a in accs]
    for pair in range(GROUP // 2):
        both = jnp.concatenate(outs[2 * pair:2 * pair + 2], axis=0)
        o_ref[:, pair * LANE:(pair + 1) * LANE] = both.T.astype(BF16)


def _flash_safe_body(q_ref, kc_ref, vtc_ref, kx_ref, vtx_ref, o_ref):
    is_latent = pl.program_id(2) < pl.num_programs(2) - 1
    tq = q_ref.shape[0]
    qs = [q_ref[:, g * LANE:(g + 1) * LANE] for g in range(GROUP)]

    def step(kt, vt, carry):
        new = []
        for g in range(GROUP):
            m, acc = carry[g]
            st = _dot_nt(kt, qs[g])
            m_new = jnp.maximum(m, jnp.max(st, axis=0, keepdims=True))
            p = jnp.exp2(st - m_new)
            acc_new = jnp.exp2(m - m_new) * acc + _dot(vt, p.astype(BF16))
            new.append((m_new, acc_new))
        return tuple(new)

    init = tuple((jnp.full((1, tq), -jnp.inf, F32), jnp.zeros((V_ROWS, tq), F32)) for _ in range(GROUP))
    carry = step(kc_ref[...], vtc_ref[0], init)
    carry = lax.fori_loop(
        0, jnp.where(is_latent, vtx_ref.shape[0], 0),
        lambda j, cr: step(kx_ref[pl.ds(pl.multiple_of(j * KV_TILE, KV_TILE), KV_TILE), :], vtx_ref[j], cr),
        carry)
    _store_heads(o_ref, [acc for _, acc in carry])


def _flash_bounded_body(q_ref, kc_ref, vtc_ref, kx_ref, vtx_ref, o_ref, qa_ref, acc_ref):
    tq = q_ref.shape[0]
    for g in range(GROUP):
        qa_ref[g * tq:(g + 1) * tq, :] = q_ref[:, g * LANE:(g + 1) * LANE]

    def chunk(kt, vt):
        st = _dot_nt(kt, qa_ref[...])
        return _dot(vt, jnp.exp2(st).astype(BF16))

    is_latent = pl.program_id(2) < pl.num_programs(2) - 1

    @pl.when(is_latent)
    def _():
        ck = FLASH_CHUNK
        per_tile = KV_TILE // ck
        n = vtx_ref.shape[0] * per_tile
        scores = lambda c: _dot_nt(kx_ref[c * ck:(c + 1) * ck, :], qa_ref[...])
        values_t = lambda c: vtx_ref[c // per_tile, :, (c % per_tile) * ck:(c % per_tile + 1) * ck]
        st = _dot_nt(kc_ref[...], qa_ref[...])
        for c in range(n + 1):
            st_next = scores(c) if c < n else None
            pv = _dot(vtc_ref[0] if c == 0 else values_t(c - 1), jnp.exp2(st).astype(BF16))
            if c == 0:
                acc_ref[...] = pv
            else:
                acc_ref[...] += pv
            st = st_next

    @pl.when(jnp.logical_not(is_latent))
    def _():
        acc_ref[...] = chunk(kc_ref[...], vtc_ref[0])

    _store_heads(o_ref, [acc_ref[:, g * tq:(g + 1) * tq] for g in range(GROUP)])


def _flash(q, k, vt, nb, t, lc, bounded):
    rows = q.shape[0]
    tq = SEQ_BLOCK
    n_xt = t // KV_TILE
    nq = t // tq
    ctx_q0 = nb * nq
    ctx_tile = nb * t // KV_TILE
    qmap = lambda b, h, i: (jnp.where(i == nq, ctx_q0 + b, b * nq + i), h)
    in_specs = [pl.BlockSpec((tq, GROUP * LANE), qmap),
                pl.BlockSpec((lc, LANE), lambda b, h, i: (nb * t // lc + b, h)),
                pl.BlockSpec((1, V_ROWS, lc), lambda b, h, i: (ctx_tile, h, b)),
                pl.BlockSpec((t, LANE), lambda b, h, i: (b, h)),
                pl.BlockSpec((n_xt, V_ROWS, KV_TILE), lambda b, h, i: (b, h, 0))]
    if bounded:
        body = _flash_bounded_body
        scratch = [pltpu.VMEM((GROUP * tq, LANE), BF16), pltpu.VMEM((V_ROWS, GROUP * tq), F32)]
    else:
        body = _flash_safe_body
        scratch = []
    return pl.pallas_call(
        body,
        grid=(nb, KV_HEADS, nq + 1),
        in_specs=in_specs,
        out_specs=pl.BlockSpec((tq, GROUP * HEAD_DIM), qmap),
        out_shape=jax.ShapeDtypeStruct((rows, ATT_HEADS * HEAD_DIM), BF16),
        scratch_shapes=scratch,
        compiler_params=_cparams("arbitrary", "arbitrary", "arbitrary"),
        name="flash_bounded" if bounded else "flash_safe",
    )(q, k, vt, k, vt)


def _outproj_mlp_body(*refs, n_src, seq_edges):
    mod_ref, g_ref, wo_ref, w1_ref, w2_ref, out_ref = refs[-6:]
    mix_refs = refs[n_src:-6]
    y = mix_refs[0][...] if seq_edges is None else _hyb_mix(*mix_refs, *seq_edges)
    d = D_MODEL
    mod = lambda k: mod_ref[0, :, k * d:(k + 1) * d]
    x1 = _src_rows(refs[:n_src]) + mod(2) * _dot(y, wo_ref[...])
    h2 = _normmod(x1, g_ref[...], mod(3), mod(4)).astype(BF16)
    acc = jnp.zeros_like(x1)
    for j in range(MLP_HIDDEN // d):
        u = jnp.maximum(_dot(h2, w1_ref[:, j * d:(j + 1) * d]), 0.0)
        acc = acc + _dot((u * u).astype(BF16), w2_ref[j * d:(j + 1) * d, :])
    out_ref[...] = x1 + mod(5) * acc


def _outproj_mlp(mix, src, mod, g2, wo, w1, w2, n_tiles, n_xt, seq_edges=None):
    d = D_MODEL
    tm = TOKEN_TILE
    row = lambda i: (i, 0)
    const = lambda i: (0, 0)
    mix_specs = [pl.BlockSpec((tm, d), row)] if seq_edges is None else _hyb_mix_specs(mix[0].shape[0], tm)
    return pl.pallas_call(
        functools.partial(_outproj_mlp_body, n_src=len(src), seq_edges=seq_edges),
        grid=(n_tiles,),
        in_specs=_src_specs(src, tm) + mix_specs + [
                  pl.BlockSpec((1, 1, 6 * d), lambda i: (i // n_xt, 0, 0)),
                  pl.BlockSpec((1, d), const),
                  pl.BlockSpec((d, d), const),
                  pl.BlockSpec((d, MLP_HIDDEN), const),
                  pl.BlockSpec((MLP_HIDDEN, d), const)],
        out_specs=pl.BlockSpec((tm, d), row),
        out_shape=jax.ShapeDtypeStruct((n_tiles * tm, d), F32),
        compiler_params=_cparams("arbitrary"),
        name="outproj_mlp",
    )(*src, *mix, mod, g2, wo, w1, w2)


def _pad_heads(w, n_heads, width):
    d = w.shape[0]
    w = w.reshape(d, n_heads, width)
    return jnp.pad(w, ((0, 0), (0, 0), (0, LANE - width))).reshape(d, n_heads * LANE)


def _hyb_weights(w_in, gate_b):
    sizes = (ML_HEADS * ML_DK, ML_HEADS * ML_DK, ML_HEADS * ML_DV, ML_HEADS * ML_DV, 4 * ML_HEADS,
             SC_WIDTH, SC_WIDTH, SC_WIDTH)
    parts, c0 = [], 0
    for s in sizes:
        parts.append(w_in[:, c0:c0 + s])
        c0 += s
    wq, wk, wv, wo, wg, wb, wc, wu = parts
    wg = jnp.pad(wg, ((0, 0), (0, LANE - wg.shape[1])))
    w = jnp.concatenate([_pad_heads(wq, ML_HEADS, ML_DK), _pad_heads(wk, ML_HEADS, ML_DK), wv, wo, wg, wb, wc, wu],
                        axis=1).astype(BF16)
    gb = jnp.pad(gate_b.astype(F32), (0, LANE - gate_b.shape[0])).reshape(1, LANE)
    return w, gb


def _rope_tables(t, n_ident):
    half = HEAD_DIM // 2
    inv = ROPE_THETA ** (-jnp.arange(0, half, 2, dtype=F32) / half)
    n_rows = t // GRID_W
    r = jnp.arange(n_rows, dtype=F32)[:, None] * inv
    c = jnp.arange(GRID_W, dtype=F32)[:, None] * inv
    z_r, z_c = jnp.zeros_like(r), jnp.zeros_like(c)
    lanes = lambda q0, q1, q2, q3: jnp.concatenate([q0, q1, q2, q3] * 2, axis=-1)
    outer = lambda by_row, by_col: (by_row[:, None, :] + by_col[None, :, :]).reshape(t, LANE)
    cos_t = outer(lanes(jnp.cos(r), jnp.cos(r), z_r, z_r), lanes(z_c, z_c, jnp.cos(c), jnp.cos(c)))
    sin_t = outer(lanes(-jnp.sin(r), jnp.sin(r), z_r, z_r), lanes(z_c, z_c, -jnp.sin(c), jnp.sin(c)))
    cos_t = jnp.concatenate([cos_t, jnp.ones((n_ident, LANE), F32)], axis=0)
    sin_t = jnp.concatenate([sin_t, jnp.zeros((n_ident, LANE), F32)], axis=0)
    return cos_t, sin_t


def kernel(x, c, ctx, c_ctx, ada_w, ada_b, norm1_g, norm2_g, mlp_w1, mlp_w2, hyb_w_in, hyb_gate_b, mlstm_norm_g,
           conv_w, hyb_w_out, att_w_in, q_norm_g, k_norm_g, att_w_out):
    nb, t, d = x.shape
    lc = ctx.shape[1]
    assert d == D_MODEL and nb * lc == TOKEN_TILE and lc == SEQ_BLOCK and t % TOKEN_TILE == 0
    assert nb + 1 <= SUBLANE
    n_xt = t // TOKEN_TILE
    n_tiles = nb * n_xt + 1

    src = (x.reshape(nb * t, d).astype(F32), ctx.reshape(nb * lc, d).astype(F32))
    cvec = jnp.concatenate([c, c_ctx[None, :], jnp.zeros((SUBLANE - nb - 1, d), c.dtype)], axis=0).astype(F32)
    mods = _adaln(cvec, ada_w.astype(F32), ada_b.astype(F32))

    cos_t, sin_t = _rope_tables(t, TOKEN_TILE)
    head_of = lambda axis: lax.broadcasted_iota(jnp.int32, (256, 256), axis) // HEAD_DIM
    bd = (head_of(0) == head_of(1)).astype(BF16)
    pad_gain = lambda g: jnp.tile(g.astype(F32), 256 // HEAD_DIM).reshape(1, 256)
    seq_starts = tuple(b * t for b in range(nb)) + tuple(nb * t + b * lc for b in range(nb))
    seq_ends = tuple((b + 1) * t - 1 for b in range(nb)) + tuple(nb * t + (b + 1) * lc - 1 for b in range(nb))

    for layer in range(DEPTH):
        last = layer == DEPTH - 1
        mod = mods[layer, :nb + 1].reshape(nb + 1, 1, 6 * d)
        g1 = norm1_g[layer].astype(F32).reshape(1, d)
        g2 = norm2_g[layer].astype(F32).reshape(1, d)
        if layer % 2 == 0:
            e = layer // 2
            w, gb = _hyb_weights(hyb_w_in[e], hyb_gate_b[e])
            q, k, vt, o, gate, cb, z = _hyb_in(src, mod, g1, w, gb, n_xt)
            hf, hb = _mlstm(q, k, vt, gate, nb, t)
            cw = jnp.pad(conv_w[e].astype(F32), ((0, SUBLANE - CONV_K), (0, 0)))
            mix = (hf, hb, o, cb, z, z, z, mlstm_norm_g[e].astype(F32).reshape(1, -1), cw)
            seq_edges = (seq_starts, seq_ends)
            wo = hyb_w_out[e].astype(BF16)
        else:
            a = layer // 2
            bound = (HEAD_DIM ** 0.5 * LOG2E * BOUND_MARGIN) * jnp.max(jnp.abs(q_norm_g[a])) * jnp.max(jnp.abs(k_norm_g[a]))
            bound = bound.astype(F32)
            lane_const = (jnp.arange(LANE) == CONST_LANE).astype(F32)
            pads = jnp.stack([-bound * lane_const, lane_const] + [0.0 * lane_const] * (SUBLANE - 2))
            q, k, vt = _att_in(src[0], mod, g1, att_w_in[a].astype(BF16), pad_gain(q_norm_g[a]),
                               pad_gain(k_norm_g[a]), cos_t, sin_t, bd, pads, n_xt)
            y = lax.cond(bound <= MAX_BOUND_LOG2,
                         lambda: _flash(q, k, vt, nb, t, lc, bounded=True),
                         lambda: _flash(q, k, vt, nb, t, lc, bounded=False))
            mix = (y,)
            seq_edges = None
            wo = att_w_out[a].astype(BF16)
        src = (_outproj_mlp(mix, src, mod, g2, wo, mlp_w1[layer].astype(BF16), mlp_w2[layer].astype(BF16),
                            nb * n_xt if last else n_tiles, n_xt, seq_edges),)
    return src[0].reshape(nb, t, d).astype(x.dtype)
```

```python
import functools

import jax
import jax.numpy as jnp
from jax import lax
from jax.experimental import pallas as pl
from jax.experimental.pallas import tpu as pltpu

F32 = jnp.float32
BF16 = jnp.bfloat16

D_MODEL = 1024
DEPTH = 4
GRID_W = 64
EPS = 1e-6
ML_HEADS = 4
ML_DK = 64
ML_DV = 128
ML_CHUNK = 64
SC_WIDTH = D_MODEL // 2
CONV_K = 3
ATT_HEADS = 16
KV_HEADS = 4
HEAD_DIM = 64
GROUP = ATT_HEADS // KV_HEADS
ROPE_THETA = 10000.0
MLP_HIDDEN = 4 * D_MODEL

LANE = 128
SUBLANE = 8
TOKEN_TILE = 512
SEQ_BLOCK = 256
SCAN_CHUNK = SEQ_BLOCK
KV_TILE = TOKEN_TILE
FLASH_CHUNK = 512
VMEM_LIMIT_BYTES = 56 * 1024 * 1024

ATT_Q_COLS = ATT_HEADS * LANE
ATT_K_COLS = KV_HEADS * LANE


def _cparams(*sem):
    return pltpu.CompilerParams(dimension_semantics=sem, vmem_limit_bytes=VMEM_LIMIT_BYTES)


def _normmod(xf, g, shift, scale):
    var = jnp.mean(xf * xf, axis=-1, keepdims=True)
    return xf * lax.rsqrt(var + EPS) * g * (1.0 + scale) + shift


def _src_specs(src, tm):
    d = src[0].shape[1]
    if len(src) == 1:
        return [pl.BlockSpec((tm, d), lambda i: (i, 0))]
    n_main = src[0].shape[0] // tm
    assert src[1].shape[0] == tm
    return [pl.BlockSpec((tm, d), lambda i: (jnp.minimum(i, n_main - 1), 0)), pl.BlockSpec((tm, d), lambda i: (0, 0))]


def _src_rows(src_refs):
    if len(src_refs) == 1:
        return src_refs[0][...]
    on_ctx_tile = pl.program_id(0) == pl.num_programs(0) - 1
    return jnp.where(on_ctx_tile, src_refs[1][...], src_refs[0][...])


def _dot(a, b):
    return jnp.dot(a, b, preferred_element_type=F32)


def _dot_nt(a, b):
    return lax.dot_general(a, b, (((1,), (1,)), ((), ())), preferred_element_type=F32)


def _adaln_body(c_ref, w_ref, b_ref, o_ref):
    cv = c_ref[...]
    s = cv * jax.nn.sigmoid(cv)
    o_ref[0] = _dot(s.astype(BF16), w_ref[0].astype(BF16)) + b_ref[0]


def _adaln(cvec, ada_w, ada_b):
    depth, d, n = ada_w.shape
    bn = n // 4
    return pl.pallas_call(
        _adaln_body,
        grid=(depth, n // bn),
        in_specs=[pl.BlockSpec((SUBLANE, d), lambda l, j: (0, 0)),
                  pl.BlockSpec((1, d, bn), lambda l, j: (l, 0, j)),
                  pl.BlockSpec((1, 1, bn), lambda l, j: (l, 0, j))],
        out_specs=pl.BlockSpec((1, SUBLANE, bn), lambda l, j: (l, 0, j)),
        out_shape=jax.ShapeDtypeStruct((depth, SUBLANE, n), F32),
        compiler_params=_cparams("arbitrary", "arbitrary"),
        name="adaln",
    )(cvec, ada_w, ada_b.reshape(depth, 1, n))


def _hyb_in_body(*refs, n_src):
    mod_ref, g_ref, w_ref, gb_ref, q_ref, k_ref, vt_ref, o_ref, gate_ref, cb_ref, z_ref = refs[n_src:]
    d = D_MODEL
    h = _normmod(_src_rows(refs[:n_src]), g_ref[...], mod_ref[0, :, 0:d], mod_ref[0, :, d:2 * d]).astype(BF16)
    c0 = 0
    qk = ML_HEADS * ML_DK
    q_ref[...] = (_dot(h, w_ref[:, c0:c0 + qk]) * (ML_DK ** -0.5)).astype(BF16)
    c0 += qk
    k_ref[...] = _dot(h, w_ref[:, c0:c0 + qk]).astype(BF16)
    c0 += qk
    vt_ref[0] = _dot(h, w_ref[:, c0:c0 + 512]).T
    c0 += 512
    o_ref[...] = _dot(h, w_ref[:, c0:c0 + 512])
    c0 += 512
    g = _dot(h, w_ref[:, c0:c0 + LANE]) + gb_ref[...]
    c0 += LANE
    lane = lax.broadcasted_iota(jnp.int32, g.shape, 1)
    is_forget = ((lane // ML_HEADS) % 2) == 1
    logsig = jnp.minimum(g, 0.0) - jnp.log1p(jnp.exp(-jnp.abs(g)))
    gate_ref[...] = jnp.where(is_forget, logsig, g)
    cb_ref[...] = _dot(h, w_ref[:, c0:c0 + 512])
    c0 += 512
    gc = _dot(h, w_ref[:, c0:c0 + 512])
    c0 += 512
    z_ref[...] = gc * _dot(h, w_ref[:, c0:c0 + 512])


def _hyb_in(src, mod, g1, w, gate_b, n_xt):
    rows = sum(s.shape[0] for s in src)
    d = D_MODEL
    tm = TOKEN_TILE
    ncol = w.shape[1]
    row = lambda i: (i, 0)
    const = lambda i: (0, 0)
    qk = ML_HEADS * ML_DK
    outs = [(qk, BF16), (qk, BF16), None, (512, F32), (LANE, F32), (512, F32), (512, F32)]
    nt = rows // tm
    vt_cols = ML_HEADS * ML_DV
    return pl.pallas_call(
        functools.partial(_hyb_in_body, n_src=len(src)),
        grid=(nt,),
        in_specs=_src_specs(src, tm) + [
                  pl.BlockSpec((1, 1, 6 * d), lambda i: (i // n_xt, 0, 0)),
                  pl.BlockSpec((1, d), const),
                  pl.BlockSpec((d, ncol), const),
                  pl.BlockSpec((1, LANE), const)],
        out_specs=[pl.BlockSpec((tm, o[0]), row) if o else pl.BlockSpec((1, vt_cols, tm), lambda i: (i, 0, 0))
                   for o in outs],
        out_shape=[jax.ShapeDtypeStruct((rows, o[0]), o[1]) if o else jax.ShapeDtypeStruct((nt, vt_cols, tm), F32)
                   for o in outs],
        compiler_params=_cparams("arbitrary"),
        name="hyb_in",
    )(*src, mod, g1, w, gate_b)


def _split3(a):
    hi = a.astype(BF16)
    r = a - hi.astype(F32)
    mid = r.astype(BF16)
    lo = (r - mid.astype(F32)).astype(BF16)
    return hi, mid, lo


def _mlstm_body(qf_ref, kf_ref, vtf_ref, gf_ref, qb_ref, kb_ref, vtb_ref, gb_ref, hf_ref, hb_ref, c_ref, m_ref):
    L = SCAN_CHUNK

    @pl.when(pl.program_id(1) == 0)
    def _():
        c_ref[...] = jnp.zeros_like(c_ref)
        m_ref[...] = jnp.zeros_like(m_ref)

    s_idx = lax.broadcasted_iota(jnp.int32, (L, L), 0)
    j_idx = lax.broadcasted_iota(jnp.int32, (L, L), 1)
    diag = (s_idx == j_idx).astype(BF16)
    ones_rows = jnp.ones((ML_DV, L), F32)
    pair_lane = lax.broadcasted_iota(jnp.int32, (1, LANE), 1)

    chains = []
    for d, (q_ref, k_ref, vt_ref, gate_ref, h_ref) in enumerate(((qf_ref, kf_ref, vtf_ref, gf_ref, hf_ref),
                                                                 (qb_ref, kb_ref, vtb_ref, gb_ref, hb_ref))):
        rev = d == 1
        visible = (s_idx >= j_idx) if rev else (s_idx <= j_idx)
        tr = visible.astype(BF16)
        tc = jnp.logical_not(visible).astype(BF16) + diag
        lane0 = 2 * ML_HEADS if rev else 0
        last = 0 if rev else L - 1
        G = gate_ref[...]
        GT = G.T
        bc = sum(_dot(tc, p) for p in _split3(G))
        br = sum(_dot(p, tr) for p in _split3(GT))
        ig_sh = pltpu.roll(G, ML_HEADS, axis=1)
        b_end = bc[last:last + 1, :]
        m0 = m_ref[d, 0:1, :]
        src_col = ig_sh - bc
        m_loc = jnp.max(b_end + src_col, axis=0, keepdims=True)
        m_new = jnp.maximum(b_end + m0, m_loc)
        a2_all = jnp.exp(b_end + m0 - m_new)
        g2_all = jnp.exp(m_loc - m_new)
        m_ref[d, 0:1, :] = m_new
        for h in range(ML_HEADS):
            fl = lane0 + ML_HEADS + h
            cs = slice(h * LANE, (h + 1) * LANE)
            pair = slice((h // 2) * LANE, (h // 2 + 1) * LANE)
            own = (pair_lane // ML_DK) == (h % 2)
            brow = br[fl:fl + 1, :]
            chains.append(dict(
                d=d, h=h, cs=cs, h_ref=h_ref, visible=visible, brow=brow,
                q=q_ref[:, pair], k=jnp.where(own, k_ref[:, pair], jnp.zeros((), BF16)),
                vaug=jnp.concatenate([vt_ref[0, cs, :], ones_rows], axis=0),
                c_old=c_ref[d, h],
                w_row=jnp.exp(b_end[:, fl:fl + 1] + GT[fl - ML_HEADS:fl - ML_HEADS + 1, :] - brow - m_loc[:, fl:fl + 1]),
                src=src_col[:, fl:fl + 1], m_inter=brow + m0[:, fl:fl + 1],
                a2=a2_all[:, fl:fl + 1], g2=g2_all[:, fl:fl + 1]))

    for c in chains:
        c["kq"] = _dot_nt(c["k"], c["q"])
        c["inter"] = _dot_nt(c["c_old"].astype(BF16), c["q"])
        c["c_loc"] = _dot((c["vaug"] * c["w_row"]).astype(BF16), c["k"])
    for c in chains:
        logd = jnp.where(c["visible"], c["brow"] + c["src"], -jnp.inf)
        c["m"] = jnp.maximum(c["m_inter"], jnp.max(logd, axis=0, keepdims=True))
        c["s_t"] = (c["kq"] * jnp.exp(logd - c["m"])).astype(BF16)
    for c in chains:
        num_t = _dot(c["vaug"].astype(BF16), c["s_t"]) + jnp.exp(c["m_inter"] - c["m"]) * c["inter"]
        h_t = num_t[:ML_DV, :] / jnp.maximum(jnp.abs(num_t[ML_DV:, :]), jnp.exp(-c["m"]))
        c["h_ref"][:, c["cs"]] = h_t.T
        c_ref[c["d"], c["h"]] = c["a2"] * c["c_old"] + c["g2"] * c["c_loc"]


def _mlstm(q, k, vt, gate, nb, t):
    rows = q.shape[0]
    blk = SEQ_BLOCK
    per_tile = TOKEN_TILE // blk
    nxb = t // blk
    ctx_blk0 = nb * nxb

    def specs(rev):
        def block(b, i):
            xi = (nxb - i) if rev else (i - 1)
            return jnp.where(i == 0, ctx_blk0 + b, b * nxb + xi)
        spec = lambda c: pl.BlockSpec((blk, c), lambda b, i: (block(b, i), 0))
        vt_spec = pl.BlockSpec((1, ML_HEADS * ML_DV, blk),
                               lambda b, i: (block(b, i) // per_tile, 0, block(b, i) % per_tile))
        qk = ML_HEADS * ML_DK
        return [spec(qk), spec(qk), vt_spec, spec(LANE)], spec(512)

    (in_f, out_f), (in_b, out_b) = specs(False), specs(True)
    return pl.pallas_call(
        _mlstm_body,
        grid=(nb, nxb + 1),
        in_specs=in_f + in_b,
        out_specs=[out_f, out_b],
        out_shape=[jax.ShapeDtypeStruct((rows, 512), F32)] * 2,
        scratch_shapes=[pltpu.VMEM((2, ML_HEADS, 2 * ML_DV, LANE), F32), pltpu.VMEM((2, SUBLANE, LANE), F32)],
        compiler_params=_cparams("arbitrary", "arbitrary"),
        name="mlstm",
    )(q, k, vt, gate, q, k, vt, gate)


def _hyb_mix(hf_ref, hb_ref, o_ref, cb_ref, z_ref, zp_ref, zn_ref, mg_ref, cw_ref, starts, ends):
    tm = hf_ref.shape[0]
    i = pl.program_id(0)
    parts = []
    for h in range(ML_HEADS):
        cs = slice(h * ML_DV, (h + 1) * ML_DV)
        blk = hf_ref[:, cs] + hb_ref[:, cs]
        var = jnp.mean(blk * blk, axis=-1, keepdims=True)
        hn = blk * lax.rsqrt(var + EPS) * mg_ref[:, cs]
        parts.append((hn * jax.nn.sigmoid(o_ref[:, cs])).astype(BF16))
    z = z_ref[...]
    loc = lax.broadcasted_iota(jnp.int32, (tm, 1), 0)
    row = loc + i * tm
    is_start = functools.reduce(jnp.logical_or, [row == r for r in starts])
    is_end = functools.reduce(jnp.logical_or, [row == r for r in ends])
    zprev = jnp.where(loc == 0, zp_ref[SUBLANE - 1:SUBLANE, :], pltpu.roll(z, 1, axis=0))
    zprev = jnp.where(is_start, 0.0, zprev)
    znext = jnp.where(loc == tm - 1, zn_ref[0:1, :], pltpu.roll(z, tm - 1, axis=0))
    znext = jnp.where(is_end, 0.0, znext)
    conv = cw_ref[0:1, :] * zprev + cw_ref[1:2, :] * z + cw_ref[2:3, :] * znext
    parts.append((cb_ref[...] * conv).astype(BF16))
    return jnp.concatenate(parts, axis=1)


def _hyb_mix_specs(rows, tm):
    per = tm // SUBLANE
    nblk = rows // SUBLANE
    const = lambda i: (0, 0)
    return [pl.BlockSpec((tm, 512), lambda i: (i, 0))] * 5 + [
        pl.BlockSpec((SUBLANE, 512), lambda i: (jnp.maximum(i * per - 1, 0), 0)),
        pl.BlockSpec((SUBLANE, 512), lambda i: (jnp.minimum((i + 1) * per, nblk - 1), 0)),
        pl.BlockSpec((1, 512), const),
        pl.BlockSpec((SUBLANE, 512), const)]


CONST_LANE = HEAD_DIM
V_ROWS = LANE
LOG2E = 1.4426950408889634
BOUND_MARGIN = 1.02
MAX_BOUND_LOG2 = 48.0


def _att_in_body(x_ref, mod_ref, g_ref, w_ref, qg_ref, kg_ref, cos_ref, sin_ref, bd_ref, pads_ref,
                 q_ref, k_ref, vt_ref):
    d = D_MODEL
    h = _normmod(x_ref[...], g_ref[...], mod_ref[0, :, 0:d], mod_ref[0, :, d:2 * d]).astype(BF16)
    cosv = cos_ref[...]
    sinv = sin_ref[...]
    bd = bd_ref[...]
    lane = lax.broadcasted_iota(jnp.int32, cosv.shape, 1)
    upper_half = ((lane // (HEAD_DIM // 4)) % 2) == 1
    first_head = lane < HEAD_DIM

    def norm_rope(p, ssq, gain, scale, pad, out_ref, c_out):
        pn = p * (lax.rsqrt(ssq * (1.0 / HEAD_DIM) + EPS) * scale) * gain
        for half in range(2):
            xh = pn[:, half * LANE:(half + 1) * LANE]
            partner = jnp.where(upper_half, pltpu.roll(xh, HEAD_DIM // 4, axis=1),
                                pltpu.roll(xh, LANE - HEAD_DIM // 4, axis=1))
            y = xh * cosv + partner * sinv
            c0 = c_out + 2 * half * LANE
            out_ref[:, c0:c0 + LANE] = jnp.where(first_head, y, pad).astype(BF16)
            out_ref[:, c0 + LANE:c0 + 2 * LANE] = jnp.where(first_head, pltpu.roll(y, HEAD_DIM, axis=1), pad).astype(BF16)

    nq, nk = ATT_HEADS * HEAD_DIM, KV_HEADS * HEAD_DIM
    q_scale = HEAD_DIM ** -0.5 * LOG2E
    n_blk = (nq + nk) // 256
    ps = [_dot(h, w_ref[:, j * 256:(j + 1) * 256]) for j in range(n_blk)]
    v = _dot(h, w_ref[:, nq + nk:nq + 2 * nk])
    ssqs = [_dot((p * p).astype(BF16), bd) for p in ps]
    for j in range(nq // 256):
        norm_rope(ps[j], ssqs[j], qg_ref[...], q_scale, pads_ref[0:1, :], q_ref, 2 * j * 256)
    norm_rope(ps[-1], ssqs[-1], kg_ref[...], 1.0, pads_ref[1:2, :], k_ref, 0)
    vt = v.T.astype(BF16)
    tail_row = lax.broadcasted_iota(jnp.int32, (V_ROWS - HEAD_DIM, vt.shape[1]), 0)
    tail = jnp.where(tail_row == 0, 1.0, 0.0).astype(BF16)
    for hh in range(KV_HEADS):
        r0 = hh * V_ROWS
        vt_ref[0, r0:r0 + HEAD_DIM, :] = vt[hh * HEAD_DIM:(hh + 1) * HEAD_DIM, :]
        vt_ref[0, r0 + HEAD_DIM:r0 + V_ROWS, :] = tail


def _att_in(stream, mod, g1, w, qg, kg, cos_t, sin_t, bd, pads, n_xt):
    rows, d = stream.shape
    tm = TOKEN_TILE
    nt = rows // tm
    ncol = w.shape[1]
    row = lambda i: (i, 0)
    const = lambda i: (0, 0)
    tab = lambda i: (jnp.where(i == nt - 1, n_xt, i % n_xt), 0)
    return pl.pallas_call(
        _att_in_body,
        grid=(nt,),
        in_specs=[pl.BlockSpec((tm, d), row),
                  pl.BlockSpec((1, 1, 6 * d), lambda i: (i // n_xt, 0, 0)),
                  pl.BlockSpec((1, d), const),
                  pl.BlockSpec((d, ncol), const),
                  pl.BlockSpec((1, 256), const),
                  pl.BlockSpec((1, 256), const),
                  pl.BlockSpec((tm, LANE), tab),
                  pl.BlockSpec((tm, LANE), tab),
                  pl.BlockSpec((256, 256), const),
                  pl.BlockSpec((SUBLANE, LANE), const)],
        out_specs=[pl.BlockSpec((tm, ATT_Q_COLS), row),
                   pl.BlockSpec((tm, ATT_K_COLS), row),
                   pl.BlockSpec((1, KV_HEADS * V_ROWS, tm), lambda i: (i, 0, 0))],
        out_shape=[jax.ShapeDtypeStruct((rows, ATT_Q_COLS), BF16),
                   jax.ShapeDtypeStruct((rows, ATT_K_COLS), BF16),
                   jax.ShapeDtypeStruct((nt, KV_HEADS * V_ROWS, tm), BF16)],
        compiler_params=_cparams("arbitrary"),
        name="att_in",
    )(stream, mod, g1, w, qg, kg, cos_t, sin_t, bd, pads)


def _store_heads(o_ref, accs):
    outs = [a[:HEAD_DIM, :] * (1.0 / a[CONST_LANE:CONST_LANE + 1, :]) for a in accs]
    for pair in range(GROUP // 2):
        both = jnp.concatenate(outs[2 * pair:2 * pair + 2], axis=0)
        o_ref[:, pair * LANE:(pair + 1) * LANE] = both.T.astype(BF16)


def _flash_safe_body(q_ref, kc_ref, vtc_ref, kx_ref, vtx_ref, o_ref):
    is_latent = pl.program_id(2) < pl.num_programs(2) - 1
    tq = q_ref.shape[0]
    qs = [q_ref[:, g * LANE:(g + 1) * LANE] for g in range(GROUP)]

    def step(kt, vt, carry):
        new = []
        for g in range(GROUP):
            m, acc = carry[g]
            st = _dot_nt(kt, qs[g])
            m_new = jnp.maximum(m, jnp.max(st, axis=0, keepdims=True))
            p = jnp.exp2(st - m_new)
            acc_new = jnp.exp2(m - m_new) * acc + _dot(vt, p.astype(BF16))
            new.append((m_new, acc_new))
        return tuple(new)

    init = tuple((jnp.full((1, tq), -jnp.inf, F32), jnp.zeros((V_ROWS, tq), F32)) for _ in range(GROUP))
    carry = step(kc_ref[...], vtc_ref[0], init)
    carry = lax.fori_loop(
        0, jnp.where(is_latent, vtx_ref.shape[0], 0),
        lambda j, cr: step(kx_ref[pl.ds(pl.multiple_of(j * KV_TILE, KV_TILE), KV_TILE), :], vtx_ref[j], cr),
        carry)
    _store_heads(o_ref, [acc for _, acc in carry])


def _flash_bounded_body(q_ref, kc_ref, vtc_ref, kx_ref, vtx_ref, o_ref, qa_ref, acc_ref):
    tq = q_ref.shape[0]
    for g in range(GROUP):
        qa_ref[g * tq:(g + 1) * tq, :] = q_ref[:, g * LANE:(g + 1) * LANE]

    def chunk(kt, vt):
        st = _dot_nt(kt, qa_ref[...])
        return _dot(vt, jnp.exp2(st).astype(BF16))

    is_latent = pl.program_id(2) < pl.num_programs(2) - 1

    @pl.when(is_latent)
    def _():
        ck = FLASH_CHUNK
        per_tile = KV_TILE // ck
        n = vtx_ref.shape[0] * per_tile
        scores = lambda c: _dot_nt(kx_ref[c * ck:(c + 1) * ck, :], qa_ref[...])
        values_t = lambda c: vtx_ref[c // per_tile, :, (c % per_tile) * ck:(c % per_tile + 1) * ck]
        st = _dot_nt(kc_ref[...], qa_ref[...])
        for c in range(n + 1):
            st_next = scores(c) if c < n else None
            pv = _dot(vtc_ref[0] if c == 0 else values_t(c - 1), jnp.exp2(st).astype(BF16))
            if c == 0:
                acc_ref[...] = pv
            else:
                acc_ref[...] += pv
            st = st_next

    @pl.when(jnp.logical_not(is_latent))
    def _():
        acc_ref[...] = chunk(kc_ref[...], vtc_ref[0])

    _store_heads(o_ref, [acc_ref[:, g * tq:(g + 1) * tq] for g in range(GROUP)])


def _flash(q, k, vt, nb, t, lc, bounded):
    rows = q.shape[0]
    tq = SEQ_BLOCK
    n_xt = t // KV_TILE
    nq = t // tq
    ctx_q0 = nb * nq
    ctx_tile = nb * t // KV_TILE
    qmap = lambda b, h, i: (jnp.where(i == nq, ctx_q0 + b, b * nq + i), h)
    in_specs = [pl.BlockSpec((tq, GROUP * LANE), qmap),
                pl.BlockSpec((lc, LANE), lambda b, h, i: (nb * t // lc + b, h)),
                pl.BlockSpec((1, V_ROWS, lc), lambda b, h, i: (ctx_tile, h, b)),
                pl.BlockSpec((t, LANE), lambda b, h, i: (b, h)),
                pl.BlockSpec((n_xt, V_ROWS, KV_TILE), lambda b, h, i: (b, h, 0))]
    if bounded:
        body = _flash_bounded_body
        scratch = [pltpu.VMEM((GROUP * tq, LANE), BF16), pltpu.VMEM((V_ROWS, GROUP * tq), F32)]
    else:
        body = _flash_safe_body
        scratch = []
    return pl.pallas_call(
        body,
        grid=(nb, KV_HEADS, nq + 1),
        in_specs=in_specs,
        out_specs=pl.BlockSpec((tq, GROUP * HEAD_DIM), qmap),
        out_shape=jax.ShapeDtypeStruct((rows, ATT_HEADS * HEAD_DIM), BF16),
        scratch_shapes=scratch,
        compiler_params=_cparams("arbitrary", "arbitrary", "arbitrary"),
        name="flash_bounded" if bounded else "flash_safe",
    )(q, k, vt, k, vt)


def _outproj_mlp_body(*refs, n_src, seq_edges):
    mod_ref, g_ref, wo_ref, w1_ref, w2_ref, out_ref = refs[-6:]
    mix_refs = refs[n_src:-6]
    y = mix_refs[0][...] if seq_edges is None else _hyb_mix(*mix_refs, *seq_edges)
    d = D_MODEL
    mod = lambda k: mod_ref[0, :, k * d:(k + 1) * d]
    x1 = _src_rows(refs[:n_src]) + mod(2) * _dot(y, wo_ref[...])
    h2 = _normmod(x1, g_ref[...], mod(3), mod(4)).astype(BF16)
    acc = jnp.zeros_like(x1)
    for j in range(MLP_HIDDEN // d):
        u = jnp.maximum(_dot(h2, w1_ref[:, j * d:(j + 1) * d]), 0.0)
        acc = acc + _dot((u * u).astype(BF16), w2_ref[j * d:(j + 1) * d, :])
    out_ref[...] = x1 + mod(5) * acc


def _outproj_mlp(mix, src, mod, g2, wo, w1, w2, n_tiles, n_xt, seq_edges=None):
    d = D_MODEL
    tm = TOKEN_TILE
    row = lambda i: (i, 0)
    const = lambda i: (0, 0)
    mix_specs = [pl.BlockSpec((tm, d), row)] if seq_edges is None else _hyb_mix_specs(mix[0].shape[0], tm)
    return pl.pallas_call(
        functools.partial(_outproj_mlp_body, n_src=len(src), seq_edges=seq_edges),
        grid=(n_tiles,),
        in_specs=_src_specs(src, tm) + mix_specs + [
                  pl.BlockSpec((1, 1, 6 * d), lambda i: (i // n_xt, 0, 0)),
                  pl.BlockSpec((1, d), const),
                  pl.BlockSpec((d, d), const),
                  pl.BlockSpec((d, MLP_HIDDEN), const),
                  pl.BlockSpec((MLP_HIDDEN, d), const)],
        out_specs=pl.BlockSpec((tm, d), row),
        out_shape=jax.ShapeDtypeStruct((n_tiles * tm, d), F32),
        compiler_params=_cparams("arbitrary"),
        name="outproj_mlp",
    )(*src, *mix, mod, g2, wo, w1, w2)


def _hyb_weights(w_in, gate_b):
    n_gate = 4 * ML_HEADS
    g0 = 2 * ML_HEADS * ML_DK + 2 * ML_HEADS * ML_DV
    w = jnp.concatenate([w_in[:, :g0], jnp.pad(w_in[:, g0:g0 + n_gate], ((0, 0), (0, LANE - n_gate))),
                         w_in[:, g0 + n_gate:]], axis=1).astype(BF16)
    gb = jnp.pad(gate_b.astype(F32), (0, LANE - n_gate)).reshape(1, LANE)
    return w, gb


def _rope_tables(t, n_ident):
    half = HEAD_DIM // 2
    inv = ROPE_THETA ** (-jnp.arange(0, half, 2, dtype=F32) / half)
    n_rows = t // GRID_W
    r = jnp.arange(n_rows, dtype=F32)[:, None] * inv
    c = jnp.arange(GRID_W, dtype=F32)[:, None] * inv
    z_r, z_c = jnp.zeros_like(r), jnp.zeros_like(c)
    lanes = lambda q0, q1, q2, q3: jnp.concatenate([q0, q1, q2, q3] * 2, axis=-1)
    outer = lambda by_row, by_col: (by_row[:, None, :] + by_col[None, :, :]).reshape(t, LANE)
    cos_t = outer(lanes(jnp.cos(r), jnp.cos(r), z_r, z_r), lanes(z_c, z_c, jnp.cos(c), jnp.cos(c)))
    sin_t = outer(lanes(-jnp.sin(r), jnp.sin(r), z_r, z_r), lanes(z_c, z_c, -jnp.sin(c), jnp.sin(c)))
    cos_t = jnp.concatenate([cos_t, jnp.ones((n_ident, LANE), F32)], axis=0)
    sin_t = jnp.concatenate([sin_t, jnp.zeros((n_ident, LANE), F32)], axis=0)
    return cos_t, sin_t


def kernel(x, c, ctx, c_ctx, ada_w, ada_b, norm1_g, norm2_g, mlp_w1, mlp_w2, hyb_w_in, hyb_gate_b, mlstm_norm_g,
           conv_w, hyb_w_out, att_w_in, q_norm_g, k_norm_g, att_w_out):
    nb, t, d = x.shape
    lc = ctx.shape[1]
    assert d == D_MODEL and nb * lc == TOKEN_TILE and lc == SEQ_BLOCK and t % TOKEN_TILE == 0
    assert nb + 1 <= SUBLANE
    n_xt = t // TOKEN_TILE
    n_tiles = nb * n_xt + 1

    src = (x.reshape(nb * t, d).astype(F32), ctx.reshape(nb * lc, d).astype(F32))
    cvec = jnp.concatenate([c, c_ctx[None, :], jnp.zeros((SUBLANE - nb - 1, d), c.dtype)], axis=0).astype(F32)
    mods = _adaln(cvec, ada_w.astype(F32), ada_b.astype(F32))

    cos_t, sin_t = _rope_tables(t, TOKEN_TILE)
    head_of = lambda axis: lax.broadcasted_iota(jnp.int32, (256, 256), axis) // HEAD_DIM
    bd = (head_of(0) == head_of(1)).astype(BF16)
    pad_gain = lambda g: jnp.tile(g.astype(F32), 256 // HEAD_DIM).reshape(1, 256)
    seq_starts = tuple(b * t for b in range(nb)) + tuple(nb * t + b * lc for b in range(nb))
    seq_ends = tuple((b + 1) * t - 1 for b in range(nb)) + tuple(nb * t + (b + 1) * lc - 1 for b in range(nb))

    for layer in range(DEPTH):
        last = layer == DEPTH - 1
        mod = mods[layer, :nb + 1].reshape(nb + 1, 1, 6 * d)
        g1 = norm1_g[layer].astype(F32).reshape(1, d)
        g2 = norm2_g[layer].astype(F32).reshape(1, d)
        if layer % 2 == 0:
            e = layer // 2
            w, gb = _hyb_weights(hyb_w_in[e], hyb_gate_b[e])
            q, k, vt, o, gate, cb, z = _hyb_in(src, mod, g1, w, gb, n_xt)
            hf, hb = _mlstm(q, k, vt, gate, nb, t)
            cw = jnp.pad(conv_w[e].astype(F32), ((0, SUBLANE - CONV_K), (0, 0)))
            mix = (hf, hb, o, cb, z, z, z, mlstm_norm_g[e].astype(F32).reshape(1, -1), cw)
            seq_edges = (seq_starts, seq_ends)
            wo = hyb_w_out[e].astype(BF16)
        else:
            a = layer // 2
            bound = (HEAD_DIM ** 0.5 * LOG2E * BOUND_MARGIN) * jnp.max(jnp.abs(q_norm_g[a])) * jnp.max(jnp.abs(k_norm_g[a]))
            bound = bound.astype(F32)
            lane_const = (jnp.arange(LANE) == CONST_LANE).astype(F32)
            pads = jnp.stack([-bound * lane_const, lane_const] + [0.0 * lane_const] * (SUBLANE - 2))
            q, k, vt = _att_in(src[0], mod, g1, att_w_in[a].astype(BF16), pad_gain(q_norm_g[a]),
                               pad_gain(k_norm_g[a]), cos_t, sin_t, bd, pads, n_xt)
            y = lax.cond(bound <= MAX_BOUND_LOG2,
                         lambda: _flash(q, k, vt, nb, t, lc, bounded=True),
                         lambda: _flash(q, k, vt, nb, t, lc, bounded=False))
            mix = (y,)
            seq_edges = None
            wo = att_w_out[a].astype(BF16)
        src = (_outproj_mlp(mix, src, mod, g2, wo, mlp_w1[layer].astype(BF16), mlp_w2[layer].astype(BF16),
                            nb * n_xt if last else n_tiles, n_xt, seq_edges),)
    return src[0].reshape(nb, t, d).astype(x.dtype)
```

```python
import functools

import jax
import jax.numpy as jnp
from jax import lax
from jax.experimental import pallas as pl
from jax.experimental.pallas import tpu as pltpu

F32 = jnp.float32
BF16 = jnp.bfloat16

D_MODEL = 1024
DEPTH = 4
GRID_W = 64
EPS = 1e-6
ML_HEADS = 4
ML_DK = 64
ML_DV = 128
ML_CHUNK = 64
SC_WIDTH = D_MODEL // 2
CONV_K = 3
ATT_HEADS = 16
KV_HEADS = 4
HEAD_DIM = 64
GROUP = ATT_HEADS // KV_HEADS
ROPE_THETA = 10000.0
MLP_HIDDEN = 4 * D_MODEL

LANE = 128
SUBLANE = 8
TOKEN_TILE = 512
SEQ_BLOCK = 256
SCAN_CHUNK = SEQ_BLOCK
KV_TILE = TOKEN_TILE
FLASH_CHUNK = 512
VMEM_LIMIT_BYTES = 56 * 1024 * 1024

ATT_Q_COLS = ATT_HEADS * LANE
ATT_K_COLS = KV_HEADS * LANE


def _cparams(*sem):
    return pltpu.CompilerParams(dimension_semantics=sem, vmem_limit_bytes=VMEM_LIMIT_BYTES)


def _normmod(xf, g, shift, scale):
    var = jnp.mean(xf * xf, axis=-1, keepdims=True)
    return xf * lax.rsqrt(var + EPS) * g * (1.0 + scale) + shift


def _src_specs(src, tm):
    d = src[0].shape[1]
    if len(src) == 1:
        return [pl.BlockSpec((tm, d), lambda i: (i, 0))]
    n_main = src[0].shape[0] // tm
    assert src[1].shape[0] == tm
    return [pl.BlockSpec((tm, d), lambda i: (jnp.minimum(i, n_main - 1), 0)), pl.BlockSpec((tm, d), lambda i: (0, 0))]


def _src_rows(src_refs):
    if len(src_refs) == 1:
        return src_refs[0][...]
    on_ctx_tile = pl.program_id(0) == pl.num_programs(0) - 1
    return jnp.where(on_ctx_tile, src_refs[1][...], src_refs[0][...])


def _dot(a, b):
    return jnp.dot(a, b, preferred_element_type=F32)


def _dot_nt(a, b):
    return lax.dot_general(a, b, (((1,), (1,)), ((), ())), preferred_element_type=F32)


def _adaln_body(c_ref, w_ref, b_ref, o_ref):
    cv = c_ref[...]
    s = cv * jax.nn.sigmoid(cv)
    o_ref[0] = _dot(s.astype(BF16), w_ref[0].astype(BF16)) + b_ref[0]


def _adaln(cvec, ada_w, ada_b):
    depth, d, n = ada_w.shape
    bn = n // 4
    return pl.pallas_call(
        _adaln_body,
        grid=(depth, n // bn),
        in_specs=[pl.BlockSpec((SUBLANE, d), lambda l, j: (0, 0)),
                  pl.BlockSpec((1, d, bn), lambda l, j: (l, 0, j)),
                  pl.BlockSpec((1, 1, bn), lambda l, j: (l, 0, j))],
        out_specs=pl.BlockSpec((1, SUBLANE, bn), lambda l, j: (l, 0, j)),
        out_shape=jax.ShapeDtypeStruct((depth, SUBLANE, n), F32),
        compiler_params=_cparams("arbitrary", "arbitrary"),
        name="adaln",
    )(cvec, ada_w, ada_b.reshape(depth, 1, n))


def _hyb_in_body(*refs, n_src):
    mod_ref, g_ref, w_ref, gb_ref, q_ref, k_ref, vt_ref, o_ref, gate_ref, cb_ref, z_ref = refs[n_src:]
    d = D_MODEL
    h = _normmod(_src_rows(refs[:n_src]), g_ref[...], mod_ref[0, :, 0:d], mod_ref[0, :, d:2 * d]).astype(BF16)
    c0 = 0
    qk = ML_HEADS * ML_DK
    q_ref[...] = (_dot(h, w_ref[:, c0:c0 + qk]) * (ML_DK ** -0.5)).astype(BF16)
    c0 += qk
    k_ref[...] = _dot(h, w_ref[:, c0:c0 + qk]).astype(BF16)
    c0 += qk
    vt_ref[0] = _dot(h, w_ref[:, c0:c0 + 512]).T
    c0 += 512
    o_ref[...] = _dot(h, w_ref[:, c0:c0 + 512])
    c0 += 512
    g = _dot(h, w_ref[:, c0:c0 + LANE]) + gb_ref[...]
    c0 += LANE
    lane = lax.broadcasted_iota(jnp.int32, g.shape, 1)
    is_forget = ((lane // ML_HEADS) % 2) == 1
    logsig = jnp.minimum(g, 0.0) - jnp.log1p(jnp.exp(-jnp.abs(g)))
    gate_ref[...] = jnp.where(is_forget, logsig, g)
    cb_ref[...] = _dot(h, w_ref[:, c0:c0 + 512])
    c0 += 512
    gc = _dot(h, w_ref[:, c0:c0 + 512])
    c0 += 512
    z_ref[...] = gc * _dot(h, w_ref[:, c0:c0 + 512])


def _hyb_in(src, mod, g1, w, gate_b, n_xt):
    rows = sum(s.shape[0] for s in src)
    d = D_MODEL
    tm = TOKEN_TILE
    ncol = w.shape[1]
    row = lambda i: (i, 0)
    const = lambda i: (0, 0)
    qk = ML_HEADS * ML_DK
    outs = [(qk, BF16), (qk, BF16), None, (512, F32), (LANE, F32), (512, F32), (512, F32)]
    nt = rows // tm
    vt_cols = ML_HEADS * ML_DV
    return pl.pallas_call(
        functools.partial(_hyb_in_body, n_src=len(src)),
        grid=(nt,),
        in_specs=_src_specs(src, tm) + [
                  pl.BlockSpec((1, 1, 6 * d), lambda i: (i // n_xt, 0, 0)),
                  pl.BlockSpec((1, d), const),
                  pl.BlockSpec((d, ncol), const),
                  pl.BlockSpec((1, LANE), const)],
        out_specs=[pl.BlockSpec((tm, o[0]), row) if o else pl.BlockSpec((1, vt_cols, tm), lambda i: (i, 0, 0))
                   for o in outs],
        out_shape=[jax.ShapeDtypeStruct((rows, o[0]), o[1]) if o else jax.ShapeDtypeStruct((nt, vt_cols, tm), F32)
                   for o in outs],
        compiler_params=_cparams("arbitrary"),
        name="hyb_in",
    )(*src, mod, g1, w, gate_b)


def _split3(a):
    hi = a.astype(BF16)
    r = a - hi.astype(F32)
    mid = r.astype(BF16)
    lo = (r - mid.astype(F32)).astype(BF16)
    return hi, mid, lo


def _mlstm_body(qf_ref, kf_ref, vtf_ref, gf_ref, qb_ref, kb_ref, vtb_ref, gb_ref, hf_ref, hb_ref, c_ref, m_ref):
    L = SCAN_CHUNK

    @pl.when(pl.program_id(1) == 0)
    def _():
        c_ref[...] = jnp.zeros_like(c_ref)
        m_ref[...] = jnp.zeros_like(m_ref)

    s_idx = lax.broadcasted_iota(jnp.int32, (L, L), 0)
    j_idx = lax.broadcasted_iota(jnp.int32, (L, L), 1)
    diag = (s_idx == j_idx).astype(BF16)
    ones_rows = jnp.ones((ML_DV, L), F32)
    pair_lane = lax.broadcasted_iota(jnp.int32, (1, LANE), 1)

    chains = []
    for d, (q_ref, k_ref, vt_ref, gate_ref, h_ref) in enumerate(((qf_ref, kf_ref, vtf_ref, gf_ref, hf_ref),
                                                                 (qb_ref, kb_ref, vtb_ref, gb_ref, hb_ref))):
        rev = d == 1
        visible = (s_idx >= j_idx) if rev else (s_idx <= j_idx)
        tr = visible.astype(BF16)
        tc = jnp.logical_not(visible).astype(BF16) + diag
        lane0 = 2 * ML_HEADS if rev else 0
        last = 0 if rev else L - 1
        G = gate_ref[...]
        GT = G.T
        bc = sum(_dot(tc, p) for p in _split3(G))
        br = sum(_dot(p, tr) for p in _split3(GT))
        ig_sh = pltpu.roll(G, ML_HEADS, axis=1)
        b_end = bc[last:last + 1, :]
        m0 = m_ref[d, 0:1, :]
        src_col = ig_sh - bc
        m_loc = jnp.max(b_end + src_col, axis=0, keepdims=True)
        m_new = jnp.maximum(b_end + m0, m_loc)
        a2_all = jnp.exp(b_end + m0 - m_new)
        g2_all = jnp.exp(m_loc - m_new)
        m_ref[d, 0:1, :] = m_new
        for h in range(ML_HEADS):
            fl = lane0 + ML_HEADS + h
            cs = slice(h * LANE, (h + 1) * LANE)
            pair = slice((h // 2) * LANE, (h // 2 + 1) * LANE)
            own = (pair_lane // ML_DK) == (h % 2)
            brow = br[fl:fl + 1, :]
            chains.append(dict(
                d=d, h=h, cs=cs, h_ref=h_ref, visible=visible, brow=brow,
                q=q_ref[:, pair], k=jnp.where(own, k_ref[:, pair], jnp.zeros((), BF16)),
                vaug=jnp.concatenate([vt_ref[0, cs, :], ones_rows], axis=0),
                c_old=c_ref[d, h],
                w_row=jnp.exp(b_end[:, fl:fl + 1] + GT[fl - ML_HEADS:fl - ML_HEADS + 1, :] - brow - m_loc[:, fl:fl + 1]),
                src=src_col[:, fl:fl + 1], m_inter=brow + m0[:, fl:fl + 1],
                a2=a2_all[:, fl:fl + 1], g2=g2_all[:, fl:fl + 1]))

    for c in chains:
        c["kq"] = _dot_nt(c["k"], c["q"])
        c["inter"] = _dot_nt(c["c_old"].astype(BF16), c["q"])
        c["c_loc"] = _dot((c["vaug"] * c["w_row"]).astype(BF16), c["k"])
    for c in chains:
        logd = jnp.where(c["visible"], c["brow"] + c["src"], -jnp.inf)
        c["m"] = jnp.maximum(c["m_inter"], jnp.max(logd, axis=0, keepdims=True))
        c["s_t"] = (c["kq"] * jnp.exp(logd - c["m"])).astype(BF16)
    for c in chains:
        num_t = _dot(c["vaug"].astype(BF16), c["s_t"]) + jnp.exp(c["m_inter"] - c["m"]) * c["inter"]
        h_t = num_t[:ML_DV, :] / jnp.maximum(jnp.abs(num_t[ML_DV:, :]), jnp.exp(-c["m"]))
        c["h_ref"][:, c["cs"]] = h_t.T
        c_ref[c["d"], c["h"]] = c["a2"] * c["c_old"] + c["g2"] * c["c_loc"]


def _mlstm(q, k, vt, gate, nb, t):
    rows = q.shape[0]
    blk = SEQ_BLOCK
    per_tile = TOKEN_TILE // blk
    nxb = t // blk
    ctx_blk0 = nb * nxb

    def specs(rev):
        def block(b, i):
            xi = (nxb - i) if rev else (i - 1)
            return jnp.where(i == 0, ctx_blk0 + b, b * nxb + xi)
        spec = lambda c: pl.BlockSpec((blk, c), lambda b, i: (block(b, i), 0))
        vt_spec = pl.BlockSpec((1, ML_HEADS * ML_DV, blk),
                               lambda b, i: (block(b, i) // per_tile, 0, block(b, i) % per_tile))
        qk = ML_HEADS * ML_DK
        return [spec(qk), spec(qk), vt_spec, spec(LANE)], spec(512)

    (in_f, out_f), (in_b, out_b) = specs(False), specs(True)
    return pl.pallas_call(
        _mlstm_body,
        grid=(nb, nxb + 1),
        in_specs=in_f + in_b,
        out_specs=[out_f, out_b],
        out_shape=[jax.ShapeDtypeStruct((rows, 512), F32)] * 2,
        scratch_shapes=[pltpu.VMEM((2, ML_HEADS, 2 * ML_DV, LANE), F32), pltpu.VMEM((2, SUBLANE, LANE), F32)],
        compiler_params=_cparams("arbitrary", "arbitrary"),
        name="mlstm",
    )(q, k, vt, gate, q, k, vt, gate)


def _hyb_mix(hf_ref, hb_ref, o_ref, cb_ref, z_ref, zp_ref, zn_ref, mg_ref, cw_ref, starts, ends):
    tm = hf_ref.shape[0]
    i = pl.program_id(0)
    parts = []
    for h in range(ML_HEADS):
        cs = slice(h * ML_DV, (h + 1) * ML_DV)
        blk = hf_ref[:, cs] + hb_ref[:, cs]
        var = jnp.mean(blk * blk, axis=-1, keepdims=True)
        hn = blk * lax.rsqrt(var + EPS) * mg_ref[:, cs]
        parts.append((hn * jax.nn.sigmoid(o_ref[:, cs])).astype(BF16))
    z = z_ref[...]
    loc = lax.broadcasted_iota(jnp.int32, (tm, 1), 0)
    row = loc + i * tm
    is_start = functools.reduce(jnp.logical_or, [row == r for r in starts])
    is_end = functools.reduce(jnp.logical_or, [row == r for r in ends])
    zprev = jnp.where(loc == 0, zp_ref[SUBLANE - 1:SUBLANE, :], pltpu.roll(z, 1, axis=0))
    zprev = jnp.where(is_start, 0.0, zprev)
    znext = jnp.where(loc == tm - 1, zn_ref[0:1, :], pltpu.roll(z, tm - 1, axis=0))
    znext = jnp.where(is_end, 0.0, znext)
    conv = cw_ref[0:1, :] * zprev + cw_ref[1:2, :] * z + cw_ref[2:3, :] * znext
    parts.append((cb_ref[...] * conv).astype(BF16))
    return jnp.concatenate(parts, axis=1)


def _hyb_mix_specs(rows, tm):
    per = tm // SUBLANE
    nblk = rows // SUBLANE
    const = lambda i: (0, 0)
    return [pl.BlockSpec((tm, 512), lambda i: (i, 0))] * 5 + [
        pl.BlockSpec((SUBLANE, 512), lambda i: (jnp.maximum(i * per - 1, 0), 0)),
        pl.BlockSpec((SUBLANE, 512), lambda i: (jnp.minimum((i + 1) * per, nblk - 1), 0)),
        pl.BlockSpec((1, 512), const),
        pl.BlockSpec((SUBLANE, 512), const)]


CONST_LANE = HEAD_DIM
V_ROWS = LANE
LOG2E = 1.4426950408889634
BOUND_MARGIN = 1.02
MAX_BOUND_LOG2 = 48.0


def _att_in_body(x_ref, mod_ref, g_ref, w_ref, qg_ref, kg_ref, cos_ref, sin_ref, bd_ref, pads_ref,
                 q_ref, k_ref, vt_ref):
    d = D_MODEL
    h = _normmod(x_ref[...], g_ref[...], mod_ref[0, :, 0:d], mod_ref[0, :, d:2 * d]).astype(BF16)
    cosv = cos_ref[...]
    sinv = sin_ref[...]
    bd = bd_ref[...]
    lane = lax.broadcasted_iota(jnp.int32, cosv.shape, 1)
    upper_half = ((lane // (HEAD_DIM // 4)) % 2) == 1
    first_head = lane < HEAD_DIM

    def norm_rope(p, ssq, gain, scale, pad, out_ref, c_out):
        pn = p * (lax.rsqrt(ssq * (1.0 / HEAD_DIM) + EPS) * scale) * gain
        for half in range(2):
            xh = pn[:, half * LANE:(half + 1) * LANE]
            partner = jnp.where(upper_half, pltpu.roll(xh, HEAD_DIM // 4, axis=1),
                                pltpu.roll(xh, LANE - HEAD_DIM // 4, axis=1))
            y = xh * cosv + partner * sinv
            c0 = c_out + 2 * half * LANE
            out_ref[:, c0:c0 + LANE] = jnp.where(first_head, y, pad).astype(BF16)
            out_ref[:, c0 + LANE:c0 + 2 * LANE] = jnp.where(first_head, pltpu.roll(y, HEAD_DIM, axis=1), pad).astype(BF16)

    nq, nk = ATT_HEADS * HEAD_DIM, KV_HEADS * HEAD_DIM
    q_scale = HEAD_DIM ** -0.5 * LOG2E
    n_blk = (nq + nk) // 256
    ps = [_dot(h, w_ref[:, j * 256:(j + 1) * 256]) for j in range(n_blk)]
    v = _dot(h, w_ref[:, nq + nk:nq + 2 * nk])
    ssqs = [_dot((p * p).astype(BF16), bd) for p in ps]
    for j in range(nq // 256):
        norm_rope(ps[j], ssqs[j], qg_ref[...], q_scale, pads_ref[0:1, :], q_ref, 2 * j * 256)
    norm_rope(ps[-1], ssqs[-1], kg_ref[...], 1.0, pads_ref[1:2, :], k_ref, 0)
    vt = v.T.astype(BF16)
    tail_row = lax.broadcasted_iota(jnp.int32, (V_ROWS - HEAD_DIM, vt.shape[1]), 0)
    tail = jnp.where(tail_row == 0, 1.0, 0.0).astype(BF16)
    for hh in range(KV_HEADS):
        r0 = hh * V_ROWS
        vt_ref[0, r0:r0 + HEAD_DIM, :] = vt[hh * HEAD_DIM:(hh + 1) * HEAD_DIM, :]
        vt_ref[0, r0 + HEAD_DIM:r0 + V_ROWS, :] = tail


def _att_in(stream, mod, g1, w, qg, kg, cos_t, sin_t, bd, pads, n_xt):
    rows, d = stream.shape
    tm = TOKEN_TILE
    nt = rows // tm
    ncol = w.shape[1]
    row = lambda i: (i, 0)
    const = lambda i: (0, 0)
    tab = lambda i: (jnp.where(i == nt - 1, n_xt, i % n_xt), 0)
    return pl.pallas_call(
        _att_in_body,
        grid=(nt,),
        in_specs=[pl.BlockSpec((tm, d), row),
                  pl.BlockSpec((1, 1, 6 * d), lambda i: (i // n_xt, 0, 0)),
                  pl.BlockSpec((1, d), const),
                  pl.BlockSpec((d, ncol), const),
                  pl.BlockSpec((1, 256), const),
                  pl.BlockSpec((1, 256), const),
                  pl.BlockSpec((tm, LANE), tab),
                  pl.BlockSpec((tm, LANE), tab),
                  pl.BlockSpec((256, 256), const),
                  pl.BlockSpec((SUBLANE, LANE), const)],
        out_specs=[pl.BlockSpec((tm, ATT_Q_COLS), row),
                   pl.BlockSpec((tm, ATT_K_COLS), row),
                   pl.BlockSpec((1, KV_HEADS * V_ROWS, tm), lambda i: (i, 0, 0))],
        out_shape=[jax.ShapeDtypeStruct((rows, ATT_Q_COLS), BF16),
                   jax.ShapeDtypeStruct((rows, ATT_K_COLS), BF16),
                   jax.ShapeDtypeStruct((nt, KV_HEADS * V_ROWS, tm), BF16)],
        compiler_params=_cparams("arbitrary"),
        name="att_in",
    )(stream, mod, g1, w, qg, kg, cos_t, sin_t, bd, pads)


def _store_heads(o_ref, accs):
    outs = [a[:HEAD_DIM, :] * (1.0 / a[CONST_LANE:CONST_LANE + 1, :]) for a in accs]
    for pair in range(GROUP // 2):
        both = jnp.concatenate(outs[2 * pair:2 * pair + 2], axis=0)
        o_ref[:, pair * LANE:(pair + 1) * LANE] = both.T.astype(BF16)


def _flash_safe_body(q_ref, kc_ref, vtc_ref, kx_ref, vtx_ref, o_ref):
    is_latent = pl.program_id(2) < pl.num_programs(2) - 1
    tq = q_ref.shape[0]
    qs = [q_ref[:, g * LANE:(g + 1) * LANE] for g in range(GROUP)]

    def step(kt, vt, carry):
        new = []
        for g in range(GROUP):
            m, acc = carry[g]
            st = _dot_nt(kt, qs[g])
            m_new = jnp.maximum(m, jnp.max(st, axis=0, keepdims=True))
            p = jnp.exp2(st - m_new)
            acc_new = jnp.exp2(m - m_new) * acc + _dot(vt, p.astype(BF16))
            new.append((m_new, acc_new))
        return tuple(new)

    init = tuple((jnp.full((1, tq), -jnp.inf, F32), jnp.zeros((V_ROWS, tq), F32)) for _ in range(GROUP))
    carry = step(kc_ref[...], vtc_ref[0], init)
    carry = lax.fori_loop(
        0, jnp.where(is_latent, vtx_ref.shape[0], 0),
        lambda j, cr: step(kx_ref[pl.ds(pl.multiple_of(j * KV_TILE, KV_TILE), KV_TILE), :], vtx_ref[j], cr),
        carry)
    _store_heads(o_ref, [acc for _, acc in carry])


def _flash_bounded_body(q_ref, kc_ref, vtc_ref, kx_ref, vtx_ref, o_ref, qa_ref, acc_ref):
    tq = q_ref.shape[0]
    for g in range(GROUP):
        qa_ref[g * tq:(g + 1) * tq, :] = q_ref[:, g * LANE:(g + 1) * LANE]

    def chunk(kt, vt):
        st = _dot_nt(kt, qa_ref[...])
        return _dot(vt, jnp.exp2(st).astype(BF16))

    is_latent = pl.program_id(2) < pl.num_programs(2) - 1

    @pl.when(is_latent)
    def _():
        ck = FLASH_CHUNK
        per_tile = KV_TILE // ck
        n = vtx_ref.shape[0] * per_tile
        scores = lambda c: _dot_nt(kx_ref[c * ck:(c + 1) * ck, :], qa_ref[...])
        values_t = lambda c: vtx_ref[c // per_tile, :, (c % per_tile) * ck:(c % per_tile + 1) * ck]
        st = _dot_nt(kc_ref[...], qa_ref[...])
        for c in range(n + 1):
            st_next = scores(c) if c < n else None
            pv = _dot(vtc_ref[0] if c == 0 else values_t(c - 1), jnp.exp2(st).astype(BF16))
            if c == 0:
                acc_ref[...] = pv
            else:
                acc_ref[...] += pv
            st = st_next

    @pl.when(jnp.logical_not(is_latent))
    def _():
        acc_ref[...] = chunk(kc_ref[...], vtc_ref[0])

    _store_heads(o_ref, [acc_ref[:, g * tq:(g + 1) * tq] for g in range(GROUP)])


def _flash(q, k, vt, nb, t, lc, bounded):
    rows = q.shape[0]
    tq = SEQ_BLOCK
    n_xt = t // KV_TILE
    nq = t // tq
    ctx_q0 = nb * nq
    ctx_tile = nb * t // KV_TILE
    qmap = lambda b, h, i: (jnp.where(i == nq, ctx_q0 + b, b * nq + i), h)
    in_specs = [pl.BlockSpec((tq, GROUP * LANE), qmap),
                pl.BlockSpec((lc, LANE), lambda b, h, i: (nb * t // lc + b, h)),
                pl.BlockSpec((1, V_ROWS, lc), lambda b, h, i: (ctx_tile, h, b)),
                pl.BlockSpec((t, LANE), lambda b, h, i: (b, h)),
                pl.BlockSpec((n_xt, V_ROWS, KV_TILE), lambda b, h, i: (b, h, 0))]
    if bounded:
        body = _flash_bounded_body
        scratch = [pltpu.VMEM((GROUP * tq, LANE), BF16), pltpu.VMEM((V_ROWS, GROUP * tq), F32)]
    else:
        body = _flash_safe_body
        scratch = []
    return pl.pallas_call(
        body,
        grid=(nb, KV_HEADS, nq + 1),
        in_specs=in_specs,
        out_specs=pl.BlockSpec((tq, GROUP * HEAD_DIM), qmap),
        out_shape=jax.ShapeDtypeStruct((rows, ATT_HEADS * HEAD_DIM), BF16),
        scratch_shapes=scratch,
        compiler_params=_cparams("arbitrary", "arbitrary", "arbitrary"),
        name="flash_bounded" if bounded else "flash_safe",
    )(q, k, vt, k, vt)


def _outproj_mlp_body(*refs, n_src, seq_edges):
    mod_ref, g_ref, wo_ref, w1_ref, w2_ref, out_ref = refs[-6:]
    mix_refs = refs[n_src:-6]
    y = mix_refs[0][...] if seq_edges is None else _hyb_mix(*mix_refs, *seq_edges)
    d = D_MODEL
    mod = lambda k: mod_ref[0, :, k * d:(k + 1) * d]
    x1 = _src_rows(refs[:n_src]) + mod(2) * _dot(y, wo_ref[...])
    h2 = _normmod(x1, g_ref[...], mod(3), mod(4)).astype(BF16)
    acc = jnp.zeros_like(x1)
    for j in range(MLP_HIDDEN // d):
        u = jnp.maximum(_dot(h2, w1_ref[0, :, j * d:(j + 1) * d]), 0.0)
        acc = acc + _dot((u * u).astype(BF16), w2_ref[0, j * d:(j + 1) * d, :])
    out_ref[...] = x1 + mod(5) * acc


def _outproj_mlp(mix, src, mod, g2, wo, w1, w2, layer, n_tiles, n_xt, seq_edges=None):
    d = D_MODEL
    tm = TOKEN_TILE
    row = lambda i: (i, 0)
    const = lambda i: (0, 0)
    mix_specs = [pl.BlockSpec((tm, d), row)] if seq_edges is None else _hyb_mix_specs(mix[0].shape[0], tm)
    return pl.pallas_call(
        functools.partial(_outproj_mlp_body, n_src=len(src), seq_edges=seq_edges),
        grid=(n_tiles,),
        in_specs=_src_specs(src, tm) + mix_specs + [
                  pl.BlockSpec((1, 1, 6 * d), lambda i: (i // n_xt, 0, 0)),
                  pl.BlockSpec((1, d), const),
                  pl.BlockSpec((d, d), const),
                  pl.BlockSpec((1, d, MLP_HIDDEN), lambda i: (layer, 0, 0)),
                  pl.BlockSpec((1, MLP_HIDDEN, d), lambda i: (layer, 0, 0))],
        out_specs=pl.BlockSpec((tm, d), row),
        out_shape=jax.ShapeDtypeStruct((n_tiles * tm, d), F32),
        compiler_params=_cparams("arbitrary"),
        name="outproj_mlp",
    )(*src, *mix, mod, g2, wo, w1, w2)


def _hyb_weights(w_in, gate_b):
    n_gate = 4 * ML_HEADS
    g0 = 2 * ML_HEADS * ML_DK + 2 * ML_HEADS * ML_DV
    w = jnp.concatenate([w_in[:, :g0], jnp.pad(w_in[:, g0:g0 + n_gate], ((0, 0), (0, LANE - n_gate))),
                         w_in[:, g0 + n_gate:]], axis=1).astype(BF16)
    gb = jnp.pad(gate_b.astype(F32), (0, LANE - n_gate)).reshape(1, LANE)
    return w, gb


def _rope_tables(t, n_ident):
    half = HEAD_DIM // 2
    inv = ROPE_THETA ** (-jnp.arange(0, half, 2, dtype=F32) / half)
    n_rows = t // GRID_W
    r = jnp.arange(n_rows, dtype=F32)[:, None] * inv
    c = jnp.arange(GRID_W, dtype=F32)[:, None] * inv
    z_r, z_c = jnp.zeros_like(r), jnp.zeros_like(c)
    lanes = lambda q0, q1, q2, q3: jnp.concatenate([q0, q1, q2, q3] * 2, axis=-1)
    outer = lambda by_row, by_col: (by_row[:, None, :] + by_col[None, :, :]).reshape(t, LANE)
    cos_t = outer(lanes(jnp.cos(r), jnp.cos(r), z_r, z_r), lanes(z_c, z_c, jnp.cos(c), jnp.cos(c)))
    sin_t = outer(lanes(-jnp.sin(r), jnp.sin(r), z_r, z_r), lanes(z_c, z_c, -jnp.sin(c), jnp.sin(c)))
    cos_t = jnp.concatenate([cos_t, jnp.ones((n_ident, LANE), F32)], axis=0)
    sin_t = jnp.concatenate([sin_t, jnp.zeros((n_ident, LANE), F32)], axis=0)
    return cos_t, sin_t


def kernel(x, c, ctx, c_ctx, ada_w, ada_b, norm1_g, norm2_g, mlp_w1, mlp_w2, hyb_w_in, hyb_gate_b, mlstm_norm_g,
           conv_w, hyb_w_out, att_w_in, q_norm_g, k_norm_g, att_w_out):
    nb, t, d = x.shape
    lc = ctx.shape[1]
    assert d == D_MODEL and nb * lc == TOKEN_TILE and lc == SEQ_BLOCK and t % TOKEN_TILE == 0
    assert nb + 1 <= SUBLANE
    n_xt = t // TOKEN_TILE
    n_tiles = nb * n_xt + 1

    src = (x.reshape(nb * t, d).astype(F32), ctx.reshape(nb * lc, d).astype(F32))
    cvec = jnp.concatenate([c, c_ctx[None, :], jnp.zeros((SUBLANE - nb - 1, d), c.dtype)], axis=0).astype(F32)
    mods = _adaln(cvec, ada_w.astype(F32), ada_b.astype(F32))

    w1_all, w2_all = mlp_w1.astype(BF16), mlp_w2.astype(BF16)
    cos_t, sin_t = _rope_tables(t, TOKEN_TILE)
    head_of = lambda axis: lax.broadcasted_iota(jnp.int32, (256, 256), axis) // HEAD_DIM
    bd = (head_of(0) == head_of(1)).astype(BF16)
    pad_gain = lambda g: jnp.tile(g.astype(F32), 256 // HEAD_DIM).reshape(1, 256)
    seq_starts = tuple(b * t for b in range(nb)) + tuple(nb * t + b * lc for b in range(nb))
    seq_ends = tuple((b + 1) * t - 1 for b in range(nb)) + tuple(nb * t + (b + 1) * lc - 1 for b in range(nb))

    for layer in range(DEPTH):
        last = layer == DEPTH - 1
        mod = mods[layer, :nb + 1].reshape(nb + 1, 1, 6 * d)
        g1 = norm1_g[layer].astype(F32).reshape(1, d)
        g2 = norm2_g[layer].astype(F32).reshape(1, d)
        if layer % 2 == 0:
            e = layer // 2
            w, gb = _hyb_weights(hyb_w_in[e], hyb_gate_b[e])
            q, k, vt, o, gate, cb, z = _hyb_in(src, mod, g1, w, gb, n_xt)
            hf, hb = _mlstm(q, k, vt, gate, nb, t)
            cw = jnp.pad(conv_w[e].astype(F32), ((0, SUBLANE - CONV_K), (0, 0)))
            mix = (hf, hb, o, cb, z, z, z, mlstm_norm_g[e].astype(F32).reshape(1, -1), cw)
            seq_edges = (seq_starts, seq_ends)
            wo = hyb_w_out[e].astype(BF16)
        else:
            a = layer // 2
            bound = (HEAD_DIM ** 0.5 * LOG2E * BOUND_MARGIN) * jnp.max(jnp.abs(q_norm_g[a])) * jnp.max(jnp.abs(k_norm_g[a]))
            bound = bound.astype(F32)
            lane_const = (jnp.arange(LANE) == CONST_LANE).astype(F32)
            pads = jnp.stack([-bound * lane_const, lane_const] + [0.0 * lane_const] * (SUBLANE - 2))
            q, k, vt = _att_in(src[0], mod, g1, att_w_in[a].astype(BF16), pad_gain(q_norm_g[a]),
                               pad_gain(k_norm_g[a]), cos_t, sin_t, bd, pads, n_xt)
            y = lax.cond(bound <= MAX_BOUND_LOG2,
                         lambda: _flash(q, k, vt, nb, t, lc, bounded=True),
                         lambda: _flash(q, k, vt, nb, t, lc, bounded=False))
            mix = (y,)
            seq_edges = None
            wo = att_w_out[a].astype(BF16)
        src = (_outproj_mlp(mix, src, mod, g2, wo, w1_all, w2_all, layer,
                            nb * n_xt if last else n_tiles, n_xt, seq_edges),)
    return src[0].reshape(nb, t, d).astype(x.dtype)
```

```python
import functools

import jax
import jax.numpy as jnp
from jax import lax
from jax.experimental import pallas as pl
from jax.experimental.pallas import tpu as pltpu

F32 = jnp.float32
BF16 = jnp.bfloat16

D_MODEL = 1024
DEPTH = 4
GRID_W = 64
EPS = 1e-6
ML_HEADS = 4
ML_DK = 64
ML_DV = 128
ML_V_COLS = ML_HEADS * ML_DV
SC_WIDTH = D_MODEL // 2
assert ML_V_COLS == SC_WIDTH
CONV_K = 3
ATT_HEADS = 16
KV_HEADS = 4
HEAD_DIM = 64
GROUP = ATT_HEADS // KV_HEADS
ROPE_THETA = 10000.0
MLP_HIDDEN = 4 * D_MODEL

LANE = 128
SUBLANE = 8
TOKEN_TILE = 512
SEQ_BLOCK = 256
SCAN_CHUNK = SEQ_BLOCK
KV_TILE = TOKEN_TILE
FLASH_CHUNK = KV_TILE
NORM_BLOCK = 2 * LANE
ADALN_COL_BLOCKS = 4
VMEM_LIMIT_BYTES = 56 * 1024 * 1024

ATT_Q_COLS = ATT_HEADS * LANE
ATT_K_COLS = KV_HEADS * LANE


def _cparams(*sem):
    return pltpu.CompilerParams(dimension_semantics=sem, vmem_limit_bytes=VMEM_LIMIT_BYTES)


def _normmod(xf, g, shift, scale):
    var = jnp.mean(xf * xf, axis=-1, keepdims=True)
    return xf * lax.rsqrt(var + EPS) * g * (1.0 + scale) + shift


def _src_specs(src, tm):
    d = src[0].shape[1]
    if len(src) == 1:
        return [pl.BlockSpec((tm, d), lambda i: (i, 0))]
    n_main = src[0].shape[0] // tm
    assert src[1].shape[0] == tm
    return [pl.BlockSpec((tm, d), lambda i: (jnp.minimum(i, n_main - 1), 0)), pl.BlockSpec((tm, d), lambda i: (0, 0))]


def _src_rows(src_refs):
    if len(src_refs) == 1:
        return src_refs[0][...]
    on_ctx_tile = pl.program_id(0) == pl.num_programs(0) - 1
    return jnp.where(on_ctx_tile, src_refs[1][...], src_refs[0][...])


def _dot(a, b):
    return jnp.dot(a, b, preferred_element_type=F32)


def _dot_nt(a, b):
    return lax.dot_general(a, b, (((1,), (1,)), ((), ())), preferred_element_type=F32)


def _adaln_body(c_ref, w_ref, b_ref, o_ref):
    cv = c_ref[...]
    s = cv * jax.nn.sigmoid(cv)
    o_ref[0] = _dot(s.astype(BF16), w_ref[0].astype(BF16)) + b_ref[0]


def _adaln(cvec, ada_w, ada_b):
    depth, d, n = ada_w.shape
    bn = n // ADALN_COL_BLOCKS
    return pl.pallas_call(
        _adaln_body,
        grid=(depth, n // bn),
        in_specs=[pl.BlockSpec((SUBLANE, d), lambda l, j: (0, 0)),
                  pl.BlockSpec((1, d, bn), lambda l, j: (l, 0, j)),
                  pl.BlockSpec((1, 1, bn), lambda l, j: (l, 0, j))],
        out_specs=pl.BlockSpec((1, SUBLANE, bn), lambda l, j: (l, 0, j)),
        out_shape=jax.ShapeDtypeStruct((depth, SUBLANE, n), F32),
        compiler_params=_cparams("arbitrary", "arbitrary"),
        name="adaln",
    )(cvec, ada_w, ada_b.reshape(depth, 1, n))


def _hyb_in_body(*refs, n_src):
    mod_ref, g_ref, w_ref, gb_ref, q_ref, k_ref, vt_ref, o_ref, gate_ref, cb_ref, z_ref = refs[n_src:]
    d = D_MODEL
    h = _normmod(_src_rows(refs[:n_src]), g_ref[...], mod_ref[0, :, 0:d], mod_ref[0, :, d:2 * d]).astype(BF16)
    widths = (ML_HEADS * ML_DK, ML_HEADS * ML_DK, ML_V_COLS, ML_V_COLS, LANE, SC_WIDTH, SC_WIDTH, SC_WIDTH)
    starts = [sum(widths[:i]) for i in range(len(widths))]
    proj = lambda i: _dot(h, w_ref[:, starts[i]:starts[i] + widths[i]])
    q_ref[...] = (proj(0) * (ML_DK ** -0.5)).astype(BF16)
    k_ref[...] = proj(1).astype(BF16)
    vt_ref[0] = proj(2).T
    o_ref[...] = proj(3)
    g = proj(4) + gb_ref[...]
    lane = lax.broadcasted_iota(jnp.int32, g.shape, 1)
    is_forget = ((lane // ML_HEADS) % 2) == 1
    logsig = jnp.minimum(g, 0.0) - jnp.log1p(jnp.exp(-jnp.abs(g)))
    gate_ref[...] = jnp.where(is_forget, logsig, g)
    cb_ref[...] = proj(5)
    z_ref[...] = proj(6) * proj(7)


def _hyb_in(src, mod, g1, w, gate_b, n_xt):
    rows = sum(s.shape[0] for s in src)
    d = D_MODEL
    tm = TOKEN_TILE
    ncol = w.shape[1]
    row = lambda i: (i, 0)
    const = lambda i: (0, 0)
    qk = ML_HEADS * ML_DK
    outs = [(qk, BF16), (qk, BF16), None, (ML_V_COLS, F32), (LANE, F32), (SC_WIDTH, F32), (SC_WIDTH, F32)]
    nt = rows // tm
    vt_cols = ML_HEADS * ML_DV
    return pl.pallas_call(
        functools.partial(_hyb_in_body, n_src=len(src)),
        grid=(nt,),
        in_specs=_src_specs(src, tm) + [
                  pl.BlockSpec((1, 1, 6 * d), lambda i: (i // n_xt, 0, 0)),
                  pl.BlockSpec((1, d), const),
                  pl.BlockSpec((d, ncol), const),
                  pl.BlockSpec((1, LANE), const)],
        out_specs=[pl.BlockSpec((tm, o[0]), row) if o else pl.BlockSpec((1, vt_cols, tm), lambda i: (i, 0, 0))
                   for o in outs],
        out_shape=[jax.ShapeDtypeStruct((rows, o[0]), o[1]) if o else jax.ShapeDtypeStruct((nt, vt_cols, tm), F32)
                   for o in outs],
        compiler_params=_cparams("arbitrary"),
        name="hyb_in",
    )(*src, mod, g1, w, gate_b)


def _split3(a):
    hi = a.astype(BF16)
    r = a - hi.astype(F32)
    mid = r.astype(BF16)
    lo = (r - mid.astype(F32)).astype(BF16)
    return hi, mid, lo


def _mlstm_body(qf_ref, kf_ref, vtf_ref, gf_ref, qb_ref, kb_ref, vtb_ref, gb_ref, hf_ref, hb_ref, c_ref, m_ref):
    L = SCAN_CHUNK

    @pl.when(pl.program_id(1) == 0)
    def _():
        c_ref[...] = jnp.zeros_like(c_ref)
        m_ref[...] = jnp.zeros_like(m_ref)

    s_idx = lax.broadcasted_iota(jnp.int32, (L, L), 0)
    j_idx = lax.broadcasted_iota(jnp.int32, (L, L), 1)
    diag = (s_idx == j_idx).astype(BF16)
    ones_rows = jnp.ones((ML_DV, L), F32)
    pair_lane = lax.broadcasted_iota(jnp.int32, (1, LANE), 1)

    chains = []
    for d, (q_ref, k_ref, vt_ref, gate_ref, h_ref) in enumerate(((qf_ref, kf_ref, vtf_ref, gf_ref, hf_ref),
                                                                 (qb_ref, kb_ref, vtb_ref, gb_ref, hb_ref))):
        rev = d == 1
        visible = (s_idx >= j_idx) if rev else (s_idx <= j_idx)
        tr = visible.astype(BF16)
        tc = jnp.logical_not(visible).astype(BF16) + diag
        lane0 = 2 * ML_HEADS if rev else 0
        last = 0 if rev else L - 1
        G = gate_ref[...]
        GT = G.T
        bc = sum(_dot(tc, p) for p in _split3(G))
        br = sum(_dot(p, tr) for p in _split3(GT))
        ig_sh = pltpu.roll(G, ML_HEADS, axis=1)
        b_end = bc[last:last + 1, :]
        m0 = m_ref[d, 0:1, :]
        src_col = ig_sh - bc
        m_loc = jnp.max(b_end + src_col, axis=0, keepdims=True)
        m_new = jnp.maximum(b_end + m0, m_loc)
        a2_all = jnp.exp(b_end + m0 - m_new)
        g2_all = jnp.exp(m_loc - m_new)
        m_ref[d, 0:1, :] = m_new
        for h in range(ML_HEADS):
            fl = lane0 + ML_HEADS + h
            cs = slice(h * LANE, (h + 1) * LANE)
            pair = slice((h // 2) * LANE, (h // 2 + 1) * LANE)
            own = (pair_lane // ML_DK) == (h % 2)
            brow = br[fl:fl + 1, :]
            chains.append(dict(
                d=d, h=h, cs=cs, h_ref=h_ref, visible=visible, brow=brow,
                q=q_ref[:, pair], k=jnp.where(own, k_ref[:, pair], jnp.zeros((), BF16)),
                vaug=jnp.concatenate([vt_ref[0, cs, :], ones_rows], axis=0),
                c_old=c_ref[d, h],
                w_row=jnp.exp(b_end[:, fl:fl + 1] + GT[fl - ML_HEADS:fl - ML_HEADS + 1, :] - brow - m_loc[:, fl:fl + 1]),
                src=src_col[:, fl:fl + 1], m_inter=brow + m0[:, fl:fl + 1],
                a2=a2_all[:, fl:fl + 1], g2=g2_all[:, fl:fl + 1]))

    for c in chains:
        c["kq"] = _dot_nt(c["k"], c["q"])
        c["inter"] = _dot_nt(c["c_old"].astype(BF16), c["q"])
        c["c_loc"] = _dot((c["vaug"] * c["w_row"]).astype(BF16), c["k"])
    for c in chains:
        logd = jnp.where(c["visible"], c["brow"] + c["src"], -jnp.inf)
        c["m"] = jnp.maximum(c["m_inter"], jnp.max(logd, axis=0, keepdims=True))
        c["s_t"] = (c["kq"] * jnp.exp(logd - c["m"])).astype(BF16)
    for c in chains:
        num_t = _dot(c["vaug"].astype(BF16), c["s_t"]) + jnp.exp(c["m_inter"] - c["m"]) * c["inter"]
        h_t = num_t[:ML_DV, :] / jnp.maximum(jnp.abs(num_t[ML_DV:, :]), jnp.exp(-c["m"]))
        c["h_ref"][:, c["cs"]] = h_t.T
        c_ref[c["d"], c["h"]] = c["a2"] * c["c_old"] + c["g2"] * c["c_loc"]


def _mlstm(q, k, vt, gate, nb, t):
    rows = q.shape[0]
    blk = SEQ_BLOCK
    per_tile = TOKEN_TILE // blk
    nxb = t // blk
    ctx_blk0 = nb * nxb

    def specs(rev):
        def block(b, i):
            xi = (nxb - i) if rev else (i - 1)
            return jnp.where(i == 0, ctx_blk0 + b, b * nxb + xi)
        spec = lambda c: pl.BlockSpec((blk, c), lambda b, i: (block(b, i), 0))
        vt_spec = pl.BlockSpec((1, ML_HEADS * ML_DV, blk),
                               lambda b, i: (block(b, i) // per_tile, 0, block(b, i) % per_tile))
        qk = ML_HEADS * ML_DK
        return [spec(qk), spec(qk), vt_spec, spec(LANE)], spec(ML_V_COLS)

    (in_f, out_f), (in_b, out_b) = specs(False), specs(True)
    return pl.pallas_call(
        _mlstm_body,
        grid=(nb, nxb + 1),
        in_specs=in_f + in_b,
        out_specs=[out_f, out_b],
        out_shape=[jax.ShapeDtypeStruct((rows, ML_V_COLS), F32)] * 2,
        scratch_shapes=[pltpu.VMEM((2, ML_HEADS, 2 * ML_DV, LANE), F32), pltpu.VMEM((2, SUBLANE, LANE), F32)],
        compiler_params=_cparams("arbitrary", "arbitrary"),
        name="mlstm",
    )(q, k, vt, gate, q, k, vt, gate)


def _hyb_mix(hf_ref, hb_ref, o_ref, cb_ref, z_ref, zp_ref, zn_ref, mg_ref, cw_ref, starts, ends):
    tm = hf_ref.shape[0]
    i = pl.program_id(0)
    parts = []
    for h in range(ML_HEADS):
        cs = slice(h * ML_DV, (h + 1) * ML_DV)
        blk = hf_ref[:, cs] + hb_ref[:, cs]
        var = jnp.mean(blk * blk, axis=-1, keepdims=True)
        hn = blk * lax.rsqrt(var + EPS) * mg_ref[:, cs]
        parts.append((hn * jax.nn.sigmoid(o_ref[:, cs])).astype(BF16))
    z = z_ref[...]
    loc = lax.broadcasted_iota(jnp.int32, (tm, 1), 0)
    row = loc + i * tm
    is_start = functools.reduce(jnp.logical_or, [row == r for r in starts])
    is_end = functools.reduce(jnp.logical_or, [row == r for r in ends])
    zprev = jnp.where(loc == 0, zp_ref[SUBLANE - 1:SUBLANE, :], pltpu.roll(z, 1, axis=0))
    zprev = jnp.where(is_start, 0.0, zprev)
    znext = jnp.where(loc == tm - 1, zn_ref[0:1, :], pltpu.roll(z, tm - 1, axis=0))
    znext = jnp.where(is_end, 0.0, znext)
    conv = cw_ref[0:1, :] * zprev + cw_ref[1:2, :] * z + cw_ref[2:3, :] * znext
    parts.append((cb_ref[...] * conv).astype(BF16))
    return jnp.concatenate(parts, axis=1)


def _hyb_mix_specs(rows, tm):
    per = tm // SUBLANE
    nblk = rows // SUBLANE
    const = lambda i: (0, 0)
    w = SC_WIDTH
    return [pl.BlockSpec((tm, w), lambda i: (i, 0))] * 5 + [
        pl.BlockSpec((SUBLANE, w), lambda i: (jnp.maximum(i * per - 1, 0), 0)),
        pl.BlockSpec((SUBLANE, w), lambda i: (jnp.minimum((i + 1) * per, nblk - 1), 0)),
        pl.BlockSpec((1, w), const),
        pl.BlockSpec((SUBLANE, w), const)]


CONST_LANE = HEAD_DIM
V_ROWS = LANE
LOG2E = 1.4426950408889634
BOUND_MARGIN = 1.02
MAX_BOUND_LOG2 = 48.0


def _att_in_body(x_ref, mod_ref, g_ref, w_ref, qg_ref, kg_ref, cos_ref, sin_ref, bd_ref, pads_ref,
                 q_ref, k_ref, vt_ref):
    d = D_MODEL
    h = _normmod(x_ref[...], g_ref[...], mod_ref[0, :, 0:d], mod_ref[0, :, d:2 * d]).astype(BF16)
    cosv = cos_ref[...]
    sinv = sin_ref[...]
    bd = bd_ref[...]
    lane = lax.broadcasted_iota(jnp.int32, cosv.shape, 1)
    upper_half = ((lane // (HEAD_DIM // 4)) % 2) == 1
    first_head = lane < HEAD_DIM

    def norm_rope(p, ssq, gain, scale, pad, out_ref, c_out):
        pn = p * (lax.rsqrt(ssq * (1.0 / HEAD_DIM) + EPS) * scale) * gain
        for half in range(2):
            xh = pn[:, half * LANE:(half + 1) * LANE]
            partner = jnp.where(upper_half, pltpu.roll(xh, HEAD_DIM // 4, axis=1),
                                pltpu.roll(xh, LANE - HEAD_DIM // 4, axis=1))
            y = xh * cosv + partner * sinv
            c0 = c_out + 2 * half * LANE
            out_ref[:, c0:c0 + LANE] = jnp.where(first_head, y, pad).astype(BF16)
            out_ref[:, c0 + LANE:c0 + 2 * LANE] = jnp.where(first_head, pltpu.roll(y, HEAD_DIM, axis=1), pad).astype(BF16)

    nq, nk = ATT_HEADS * HEAD_DIM, KV_HEADS * HEAD_DIM
    q_scale = HEAD_DIM ** -0.5 * LOG2E
    nbk = NORM_BLOCK
    n_blk = (nq + nk) // nbk
    ps = [_dot(h, w_ref[:, j * nbk:(j + 1) * nbk]) for j in range(n_blk)]
    v = _dot(h, w_ref[:, nq + nk:nq + 2 * nk])
    ssqs = [_dot((p * p).astype(BF16), bd) for p in ps]
    for j in range(nq // nbk):
        norm_rope(ps[j], ssqs[j], qg_ref[...], q_scale, pads_ref[0:1, :], q_ref, 2 * j * nbk)
    norm_rope(ps[-1], ssqs[-1], kg_ref[...], 1.0, pads_ref[1:2, :], k_ref, 0)
    vt = v.T.astype(BF16)
    tail_row = lax.broadcasted_iota(jnp.int32, (V_ROWS - HEAD_DIM, vt.shape[1]), 0)
    tail = jnp.where(tail_row == 0, 1.0, 0.0).astype(BF16)
    for hh in range(KV_HEADS):
        r0 = hh * V_ROWS
        vt_ref[0, r0:r0 + HEAD_DIM, :] = vt[hh * HEAD_DIM:(hh + 1) * HEAD_DIM, :]
        vt_ref[0, r0 + HEAD_DIM:r0 + V_ROWS, :] = tail


def _att_in(stream, mod, g1, w, qg, kg, cos_t, sin_t, bd, pads, n_xt):
    rows, d = stream.shape
    tm = TOKEN_TILE
    nt = rows // tm
    ncol = w.shape[1]
    row = lambda i: (i, 0)
    const = lambda i: (0, 0)
    tab = lambda i: (jnp.where(i == nt - 1, n_xt, i % n_xt), 0)
    return pl.pallas_call(
        _att_in_body,
        grid=(nt,),
        in_specs=[pl.BlockSpec((tm, d), row),
                  pl.BlockSpec((1, 1, 6 * d), lambda i: (i // n_xt, 0, 0)),
                  pl.BlockSpec((1, d), const),
                  pl.BlockSpec((d, ncol), const),
                  pl.BlockSpec((1, NORM_BLOCK), const),
                  pl.BlockSpec((1, NORM_BLOCK), const),
                  pl.BlockSpec((tm, LANE), tab),
                  pl.BlockSpec((tm, LANE), tab),
                  pl.BlockSpec((NORM_BLOCK, NORM_BLOCK), const),
                  pl.BlockSpec((SUBLANE, LANE), const)],
        out_specs=[pl.BlockSpec((tm, ATT_Q_COLS), row),
                   pl.BlockSpec((tm, ATT_K_COLS), row),
                   pl.BlockSpec((1, KV_HEADS * V_ROWS, tm), lambda i: (i, 0, 0))],
        out_shape=[jax.ShapeDtypeStruct((rows, ATT_Q_COLS), BF16),
                   jax.ShapeDtypeStruct((rows, ATT_K_COLS), BF16),
                   jax.ShapeDtypeStruct((nt, KV_HEADS * V_ROWS, tm), BF16)],
        compiler_params=_cparams("arbitrary"),
        name="att_in",
    )(stream, mod, g1, w, qg, kg, cos_t, sin_t, bd, pads)


def _store_heads(o_ref, accs):
    outs = [a[:HEAD_DIM, :] * (1.0 / a[CONST_LANE:CONST_LANE + 1, :]) for a in accs]
    for pair in range(GROUP // 2):
        both = jnp.concatenate(outs[2 * pair:2 * pair + 2], axis=0)
        o_ref[:, pair * LANE:(pair + 1) * LANE] = both.T.astype(BF16)


def _flash_safe_body(q_ref, kc_ref, vtc_ref, kx_ref, vtx_ref, o_ref):
    is_latent = pl.program_id(2) < pl.num_programs(2) - 1
    tq = q_ref.shape[0]
    qs = [q_ref[:, g * LANE:(g + 1) * LANE] for g in range(GROUP)]

    def step(kt, vt, carry):
        new = []
        for g in range(GROUP):
            m, acc = carry[g]
            st = _dot_nt(kt, qs[g])
            m_new = jnp.maximum(m, jnp.max(st, axis=0, keepdims=True))
            p = jnp.exp2(st - m_new)
            acc_new = jnp.exp2(m - m_new) * acc + _dot(vt, p.astype(BF16))
            new.append((m_new, acc_new))
        return tuple(new)

    init = tuple((jnp.full((1, tq), -jnp.inf, F32), jnp.zeros((V_ROWS, tq), F32)) for _ in range(GROUP))
    carry = step(kc_ref[...], vtc_ref[0], init)
    carry = lax.fori_loop(
        0, jnp.where(is_latent, vtx_ref.shape[0], 0),
        lambda j, cr: step(kx_ref[pl.ds(pl.multiple_of(j * KV_TILE, KV_TILE), KV_TILE), :], vtx_ref[j], cr),
        carry)
    _store_heads(o_ref, [acc for _, acc in carry])


def _flash_bounded_body(q_ref, kc_ref, vtc_ref, kx_ref, vtx_ref, o_ref, qa_ref, acc_ref):
    tq = q_ref.shape[0]
    for g in range(GROUP):
        qa_ref[g * tq:(g + 1) * tq, :] = q_ref[:, g * LANE:(g + 1) * LANE]

    def chunk(kt, vt):
        st = _dot_nt(kt, qa_ref[...])
        return _dot(vt, jnp.exp2(st).astype(BF16))

    is_latent = pl.program_id(2) < pl.num_programs(2) - 1

    @pl.when(is_latent)
    def _():
        ck = FLASH_CHUNK
        per_tile = KV_TILE // ck
        n = vtx_ref.shape[0] * per_tile
        scores = lambda c: _dot_nt(kx_ref[c * ck:(c + 1) * ck, :], qa_ref[...])
        values_t = lambda c: vtx_ref[c // per_tile, :, (c % per_tile) * ck:(c % per_tile + 1) * ck]
        st = _dot_nt(kc_ref[...], qa_ref[...])
        for c in range(n + 1):
            st_next = scores(c) if c < n else None
            pv = _dot(vtc_ref[0] if c == 0 else values_t(c - 1), jnp.exp2(st).astype(BF16))
            if c == 0:
                acc_ref[...] = pv
            else:
                acc_ref[...] += pv
            st = st_next

    @pl.when(jnp.logical_not(is_latent))
    def _():
        acc_ref[...] = chunk(kc_ref[...], vtc_ref[0])

    _store_heads(o_ref, [acc_ref[:, g * tq:(g + 1) * tq] for g in range(GROUP)])


def _flash(q, k, vt, nb, t, lc, bounded):
    rows = q.shape[0]
    tq = SEQ_BLOCK
    n_xt = t // KV_TILE
    nq = t // tq
    ctx_q0 = nb * nq
    ctx_tile = nb * t // KV_TILE
    qmap = lambda b, h, i: (jnp.where(i == nq, ctx_q0 + b, b * nq + i), h)
    in_specs = [pl.BlockSpec((tq, GROUP * LANE), qmap),
                pl.BlockSpec((lc, LANE), lambda b, h, i: (nb * t // lc + b, h)),
                pl.BlockSpec((1, V_ROWS, lc), lambda b, h, i: (ctx_tile, h, b)),
                pl.BlockSpec((t, LANE), lambda b, h, i: (b, h)),
                pl.BlockSpec((n_xt, V_ROWS, KV_TILE), lambda b, h, i: (b, h, 0))]
    if bounded:
        body = _flash_bounded_body
        scratch = [pltpu.VMEM((GROUP * tq, LANE), BF16), pltpu.VMEM((V_ROWS, GROUP * tq), F32)]
    else:
        body = _flash_safe_body
        scratch = []
    return pl.pallas_call(
        body,
        grid=(nb, KV_HEADS, nq + 1),
        in_specs=in_specs,
        out_specs=pl.BlockSpec((tq, GROUP * HEAD_DIM), qmap),
        out_shape=jax.ShapeDtypeStruct((rows, ATT_HEADS * HEAD_DIM), BF16),
        scratch_shapes=scratch,
        compiler_params=_cparams("arbitrary", "arbitrary", "arbitrary"),
        name="flash_bounded" if bounded else "flash_safe",
    )(q, k, vt, k, vt)


def _outproj_mlp_body(*refs, n_src, seq_edges):
    mod_ref, g_ref, wo_ref, w1_ref, w2_ref, out_ref = refs[-6:]
    mix_refs = refs[n_src:-6]
    y = mix_refs[0][...] if seq_edges is None else _hyb_mix(*mix_refs, *seq_edges)
    d = D_MODEL
    mod = lambda k: mod_ref[0, :, k * d:(k + 1) * d]
    x1 = _src_rows(refs[:n_src]) + mod(2) * _dot(y, wo_ref[...])
    h2 = _normmod(x1, g_ref[...], mod(3), mod(4)).astype(BF16)
    acc = jnp.zeros_like(x1)
    for j in range(MLP_HIDDEN // d):
        u = jnp.maximum(_dot(h2, w1_ref[0, :, j * d:(j + 1) * d]), 0.0)
        acc = acc + _dot((u * u).astype(BF16), w2_ref[0, j * d:(j + 1) * d, :])
    out_ref[...] = x1 + mod(5) * acc


def _outproj_mlp(mix, src, mod, g2, wo, w1, w2, layer, n_tiles, n_xt, seq_edges=None):
    d = D_MODEL
    tm = TOKEN_TILE
    row = lambda i: (i, 0)
    const = lambda i: (0, 0)
    mix_specs = [pl.BlockSpec((tm, d), row)] if seq_edges is None else _hyb_mix_specs(mix[0].shape[0], tm)
    return pl.pallas_call(
        functools.partial(_outproj_mlp_body, n_src=len(src), seq_edges=seq_edges),
        grid=(n_tiles,),
        in_specs=_src_specs(src, tm) + mix_specs + [
                  pl.BlockSpec((1, 1, 6 * d), lambda i: (i // n_xt, 0, 0)),
                  pl.BlockSpec((1, d), const),
                  pl.BlockSpec((d, d), const),
                  pl.BlockSpec((1, d, MLP_HIDDEN), lambda i: (layer, 0, 0)),
                  pl.BlockSpec((1, MLP_HIDDEN, d), lambda i: (layer, 0, 0))],
        out_specs=pl.BlockSpec((tm, d), row),
        out_shape=jax.ShapeDtypeStruct((n_tiles * tm, d), F32),
        compiler_params=_cparams("arbitrary"),
        name="outproj_mlp",
    )(*src, *mix, mod, g2, wo, w1, w2)


def _hyb_weights(w_in, gate_b):
    n_gate = 4 * ML_HEADS
    g0 = 2 * ML_HEADS * ML_DK + 2 * ML_HEADS * ML_DV
    w = jnp.concatenate([w_in[:, :g0], jnp.pad(w_in[:, g0:g0 + n_gate], ((0, 0), (0, LANE - n_gate))),
                         w_in[:, g0 + n_gate:]], axis=1).astype(BF16)
    gb = jnp.pad(gate_b.astype(F32), (0, LANE - n_gate)).reshape(1, LANE)
    return w, gb


def _rope_tables(t, n_ident):
    half = HEAD_DIM // 2
    inv = ROPE_THETA ** (-jnp.arange(0, half, 2, dtype=F32) / half)
    n_rows = t // GRID_W
    r = jnp.arange(n_rows, dtype=F32)[:, None] * inv
    c = jnp.arange(GRID_W, dtype=F32)[:, None] * inv
    z_r, z_c = jnp.zeros_like(r), jnp.zeros_like(c)
    lanes = lambda q0, q1, q2, q3: jnp.concatenate([q0, q1, q2, q3] * 2, axis=-1)
    outer = lambda by_row, by_col: (by_row[:, None, :] + by_col[None, :, :]).reshape(t, LANE)
    cos_t = outer(lanes(jnp.cos(r), jnp.cos(r), z_r, z_r), lanes(z_c, z_c, jnp.cos(c), jnp.cos(c)))
    sin_t = outer(lanes(-jnp.sin(r), jnp.sin(r), z_r, z_r), lanes(z_c, z_c, -jnp.sin(c), jnp.sin(c)))
    cos_t = jnp.concatenate([cos_t, jnp.ones((n_ident, LANE), F32)], axis=0)
    sin_t = jnp.concatenate([sin_t, jnp.zeros((n_ident, LANE), F32)], axis=0)
    return cos_t, sin_t


def kernel(x, c, ctx, c_ctx, ada_w, ada_b, norm1_g, norm2_g, mlp_w1, mlp_w2, hyb_w_in, hyb_gate_b, mlstm_norm_g,
           conv_w, hyb_w_out, att_w_in, q_norm_g, k_norm_g, att_w_out):
    nb, t, d = x.shape
    lc = ctx.shape[1]
    assert d == D_MODEL and nb * lc == TOKEN_TILE and lc == SEQ_BLOCK and t % TOKEN_TILE == 0
    assert nb + 1 <= SUBLANE
    n_xt = t // TOKEN_TILE
    n_tiles = nb * n_xt + 1

    src = (x.reshape(nb * t, d).astype(F32), ctx.reshape(nb * lc, d).astype(F32))
    cvec = jnp.concatenate([c, c_ctx[None, :], jnp.zeros((SUBLANE - nb - 1, d), c.dtype)], axis=0).astype(F32)
    mods = _adaln(cvec, ada_w.astype(F32), ada_b.astype(F32))

    w1_all, w2_all = mlp_w1.astype(BF16), mlp_w2.astype(BF16)
    cos_t, sin_t = _rope_tables(t, TOKEN_TILE)
    head_of = lambda axis: lax.broadcasted_iota(jnp.int32, (NORM_BLOCK, NORM_BLOCK), axis) // HEAD_DIM
    bd = (head_of(0) == head_of(1)).astype(BF16)
    tile_gain = lambda g: jnp.tile(g.astype(F32), NORM_BLOCK // HEAD_DIM).reshape(1, NORM_BLOCK)
    seq_starts = tuple(b * t for b in range(nb)) + tuple(nb * t + b * lc for b in range(nb))
    seq_ends = tuple((b + 1) * t - 1 for b in range(nb)) + tuple(nb * t + (b + 1) * lc - 1 for b in range(nb))

    for layer in range(DEPTH):
        last = layer == DEPTH - 1
        mod = mods[layer, :nb + 1].reshape(nb + 1, 1, 6 * d)
        g1 = norm1_g[layer].astype(F32).reshape(1, d)
        g2 = norm2_g[layer].astype(F32).reshape(1, d)
        if layer % 2 == 0:
            e = layer // 2
            w, gb = _hyb_weights(hyb_w_in[e], hyb_gate_b[e])
            q, k, vt, o, gate, cb, z = _hyb_in(src, mod, g1, w, gb, n_xt)
            hf, hb = _mlstm(q, k, vt, gate, nb, t)
            cw = jnp.pad(conv_w[e].astype(F32), ((0, SUBLANE - CONV_K), (0, 0)))
            mix = (hf, hb, o, cb, z, z, z, mlstm_norm_g[e].astype(F32).reshape(1, -1), cw)
            seq_edges = (seq_starts, seq_ends)
            wo = hyb_w_out[e].astype(BF16)
        else:
            a = layer // 2
            bound = (HEAD_DIM ** 0.5 * LOG2E * BOUND_MARGIN) * jnp.max(jnp.abs(q_norm_g[a])) * jnp.max(jnp.abs(k_norm_g[a]))
            bound = bound.astype(F32)
            lane_const = (jnp.arange(LANE) == CONST_LANE).astype(F32)
            pads = jnp.stack([-bound * lane_const, lane_const] + [0.0 * lane_const] * (SUBLANE - 2))
            q, k, vt = _att_in(src[0], mod, g1, att_w_in[a].astype(BF16), tile_gain(q_norm_g[a]),
                               tile_gain(k_norm_g[a]), cos_t, sin_t, bd, pads, n_xt)
            y = lax.cond(bound <= MAX_BOUND_LOG2,
                         lambda: _flash(q, k, vt, nb, t, lc, bounded=True),
                         lambda: _flash(q, k, vt, nb, t, lc, bounded=False))
            mix = (y,)
            seq_edges = None
            wo = att_w_out[a].astype(BF16)
        src = (_outproj_mlp(mix, src, mod, g2, wo, w1_all, w2_all, layer,
                            nb * n_xt if last else n_tiles, n_xt, seq_edges),)
    return src[0].reshape(nb, t, d).astype(x.dtype)
```

```python
import functools

import jax
import jax.numpy as jnp
from jax import lax
from jax.experimental import pallas as pl
from jax.experimental.pallas import tpu as pltpu

F32 = jnp.float32
BF16 = jnp.bfloat16

D_MODEL = 1024
DEPTH = 4
GRID_W = 64
EPS = 1e-6
ML_HEADS = 4
ML_DK = 64
ML_DV = 128
ML_V_COLS = ML_HEADS * ML_DV
SC_WIDTH = D_MODEL // 2
assert ML_V_COLS == SC_WIDTH
CONV_K = 3
ATT_HEADS = 16
KV_HEADS = 4
HEAD_DIM = 64
GROUP = ATT_HEADS // KV_HEADS
ROPE_THETA = 10000.0
MLP_HIDDEN = 4 * D_MODEL

LANE = 128
SUBLANE = 8
TOKEN_TILE = 512
SEQ_BLOCK = 256
SCAN_CHUNK = SEQ_BLOCK
KV_TILE = TOKEN_TILE
FLASH_CHUNK = KV_TILE
NORM_ROWS = 16
NORM_BLOCK = 2 * LANE
ADALN_COL_BLOCKS = 4
VMEM_LIMIT_BYTES = 56 * 1024 * 1024

ATT_Q_COLS = ATT_HEADS * LANE
ATT_K_COLS = KV_HEADS * LANE


def _cparams(*sem):
    return pltpu.CompilerParams(dimension_semantics=sem, vmem_limit_bytes=VMEM_LIMIT_BYTES)


def _normmod(xf, g, shift, scale):
    var = jnp.mean(xf * xf, axis=-1, keepdims=True)
    return xf * lax.rsqrt(var + EPS) * g * (1.0 + scale) + shift


def _src_specs(src, tm):
    d = src[0].shape[1]
    if len(src) == 1:
        return [pl.BlockSpec((tm, d), lambda i: (i, 0))]
    n_main = src[0].shape[0] // tm
    assert src[1].shape[0] == tm
    return [pl.BlockSpec((tm, d), lambda i: (jnp.minimum(i, n_main - 1), 0)), pl.BlockSpec((tm, d), lambda i: (0, 0))]


def _src_rows(src_refs):
    if len(src_refs) == 1:
        return src_refs[0][...]
    on_ctx_tile = pl.program_id(0) == pl.num_programs(0) - 1
    return jnp.where(on_ctx_tile, src_refs[1][...], src_refs[0][...])


def _dot(a, b):
    return jnp.dot(a, b, preferred_element_type=F32)


def _dot_nt(a, b):
    return lax.dot_general(a, b, (((1,), (1,)), ((), ())), preferred_element_type=F32)


def _adaln_body(c_ref, w_ref, b_ref, o_ref):
    cv = c_ref[...]
    s = cv * jax.nn.sigmoid(cv)
    o_ref[0] = _dot(s.astype(BF16), w_ref[0].astype(BF16)) + b_ref[0]


def _adaln(cvec, ada_w, ada_b):
    depth, d, n = ada_w.shape
    bn = n // ADALN_COL_BLOCKS
    return pl.pallas_call(
        _adaln_body,
        grid=(depth, n // bn),
        in_specs=[pl.BlockSpec((SUBLANE, d), lambda l, j: (0, 0)),
                  pl.BlockSpec((1, d, bn), lambda l, j: (l, 0, j)),
                  pl.BlockSpec((1, 1, bn), lambda l, j: (l, 0, j))],
        out_specs=pl.BlockSpec((1, SUBLANE, bn), lambda l, j: (l, 0, j)),
        out_shape=jax.ShapeDtypeStruct((depth, SUBLANE, n), F32),
        compiler_params=_cparams("arbitrary", "arbitrary"),
        name="adaln",
    )(cvec, ada_w, ada_b.reshape(depth, 1, n))


def _hyb_in_body(*refs, n_src):
    mod_ref, g_ref, w_ref, gb_ref, q_ref, k_ref, vt_ref, o_ref, gate_ref, cb_ref, z_ref = refs[n_src:]
    d = D_MODEL
    h = _normmod(_src_rows(refs[:n_src]), g_ref[...], mod_ref[0, :, 0:d], mod_ref[0, :, d:2 * d]).astype(BF16)
    widths = (ML_HEADS * ML_DK, ML_HEADS * ML_DK, ML_V_COLS, ML_V_COLS, LANE, SC_WIDTH, SC_WIDTH, SC_WIDTH)
    starts = [sum(widths[:i]) for i in range(len(widths))]
    proj = lambda i: _dot(h, w_ref[:, starts[i]:starts[i] + widths[i]])
    q_ref[...] = (proj(0) * (ML_DK ** -0.5)).astype(BF16)
    k_ref[...] = proj(1).astype(BF16)
    vt_ref[0] = proj(2).T
    o_ref[...] = proj(3)
    g = proj(4) + gb_ref[...]
    lane = lax.broadcasted_iota(jnp.int32, g.shape, 1)
    is_forget = ((lane // ML_HEADS) % 2) == 1
    logsig = jnp.minimum(g, 0.0) - jnp.log1p(jnp.exp(-jnp.abs(g)))
    gate_ref[...] = jnp.where(is_forget, logsig, g)
    cb_ref[...] = proj(5)
    z_ref[...] = proj(6) * proj(7)


def _hyb_in(src, mod, g1, w, gate_b, n_xt):
    rows = sum(s.shape[0] for s in src)
    d = D_MODEL
    tm = TOKEN_TILE
    ncol = w.shape[1]
    row = lambda i: (i, 0)
    const = lambda i: (0, 0)
    qk = ML_HEADS * ML_DK
    outs = [(qk, BF16), (qk, BF16), None, (ML_V_COLS, F32), (LANE, F32), (SC_WIDTH, F32), (SC_WIDTH, F32)]
    nt = rows // tm
    vt_cols = ML_HEADS * ML_DV
    return pl.pallas_call(
        functools.partial(_hyb_in_body, n_src=len(src)),
        grid=(nt,),
        in_specs=_src_specs(src, tm) + [
                  pl.BlockSpec((1, 1, 6 * d), lambda i: (i // n_xt, 0, 0)),
                  pl.BlockSpec((1, d), const),
                  pl.BlockSpec((d, ncol), const),
                  pl.BlockSpec((1, LANE), const)],
        out_specs=[pl.BlockSpec((tm, o[0]), row) if o else pl.BlockSpec((1, vt_cols, tm), lambda i: (i, 0, 0))
                   for o in outs],
        out_shape=[jax.ShapeDtypeStruct((rows, o[0]), o[1]) if o else jax.ShapeDtypeStruct((nt, vt_cols, tm), F32)
                   for o in outs],
        compiler_params=_cparams("arbitrary"),
        name="hyb_in",
    )(*src, mod, g1, w, gate_b)


def _split3(a):
    hi = a.astype(BF16)
    r = a - hi.astype(F32)
    mid = r.astype(BF16)
    lo = (r - mid.astype(F32)).astype(BF16)
    return hi, mid, lo


def _mlstm_body(qf_ref, kf_ref, vtf_ref, gf_ref, qb_ref, kb_ref, vtb_ref, gb_ref, hf_ref, hb_ref, c_ref, m_ref):
    L = SCAN_CHUNK

    @pl.when(pl.program_id(1) == 0)
    def _():
        c_ref[...] = jnp.zeros_like(c_ref)
        m_ref[...] = jnp.zeros_like(m_ref)

    s_idx = lax.broadcasted_iota(jnp.int32, (L, L), 0)
    j_idx = lax.broadcasted_iota(jnp.int32, (L, L), 1)
    diag = (s_idx == j_idx).astype(BF16)
    ones_rows = jnp.ones((NORM_ROWS, L), F32)
    pair_lane = lax.broadcasted_iota(jnp.int32, (1, LANE), 1)

    chains = []
    for d, (q_ref, k_ref, vt_ref, gate_ref, h_ref) in enumerate(((qf_ref, kf_ref, vtf_ref, gf_ref, hf_ref),
                                                                 (qb_ref, kb_ref, vtb_ref, gb_ref, hb_ref))):
        rev = d == 1
        visible = (s_idx >= j_idx) if rev else (s_idx <= j_idx)
        tr = visible.astype(BF16)
        tc = jnp.logical_not(visible).astype(BF16) + diag
        lane0 = 2 * ML_HEADS if rev else 0
        last = 0 if rev else L - 1
        G = gate_ref[...]
        GT = G.T
        bc = sum(_dot(tc, p) for p in _split3(G))
        br = sum(_dot(p, tr) for p in _split3(GT))
        ig_sh = pltpu.roll(G, ML_HEADS, axis=1)
        b_end = bc[last:last + 1, :]
        m0 = m_ref[d, 0:1, :]
        src_col = ig_sh - bc
        m_loc = jnp.max(b_end + src_col, axis=0, keepdims=True)
        m_new = jnp.maximum(b_end + m0, m_loc)
        a2_all = jnp.exp(b_end + m0 - m_new)
        g2_all = jnp.exp(m_loc - m_new)
        m_ref[d, 0:1, :] = m_new
        for h in range(ML_HEADS):
            fl = lane0 + ML_HEADS + h
            cs = slice(h * LANE, (h + 1) * LANE)
            pair = slice((h // 2) * LANE, (h // 2 + 1) * LANE)
            own = (pair_lane // ML_DK) == (h % 2)
            brow = br[fl:fl + 1, :]
            chains.append(dict(
                d=d, h=h, cs=cs, h_ref=h_ref, visible=visible, brow=brow,
                q=q_ref[:, pair], k=jnp.where(own, k_ref[:, pair], jnp.zeros((), BF16)),
                vaug=jnp.concatenate([vt_ref[0, cs, :], ones_rows], axis=0),
                c_old=c_ref[d, h],
                w_row=jnp.exp(b_end[:, fl:fl + 1] + GT[fl - ML_HEADS:fl - ML_HEADS + 1, :] - brow - m_loc[:, fl:fl + 1]),
                src=src_col[:, fl:fl + 1], m_inter=brow + m0[:, fl:fl + 1],
                a2=a2_all[:, fl:fl + 1], g2=g2_all[:, fl:fl + 1]))

    for c in chains:
        c["kq"] = _dot_nt(c["k"], c["q"])
        c["inter"] = _dot_nt(c["c_old"].astype(BF16), c["q"])
        c["c_loc"] = _dot((c["vaug"] * c["w_row"]).astype(BF16), c["k"])
    for c in chains:
        logd = jnp.where(c["visible"], c["brow"] + c["src"], -jnp.inf)
        c["m"] = jnp.maximum(c["m_inter"], jnp.max(logd, axis=0, keepdims=True))
        c["s_t"] = (c["kq"] * jnp.exp(logd - c["m"])).astype(BF16)
    for c in chains:
        num_t = _dot(c["vaug"].astype(BF16), c["s_t"]) + jnp.exp(c["m_inter"] - c["m"]) * c["inter"]
        h_t = num_t[:ML_DV, :] / jnp.maximum(jnp.abs(num_t[ML_DV:ML_DV + 1, :]), jnp.exp(-c["m"]))
        c["h_ref"][:, c["cs"]] = h_t.T
        c_ref[c["d"], c["h"]] = c["a2"] * c["c_old"] + c["g2"] * c["c_loc"]


def _mlstm(q, k, vt, gate, nb, t):
    rows = q.shape[0]
    blk = SEQ_BLOCK
    per_tile = TOKEN_TILE // blk
    nxb = t // blk
    ctx_blk0 = nb * nxb

    def specs(rev):
        def block(b, i):
            xi = (nxb - i) if rev else (i - 1)
            return jnp.where(i == 0, ctx_blk0 + b, b * nxb + xi)
        spec = lambda c: pl.BlockSpec((blk, c), lambda b, i: (block(b, i), 0))
        vt_spec = pl.BlockSpec((1, ML_HEADS * ML_DV, blk),
                               lambda b, i: (block(b, i) // per_tile, 0, block(b, i) % per_tile))
        qk = ML_HEADS * ML_DK
        return [spec(qk), spec(qk), vt_spec, spec(LANE)], spec(ML_V_COLS)

    (in_f, out_f), (in_b, out_b) = specs(False), specs(True)
    return pl.pallas_call(
        _mlstm_body,
        grid=(nb, nxb + 1),
        in_specs=in_f + in_b,
        out_specs=[out_f, out_b],
        out_shape=[jax.ShapeDtypeStruct((rows, ML_V_COLS), F32)] * 2,
        scratch_shapes=[pltpu.VMEM((2, ML_HEADS, ML_DV + NORM_ROWS, LANE), F32), pltpu.VMEM((2, SUBLANE, LANE), F32)],
        compiler_params=_cparams("arbitrary", "arbitrary"),
        name="mlstm",
    )(q, k, vt, gate, q, k, vt, gate)


def _hyb_mix(hf_ref, hb_ref, o_ref, cb_ref, z_ref, zp_ref, zn_ref, mg_ref, cw_ref, starts, ends):
    tm = hf_ref.shape[0]
    i = pl.program_id(0)
    parts = []
    for h in range(ML_HEADS):
        cs = slice(h * ML_DV, (h + 1) * ML_DV)
        blk = hf_ref[:, cs] + hb_ref[:, cs]
        var = jnp.mean(blk * blk, axis=-1, keepdims=True)
        hn = blk * lax.rsqrt(var + EPS) * mg_ref[:, cs]
        parts.append((hn * jax.nn.sigmoid(o_ref[:, cs])).astype(BF16))
    z = z_ref[...]
    loc = lax.broadcasted_iota(jnp.int32, (tm, 1), 0)
    row = loc + i * tm
    is_start = functools.reduce(jnp.logical_or, [row == r for r in starts])
    is_end = functools.reduce(jnp.logical_or, [row == r for r in ends])
    zprev = jnp.where(loc == 0, zp_ref[SUBLANE - 1:SUBLANE, :], pltpu.roll(z, 1, axis=0))
    zprev = jnp.where(is_start, 0.0, zprev)
    znext = jnp.where(loc == tm - 1, zn_ref[0:1, :], pltpu.roll(z, tm - 1, axis=0))
    znext = jnp.where(is_end, 0.0, znext)
    conv = cw_ref[0:1, :] * zprev + cw_ref[1:2, :] * z + cw_ref[2:3, :] * znext
    parts.append((cb_ref[...] * conv).astype(BF16))
    return jnp.concatenate(parts, axis=1)


def _hyb_mix_specs(rows, tm):
    per = tm // SUBLANE
    nblk = rows // SUBLANE
    const = lambda i: (0, 0)
    w = SC_WIDTH
    return [pl.BlockSpec((tm, w), lambda i: (i, 0))] * 5 + [
        pl.BlockSpec((SUBLANE, w), lambda i: (jnp.maximum(i * per - 1, 0), 0)),
        pl.BlockSpec((SUBLANE, w), lambda i: (jnp.minimum((i + 1) * per, nblk - 1), 0)),
        pl.BlockSpec((1, w), const),
        pl.BlockSpec((SUBLANE, w), const)]


CONST_LANE = HEAD_DIM
V_ROWS = LANE
LOG2E = 1.4426950408889634
BOUND_MARGIN = 1.02
MAX_BOUND_LOG2 = 48.0


def _att_in_body(x_ref, mod_ref, g_ref, w_ref, qg_ref, kg_ref, cos_ref, sin_ref, bd_ref, pads_ref,
                 q_ref, k_ref, vt_ref):
    d = D_MODEL
    h = _normmod(x_ref[...], g_ref[...], mod_ref[0, :, 0:d], mod_ref[0, :, d:2 * d]).astype(BF16)
    cosv = cos_ref[...]
    sinv = sin_ref[...]
    bd = bd_ref[...]
    lane = lax.broadcasted_iota(jnp.int32, cosv.shape, 1)
    upper_half = ((lane // (HEAD_DIM // 4)) % 2) == 1
    first_head = lane < HEAD_DIM

    def norm_rope(p, ssq, gain, scale, pad, out_ref, c_out):
        pn = p * (lax.rsqrt(ssq * (1.0 / HEAD_DIM) + EPS) * scale) * gain
        for half in range(2):
            xh = pn[:, half * LANE:(half + 1) * LANE]
            partner = jnp.where(upper_half, pltpu.roll(xh, HEAD_DIM // 4, axis=1),
                                pltpu.roll(xh, LANE - HEAD_DIM // 4, axis=1))
            y = xh * cosv + partner * sinv
            c0 = c_out + 2 * half * LANE
            out_ref[:, c0:c0 + LANE] = jnp.where(first_head, y, pad).astype(BF16)
            out_ref[:, c0 + LANE:c0 + 2 * LANE] = jnp.where(first_head, pltpu.roll(y, HEAD_DIM, axis=1), pad).astype(BF16)

    nq, nk = ATT_HEADS * HEAD_DIM, KV_HEADS * HEAD_DIM
    q_scale = HEAD_DIM ** -0.5 * LOG2E
    nbk = NORM_BLOCK
    n_blk = (nq + nk) // nbk
    ps = [_dot(h, w_ref[:, j * nbk:(j + 1) * nbk]) for j in range(n_blk)]
    v = _dot(h, w_ref[:, nq + nk:nq + 2 * nk])
    ssqs = [_dot((p * p).astype(BF16), bd) for p in ps]
    for j in range(nq // nbk):
        norm_rope(ps[j], ssqs[j], qg_ref[...], q_scale, pads_ref[0:1, :], q_ref, 2 * j * nbk)
    norm_rope(ps[-1], ssqs[-1], kg_ref[...], 1.0, pads_ref[1:2, :], k_ref, 0)
    vt = v.T.astype(BF16)
    tail_row = lax.broadcasted_iota(jnp.int32, (V_ROWS - HEAD_DIM, vt.shape[1]), 0)
    tail = jnp.where(tail_row == 0, 1.0, 0.0).astype(BF16)
    for hh in range(KV_HEADS):
        r0 = hh * V_ROWS
        vt_ref[0, r0:r0 + HEAD_DIM, :] = vt[hh * HEAD_DIM:(hh + 1) * HEAD_DIM, :]
        vt_ref[0, r0 + HEAD_DIM:r0 + V_ROWS, :] = tail


def _att_in(stream, mod, g1, w, qg, kg, cos_t, sin_t, bd, pads, n_xt):
    rows, d = stream.shape
    tm = TOKEN_TILE
    nt = rows // tm
    ncol = w.shape[1]
    row = lambda i: (i, 0)
    const = lambda i: (0, 0)
    tab = lambda i: (jnp.where(i == nt - 1, n_xt, i % n_xt), 0)
    return pl.pallas_call(
        _att_in_body,
        grid=(nt,),
        in_specs=[pl.BlockSpec((tm, d), row),
                  pl.BlockSpec((1, 1, 6 * d), lambda i: (i // n_xt, 0, 0)),
                  pl.BlockSpec((1, d), const),
                  pl.BlockSpec((d, ncol), const),
                  pl.BlockSpec((1, NORM_BLOCK), const),
                  pl.BlockSpec((1, NORM_BLOCK), const),
                  pl.BlockSpec((tm, LANE), tab),
                  pl.BlockSpec((tm, LANE), tab),
                  pl.BlockSpec((NORM_BLOCK, NORM_BLOCK), const),
                  pl.BlockSpec((SUBLANE, LANE), const)],
        out_specs=[pl.BlockSpec((tm, ATT_Q_COLS), row),
                   pl.BlockSpec((tm, ATT_K_COLS), row),
                   pl.BlockSpec((1, KV_HEADS * V_ROWS, tm), lambda i: (i, 0, 0))],
        out_shape=[jax.ShapeDtypeStruct((rows, ATT_Q_COLS), BF16),
                   jax.ShapeDtypeStruct((rows, ATT_K_COLS), BF16),
                   jax.ShapeDtypeStruct((nt, KV_HEADS * V_ROWS, tm), BF16)],
        compiler_params=_cparams("arbitrary"),
        name="att_in",
    )(stream, mod, g1, w, qg, kg, cos_t, sin_t, bd, pads)


def _store_heads(o_ref, accs):
    outs = [a[:HEAD_DIM, :] * (1.0 / a[CONST_LANE:CONST_LANE + 1, :]) for a in accs]
    for pair in range(GROUP // 2):
        both = jnp.concatenate(outs[2 * pair:2 * pair + 2], axis=0)
        o_ref[:, pair * LANE:(pair + 1) * LANE] = both.T.astype(BF16)


def _flash_safe_body(q_ref, kc_ref, vtc_ref, kx_ref, vtx_ref, o_ref):
    is_latent = pl.program_id(2) < pl.num_programs(2) - 1
    tq = q_ref.shape[0]
    qs = [q_ref[:, g * LANE:(g + 1) * LANE] for g in range(GROUP)]

    def step(kt, vt, carry):
        new = []
        for g in range(GROUP):
            m, acc = carry[g]
            st = _dot_nt(kt, qs[g])
            m_new = jnp.maximum(m, jnp.max(st, axis=0, keepdims=True))
            p = jnp.exp2(st - m_new)
            acc_new = jnp.exp2(m - m_new) * acc + _dot(vt, p.astype(BF16))
            new.append((m_new, acc_new))
        return tuple(new)

    init = tuple((jnp.full((1, tq), -jnp.inf, F32), jnp.zeros((V_ROWS, tq), F32)) for _ in range(GROUP))
    carry = step(kc_ref[...], vtc_ref[0], init)
    carry = lax.fori_loop(
        0, jnp.where(is_latent, vtx_ref.shape[0], 0),
        lambda j, cr: step(kx_ref[pl.ds(pl.multiple_of(j * KV_TILE, KV_TILE), KV_TILE), :], vtx_ref[j], cr),
        carry)
    _store_heads(o_ref, [acc for _, acc in carry])


def _flash_bounded_body(q_ref, kc_ref, vtc_ref, kx_ref, vtx_ref, o_ref, qa_ref, acc_ref):
    tq = q_ref.shape[0]
    for g in range(GROUP):
        qa_ref[g * tq:(g + 1) * tq, :] = q_ref[:, g * LANE:(g + 1) * LANE]

    def chunk(kt, vt):
        st = _dot_nt(kt, qa_ref[...])
        return _dot(vt, jnp.exp2(st).astype(BF16))

    is_latent = pl.program_id(2) < pl.num_programs(2) - 1

    @pl.when(is_latent)
    def _():
        ck = FLASH_CHUNK
        per_tile = KV_TILE // ck
        n = vtx_ref.shape[0] * per_tile
        scores = lambda c: _dot_nt(kx_ref[c * ck:(c + 1) * ck, :], qa_ref[...])
        values_t = lambda c: vtx_ref[c // per_tile, :, (c % per_tile) * ck:(c % per_tile + 1) * ck]
        st = _dot_nt(kc_ref[...], qa_ref[...])
        for c in range(n + 1):
            st_next = scores(c) if c < n else None
            pv = _dot(vtc_ref[0] if c == 0 else values_t(c - 1), jnp.exp2(st).astype(BF16))
            if c == 0:
                acc_ref[...] = pv
            else:
                acc_ref[...] += pv
            st = st_next

    @pl.when(jnp.logical_not(is_latent))
    def _():
        acc_ref[...] = chunk(kc_ref[...], vtc_ref[0])

    _store_heads(o_ref, [acc_ref[:, g * tq:(g + 1) * tq] for g in range(GROUP)])


def _flash(q, k, vt, nb, t, lc, bounded):
    rows = q.shape[0]
    tq = SEQ_BLOCK
    n_xt = t // KV_TILE
    nq = t // tq
    ctx_q0 = nb * nq
    ctx_tile = nb * t // KV_TILE
    qmap = lambda b, h, i: (jnp.where(i == nq, ctx_q0 + b, b * nq + i), h)
    in_specs = [pl.BlockSpec((tq, GROUP * LANE), qmap),
                pl.BlockSpec((lc, LANE), lambda b, h, i: (nb * t // lc + b, h)),
                pl.BlockSpec((1, V_ROWS, lc), lambda b, h, i: (ctx_tile, h, b)),
                pl.BlockSpec((t, LANE), lambda b, h, i: (b, h)),
                pl.BlockSpec((n_xt, V_ROWS, KV_TILE), lambda b, h, i: (b, h, 0))]
    if bounded:
        body = _flash_bounded_body
        scratch = [pltpu.VMEM((GROUP * tq, LANE), BF16), pltpu.VMEM((V_ROWS, GROUP * tq), F32)]
    else:
        body = _flash_safe_body
        scratch = []
    return pl.pallas_call(
        body,
        grid=(nb, KV_HEADS, nq + 1),
        in_specs=in_specs,
        out_specs=pl.BlockSpec((tq, GROUP * HEAD_DIM), qmap),
        out_shape=jax.ShapeDtypeStruct((rows, ATT_HEADS * HEAD_DIM), BF16),
        scratch_shapes=scratch,
        compiler_params=_cparams("arbitrary", "arbitrary", "arbitrary"),
        name="flash_bounded" if bounded else "flash_safe",
    )(q, k, vt, k, vt)


def _outproj_mlp_body(*refs, n_src, seq_edges):
    mod_ref, g_ref, wo_ref, w1_ref, w2_ref, out_ref = refs[-6:]
    mix_refs = refs[n_src:-6]
    y = mix_refs[0][...] if seq_edges is None else _hyb_mix(*mix_refs, *seq_edges)
    d = D_MODEL
    mod = lambda k: mod_ref[0, :, k * d:(k + 1) * d]
    x1 = _src_rows(refs[:n_src]) + mod(2) * _dot(y, wo_ref[...])
    h2 = _normmod(x1, g_ref[...], mod(3), mod(4)).astype(BF16)
    acc = jnp.zeros_like(x1)
    for j in range(MLP_HIDDEN // d):
        u = jnp.maximum(_dot(h2, w1_ref[0, :, j * d:(j + 1) * d]), 0.0)
        acc = acc + _dot((u * u).astype(BF16), w2_ref[0, j * d:(j + 1) * d, :])
    out_ref[...] = x1 + mod(5) * acc


def _outproj_mlp(mix, src, mod, g2, wo, w1, w2, layer, n_tiles, n_xt, seq_edges=None):
    d = D_MODEL
    tm = TOKEN_TILE
    row = lambda i: (i, 0)
    const = lambda i: (0, 0)
    mix_specs = [pl.BlockSpec((tm, d), row)] if seq_edges is None else _hyb_mix_specs(mix[0].shape[0], tm)
    return pl.pallas_call(
        functools.partial(_outproj_mlp_body, n_src=len(src), seq_edges=seq_edges),
        grid=(n_tiles,),
        in_specs=_src_specs(src, tm) + mix_specs + [
                  pl.BlockSpec((1, 1, 6 * d), lambda i: (i // n_xt, 0, 0)),
                  pl.BlockSpec((1, d), const),
                  pl.BlockSpec((d, d), const),
                  pl.BlockSpec((1, d, MLP_HIDDEN), lambda i: (layer, 0, 0)),
                  pl.BlockSpec((1, MLP_HIDDEN, d), lambda i: (layer, 0, 0))],
        out_specs=pl.BlockSpec((tm, d), row),
        out_shape=jax.ShapeDtypeStruct((n_tiles * tm, d), F32),
        compiler_params=_cparams("arbitrary"),
        name="outproj_mlp",
    )(*src, *mix, mod, g2, wo, w1, w2)


def _hyb_weights(w_in, gate_b):
    n_gate = 4 * ML_HEADS
    g0 = 2 * ML_HEADS * ML_DK + 2 * ML_HEADS * ML_DV
    w = jnp.concatenate([w_in[:, :g0], jnp.pad(w_in[:, g0:g0 + n_gate], ((0, 0), (0, LANE - n_gate))),
                         w_in[:, g0 + n_gate:]], axis=1).astype(BF16)
    gb = jnp.pad(gate_b.astype(F32), (0, LANE - n_gate)).reshape(1, LANE)
    return w, gb


def _rope_tables(t, n_ident):
    half = HEAD_DIM // 2
    inv = ROPE_THETA ** (-jnp.arange(0, half, 2, dtype=F32) / half)
    n_rows = t // GRID_W
    r = jnp.arange(n_rows, dtype=F32)[:, None] * inv
    c = jnp.arange(GRID_W, dtype=F32)[:, None] * inv
    z_r, z_c = jnp.zeros_like(r), jnp.zeros_like(c)
    lanes = lambda q0, q1, q2, q3: jnp.concatenate([q0, q1, q2, q3] * 2, axis=-1)
    outer = lambda by_row, by_col: (by_row[:, None, :] + by_col[None, :, :]).reshape(t, LANE)
    cos_t = outer(lanes(jnp.cos(r), jnp.cos(r), z_r, z_r), lanes(z_c, z_c, jnp.cos(c), jnp.cos(c)))
    sin_t = outer(lanes(-jnp.sin(r), jnp.sin(r), z_r, z_r), lanes(z_c, z_c, -jnp.sin(c), jnp.sin(c)))
    cos_t = jnp.concatenate([cos_t, jnp.ones((n_ident, LANE), F32)], axis=0)
    sin_t = jnp.concatenate([sin_t, jnp.zeros((n_ident, LANE), F32)], axis=0)
    return cos_t, sin_t


def kernel(x, c, ctx, c_ctx, ada_w, ada_b, norm1_g, norm2_g, mlp_w1, mlp_w2, hyb_w_in, hyb_gate_b, mlstm_norm_g,
           conv_w, hyb_w_out, att_w_in, q_norm_g, k_norm_g, att_w_out):
    nb, t, d = x.shape
    lc = ctx.shape[1]
    assert d == D_MODEL and nb * lc == TOKEN_TILE and lc == SEQ_BLOCK and t % TOKEN_TILE == 0
    assert nb + 1 <= SUBLANE
    n_xt = t // TOKEN_TILE
    n_tiles = nb * n_xt + 1

    src = (x.reshape(nb * t, d).astype(F32), ctx.reshape(nb * lc, d).astype(F32))
    cvec = jnp.concatenate([c, c_ctx[None, :], jnp.zeros((SUBLANE - nb - 1, d), c.dtype)], axis=0).astype(F32)
    mods = _adaln(cvec, ada_w.astype(F32), ada_b.astype(F32))

    w1_all, w2_all = mlp_w1.astype(BF16), mlp_w2.astype(BF16)
    cos_t, sin_t = _rope_tables(t, TOKEN_TILE)
    head_of = lambda axis: lax.broadcasted_iota(jnp.int32, (NORM_BLOCK, NORM_BLOCK), axis) // HEAD_DIM
    bd = (head_of(0) == head_of(1)).astype(BF16)
    tile_gain = lambda g: jnp.tile(g.astype(F32), NORM_BLOCK // HEAD_DIM).reshape(1, NORM_BLOCK)
    seq_starts = tuple(b * t for b in range(nb)) + tuple(nb * t + b * lc for b in range(nb))
    seq_ends = tuple((b + 1) * t - 1 for b in range(nb)) + tuple(nb * t + (b + 1) * lc - 1 for b in range(nb))

    for layer in range(DEPTH):
        last = layer == DEPTH - 1
        mod = mods[layer, :nb + 1].reshape(nb + 1, 1, 6 * d)
        g1 = norm1_g[layer].astype(F32).reshape(1, d)
        g2 = norm2_g[layer].astype(F32).reshape(1, d)
        if layer % 2 == 0:
            e = layer // 2
            w, gb = _hyb_weights(hyb_w_in[e], hyb_gate_b[e])
            q, k, vt, o, gate, cb, z = _hyb_in(src, mod, g1, w, gb, n_xt)
            hf, hb = _mlstm(q, k, vt, gate, nb, t)
            cw = jnp.pad(conv_w[e].astype(F32), ((0, SUBLANE - CONV_K), (0, 0)))
            mix = (hf, hb, o, cb, z, z, z, mlstm_norm_g[e].astype(F32).reshape(1, -1), cw)
            seq_edges = (seq_starts, seq_ends)
            wo = hyb_w_out[e].astype(BF16)
        else:
            a = layer // 2
            bound = (HEAD_DIM ** 0.5 * LOG2E * BOUND_MARGIN) * jnp.max(jnp.abs(q_norm_g[a])) * jnp.max(jnp.abs(k_norm_g[a]))
            bound = bound.astype(F32)
            lane_const = (jnp.arange(LANE) == CONST_LANE).astype(F32)
            pads = jnp.stack([-bound * lane_const, lane_const] + [0.0 * lane_const] * (SUBLANE - 2))
            q, k, vt = _att_in(src[0], mod, g1, att_w_in[a].astype(BF16), tile_gain(q_norm_g[a]),
                               tile_gain(k_norm_g[a]), cos_t, sin_t, bd, pads, n_xt)
            y = lax.cond(bound <= MAX_BOUND_LOG2,
                         lambda: _flash(q, k, vt, nb, t, lc, bounded=True),
                         lambda: _flash(q, k, vt, nb, t, lc, bounded=False))
            mix = (y,)
            seq_edges = None
            wo = att_w_out[a].astype(BF16)
        src = (_outproj_mlp(mix, src, mod, g2, wo, w1_all, w2_all, layer,
                            nb * n_xt if last else n_tiles, n_xt, seq_edges),)
    return src[0].reshape(nb, t, d).astype(x.dtype)
```
